```python
import jax, jax.numpy as jnp
from jax import lax
import numpy as np

D_MODEL = 1024
BATCH = 4
SEQ = 4096
DEPTH = 1
DEC_BATCH = 32
DEC_SEQ = 8
PAST_LEN = 8192
PAGE_SIZE = 128

SB_HEADS = 8
SB_HEAD_DIM = 64
SB_WIDTH = SB_HEADS * SB_HEAD_DIM
Q_BLOCK = 128
SB_BIAS_HI = -6.0
SB_BIAS_LO = -11.0
RET_HEADS = 4
RET_QK_DIM = 128
RET_V_DIM = 256
RET_QK_WIDTH = RET_HEADS * RET_QK_DIM
RET_V_WIDTH = RET_HEADS * RET_V_DIM
RET_CHUNK = 128
ROPE_BASE = 10000.0
D_FF = 2816
NORM_EPS = 1e-6
POOL_NUM = 5
POOL_DEN = 4
IN_WIDTH = 3 * SB_WIDTH + 2 * RET_QK_WIDTH + 2 * RET_V_WIDTH + 2 * D_MODEL

kernel_name = 'stickbreak_retention_hybrid_step'

F32 = jnp.float32


def rms_norm(x, g):
    xf = x.astype(F32)
    y = xf * lax.rsqrt(jnp.mean(xf * xf, axis=-1, keepdims=True) + NORM_EPS)
    return (y * g.astype(F32)).astype(x.dtype)


def swiglu(x, w_gu, w_down):
    g, u = jnp.split(x @ w_gu, 2, axis=-1)
    return (jax.nn.silu(g) * u) @ w_down


def rope(x, pos):
    half = x.shape[-1] // 2
    freq = ROPE_BASE ** (-jnp.arange(half, dtype=F32) / half)
    ang = pos.astype(F32)[:, None] * freq[None, :]
    cos = jnp.cos(ang)[None, :, None, :]
    sin = jnp.sin(ang)[None, :, None, :]
    xf = x.astype(F32)
    x1, x2 = xf[..., :half], xf[..., half:]
    return jnp.concatenate([x1 * cos - x2 * sin, x2 * cos + x1 * sin], axis=-1).astype(x.dtype)


def sb_block(q, k, v, q_pos, bias):
    z = jnp.einsum('bqhd,bkhd->bhqk', q.astype(F32), k.astype(F32)) * (SB_HEAD_DIM ** -0.5)
    z = z + bias.astype(F32)[None, :, None, None]
    k_pos = jnp.arange(k.shape[1])
    mask = k_pos[None, :] < q_pos[:, None]
    log_stay = jnp.where(mask, jax.nn.log_sigmoid(-z), 0.0)
    tail = lax.cumsum(log_stay, axis=3, reverse=True) - log_stay
    a = jnp.where(mask, jnp.exp(jax.nn.log_sigmoid(z) + tail), 0.0)
    return jnp.einsum('bhqk,bkhd->bqhd', a, v.astype(F32)).astype(q.dtype)


def stick_breaking(q, k, v, q_start, bias):
    B, Tq, H, D = q.shape
    qb = Q_BLOCK if Tq % Q_BLOCK == 0 else Tq
    nb = Tq // qb
    q_blocks = q.reshape(B, nb, qb, H, D).transpose(1, 0, 2, 3, 4)
    pos = (q_start + jnp.arange(Tq)).reshape(nb, qb)
    out = lax.map(lambda a: sb_block(a[0], k, v, a[1], bias), (q_blocks, pos))
    return out.transpose(1, 0, 2, 3, 4).reshape(B, Tq, H, D)


def retention(q, k, v, state):
    B, T, H, DK = q.shape
    DV = v.shape[-1]
    C = RET_CHUNK if T % RET_CHUNK == 0 else T
    n = T // C
    log_gamma = jnp.log1p(-jnp.exp2(-5.0 - jnp.arange(H, dtype=F32)))
    idx = jnp.arange(C, dtype=F32)
    diff = idx[:, None] - idx[None, :]
    decay_in = jnp.where(diff >= 0, jnp.exp(log_gamma[:, None, None] * jnp.maximum(diff, 0.0)), 0.0)
    decay_q = jnp.exp(log_gamma[:, None] * (idx + 1.0))[None, :, :, None]
    decay_k = jnp.exp(log_gamma[:, None] * (C - 1.0 - idx))[None, :, :, None]
    decay_c = jnp.exp(log_gamma * C)[None, :, None, None]

    def chunks(a):
        return a.astype(F32).reshape(B, n, C, H, a.shape[-1]).transpose(1, 0, 3, 2, 4)

    qc, kc, vc = chunks(q), chunks(k * (DK ** -0.5)), chunks(v)

    def step(S, inp):
        qi, ki, vi = inp
        inner = jnp.einsum('bhcd,bhsd->bhcs', qi, ki) * decay_in[None]
        o = jnp.einsum('bhcs,bhsv->bhcv', inner, vi) + jnp.einsum('bhcd,bhdv->bhcv', qi, S) * decay_q
        S = S * decay_c + jnp.einsum('bhsd,bhsv->bhdv', ki * decay_k, vi)
        return S, o

    S, o = lax.scan(step, state.astype(F32), (qc, kc, vc))
    o = o.transpose(1, 0, 3, 2, 4).reshape(B, T, H, DV)
    return o.astype(v.dtype), S.astype(state.dtype)


def mixer(u, k_past, v_past, ret_state, q_start, w_in, sb_bias, ret_gn_g, w_sb_out, w_ret_out, w_o):
    B, T, _ = u.shape
    sizes = [SB_WIDTH] * 3 + [RET_QK_WIDTH] * 2 + [RET_V_WIDTH] * 2 + [D_MODEL] * 2
    cuts = [int(c) for c in np.cumsum(sizes)[:-1]]
    q_sb, k_sb, v_sb, q_r, k_r, v_r, g_r, a_sb, a_r = jnp.split(u @ w_in, cuts, axis=-1)
    pos = q_start + jnp.arange(T)
    q_sb = q_sb.reshape(B, T, SB_HEADS, SB_HEAD_DIM)
    k_sb = k_sb.reshape(B, T, SB_HEADS, SB_HEAD_DIM)
    v_sb = v_sb.reshape(B, T, SB_HEADS, SB_HEAD_DIM)
    if k_past is None:
        keys, vals = k_sb, v_sb
    else:
        keys = jnp.concatenate([k_past.astype(k_sb.dtype), k_sb], axis=1)
        vals = jnp.concatenate([v_past.astype(v_sb.dtype), v_sb], axis=1)
    o_sb = stick_breaking(q_sb, keys, vals, q_start, sb_bias).reshape(B, T, SB_WIDTH)
    q_r = rope(q_r.reshape(B, T, RET_HEADS, RET_QK_DIM), pos)
    k_r = rope(k_r.reshape(B, T, RET_HEADS, RET_QK_DIM), pos)
    v_r = v_r.reshape(B, T, RET_HEADS, RET_V_DIM)
    o_r, new_state = retention(q_r, k_r, v_r, ret_state)
    o_rf = o_r.astype(F32)
    o_rf = o_rf * lax.rsqrt(jnp.mean(o_rf * o_rf, axis=-1, keepdims=True) + NORM_EPS)
    o_r = (o_rf.reshape(B, T, RET_V_WIDTH) * ret_gn_g.astype(F32)).astype(u.dtype)
    o_r = jax.nn.silu(g_r) * o_r
    m = jax.nn.sigmoid(a_sb) * (o_sb @ w_sb_out) + jax.nn.sigmoid(a_r) * (o_r @ w_ret_out)
    return m @ w_o, k_sb, v_sb, new_state


def layer(x, k_past, v_past, ret_state, q_start, p):
    (g_f1_pre, w_f1_gu, w_f1_down, g_f1_post, g_m_pre, w_in, sb_bias, ret_gn_g, w_sb_out, w_ret_out, w_o,
     g_m_post, g_f2_pre, w_f2_gu, w_f2_down, g_f2_post) = p
    h = x + 0.5 * rms_norm(swiglu(rms_norm(x, g_f1_pre), w_f1_gu, w_f1_down), g_f1_post)
    mix, k_new, v_new, s_new = mixer(rms_norm(h, g_m_pre), k_past, v_past, ret_state, q_start,
                                     w_in, sb_bias, ret_gn_g, w_sb_out, w_ret_out, w_o)
    h = h + rms_norm(mix, g_m_post)
    h = h + 0.5 * rms_norm(swiglu(rms_norm(h, g_f2_pre), w_f2_gu, w_f2_down), g_f2_post)
    return h, k_new, v_new, s_new


def setup_inputs(seed: int = 0) -> dict:
    key = jax.random.key(seed)
    ks = jax.random.split(key, 32)
    n_pages = PAST_LEN // PAGE_SIZE
    n_used = DEC_BATCH * n_pages
    n_pool = (n_used * POOL_NUM) // POOL_DEN

    def w(k, shape, fan_in):
        return jax.random.normal(k, shape, F32) * (fan_in ** -0.5)

    def gain(k, dim):
        return 1.0 + 0.02 * jax.random.normal(k, (DEPTH, dim), F32)

    page_table = jax.random.permutation(ks[5], n_pool)[:n_used].reshape(DEC_BATCH, n_pages).astype(jnp.int32)
    sb_bias = (jnp.linspace(SB_BIAS_HI, SB_BIAS_LO, SB_HEADS, dtype=F32)[None, :]
               + 0.1 * jax.random.normal(ks[21], (DEPTH, SB_HEADS), F32))
    return {
        'x_prompt': jax.random.normal(ks[0], (BATCH, SEQ, D_MODEL), F32),
        'x_sample': jax.random.normal(ks[1], (DEC_BATCH, DEC_SEQ, D_MODEL), F32),
        'cache_k': jax.random.normal(ks[2], (DEPTH, n_pool, PAGE_SIZE, SB_HEADS, SB_HEAD_DIM), F32),
        'cache_v': jax.random.normal(ks[3], (DEPTH, n_pool, PAGE_SIZE, SB_HEADS, SB_HEAD_DIM), F32),
        'state_ret': jax.random.normal(ks[4], (DEPTH, DEC_BATCH, RET_HEADS, RET_QK_DIM, RET_V_DIM), F32),
        'page_table': page_table,
        'g_ffn1_pre': gain(ks[6], D_MODEL),
        'w_ffn1_gu': w(ks[7], (DEPTH, D_MODEL, 2 * D_FF), D_MODEL),
        'w_ffn1_down': w(ks[8], (DEPTH, D_FF, D_MODEL), D_FF),
        'g_ffn1_post': gain(ks[9], D_MODEL),
        'g_mix_pre': gain(ks[10], D_MODEL),
        'w_in': w(ks[11], (DEPTH, D_MODEL, IN_WIDTH), D_MODEL),
        'sb_bias': sb_bias,
        'ret_gn_g': gain(ks[12], RET_V_WIDTH),
        'w_sb_out': w(ks[13], (DEPTH, SB_WIDTH, D_MODEL), SB_WIDTH),
        'w_ret_out': w(ks[14], (DEPTH, RET_V_WIDTH, D_MODEL), RET_V_WIDTH),
        'w_o': w(ks[15], (DEPTH, D_MODEL, D_MODEL), D_MODEL),
        'g_mix_post': gain(ks[16], D_MODEL),
        'g_ffn2_pre': gain(ks[17], D_MODEL),
        'w_ffn2_gu': w(ks[18], (DEPTH, D_MODEL, 2 * D_FF), D_MODEL),
        'w_ffn2_down': w(ks[19], (DEPTH, D_FF, D_MODEL), D_FF),
        'g_ffn2_post': gain(ks[20], D_MODEL),
    }


def reference(x_prompt, x_sample, cache_k, cache_v, state_ret, page_table,
              g_ffn1_pre, w_ffn1_gu, w_ffn1_down, g_ffn1_post, g_mix_pre, w_in, sb_bias, ret_gn_g,
              w_sb_out, w_ret_out, w_o, g_mix_post, g_ffn2_pre, w_ffn2_gu, w_ffn2_down, g_ffn2_post):
    n_seq, n_pages = page_table.shape
    past_len = n_pages * PAGE_SIZE
    hp, hs = x_prompt, x_sample
    kp_l, vp_l, sp_l, ks_l, vs_l, ss_l = [], [], [], [], [], []
    for l in range(DEPTH):
        p = (g_ffn1_pre[l], w_ffn1_gu[l], w_ffn1_down[l], g_ffn1_post[l], g_mix_pre[l], w_in[l],
             sb_bias[l], ret_gn_g[l], w_sb_out[l], w_ret_out[l], w_o[l], g_mix_post[l], g_ffn2_pre[l],
             w_ffn2_gu[l], w_ffn2_down[l], g_ffn2_post[l])
        s0 = jnp.zeros((hp.shape[0], RET_HEADS, RET_QK_DIM, RET_V_DIM), state_ret.dtype)
        hp, kp, vp, sp = layer(hp, None, None, s0, 0, p)
        k_past = cache_k[l][page_table].reshape(n_seq, past_len, SB_HEADS, SB_HEAD_DIM)
        v_past = cache_v[l][page_table].reshape(n_seq, past_len, SB_HEADS, SB_HEAD_DIM)
        hs, kn, vn, sn = layer(hs, k_past, v_past, state_ret[l], past_len, p)
        kp_l.append(kp); vp_l.append(vp); sp_l.append(sp)
        ks_l.append(kn); vs_l.append(vn); ss_l.append(sn)
    return (hp, hs, jnp.stack(kp_l), jnp.stack(vp_l), jnp.stack(sp_l),
            jnp.stack(ks_l), jnp.stack(vs_l), jnp.stack(ss_l))
```

```python
import functools

import jax
import jax.numpy as jnp
from jax import lax
from jax.experimental import pallas as pl
from jax.experimental.pallas import tpu as pltpu

F32 = jnp.float32
BF16 = jnp.bfloat16

SB_HEADS = 8
SB_HEAD_DIM = 64
SB_WIDTH = SB_HEADS * SB_HEAD_DIM
RET_HEADS = 4
RET_QK_DIM = 128
RET_V_DIM = 256
RET_QK_WIDTH = RET_HEADS * RET_QK_DIM
RET_V_WIDTH = RET_HEADS * RET_V_DIM
RET_CHUNK = 128
ROPE_BASE = 10000.0
NORM_EPS = 1e-6

LANES = 128
HEADS_PER_LANE_TILE = LANES // SB_HEAD_DIM

ROW_TILE = 256
SB_BLOCK = 256
DEC_PAGES_PER_STEP = 8
RET_ROWS_PER_STEP = 512
VMEM_LIMIT = 56 * 1024 * 1024

_NT = (((1,), (1,)), ((), ()))


def _const_spec(shape):
    nd = len(shape)
    return pl.BlockSpec(shape, lambda *_: (0,) * nd, pipeline_mode=pl.Buffered(1))


def _params(n_axes, vmem=VMEM_LIMIT):
    return pltpu.CompilerParams(dimension_semantics=("arbitrary",) * n_axes,
                                vmem_limit_bytes=vmem)


def _rms(x, g):
    ms = jnp.mean(x * x, axis=-1, keepdims=True)
    return x * lax.rsqrt(ms + NORM_EPS) * g


def _silu(x):
    return x * jax.nn.sigmoid(x)


def _ffn_residual(x, g_pre, wgu_ref, wdown_ref, g_post):
    d_ff = wdown_ref.shape[0]
    xn = _rms(x, g_pre).astype(BF16)
    gate = jnp.dot(xn, wgu_ref[:, :d_ff], preferred_element_type=F32)
    up = jnp.dot(xn, wgu_ref[:, d_ff:], preferred_element_type=F32)
    act = (_silu(gate) * up).astype(BF16)
    y = jnp.dot(act, wdown_ref[...], preferred_element_type=F32)
    return x + 0.5 * _rms(y, g_post)


def _ffn_kernel(x_ref, gpre_ref, wgu_ref, wdown_ref, gpost_ref, o_ref):
    o_ref[...] = _ffn_residual(x_ref[...], gpre_ref[...], wgu_ref, wdown_ref, gpost_ref[...])


def _ffn_call(x, g_pre, wgu, wdown, g_post):
    n, d = x.shape
    row = pl.BlockSpec((ROW_TILE, d), lambda i: (i, 0))
    return pl.pallas_call(
        _ffn_kernel,
        grid=(n // ROW_TILE,),
        in_specs=[row, _const_spec(g_pre.shape), _const_spec(wgu.shape),
                  _const_spec(wdown.shape), _const_spec(g_post.shape)],
        out_specs=row,
        out_shape=jax.ShapeDtypeStruct((n, d), F32),
        compiler_params=_params(1),
        name="ffn_block",
    )(x, g_pre, wgu, wdown, g_post)


_OFF_QSB = 0
_OFF_KSB = _OFF_QSB + SB_WIDTH
_OFF_VSB = _OFF_KSB + SB_WIDTH
_OFF_QR = _OFF_VSB + SB_WIDTH
_OFF_KR = _OFF_QR + RET_QK_WIDTH
_OFF_VR = _OFF_KR + RET_QK_WIDTH
_OFF_GR = _OFF_VR + RET_V_WIDTH
_OFF_ASB = _OFF_GR + RET_V_WIDTH


def _inproj_shared(u, win_ref, cos_ref, sin_ref, qsb_ref, qr_ref, kr_ref, vr_ref, gr_ref, asb_ref, ar_ref):
    def proj(lo, width):
        return jnp.dot(u, win_ref[:, lo:lo + width], preferred_element_type=F32)

    qsb_ref[...] = proj(_OFF_QSB, SB_WIDTH) * (SB_HEAD_DIM ** -0.5)
    cos = cos_ref[...]
    sin = sin_ref[...]
    q_r = proj(_OFF_QR, RET_QK_WIDTH)
    k_r = proj(_OFF_KR, RET_QK_WIDTH)
    for hh in range(RET_HEADS):
        sl = slice(hh * RET_QK_DIM, (hh + 1) * RET_QK_DIM)
        qh = q_r[:, sl]
        kh = k_r[:, sl]
        qr_ref[:, sl] = qh * cos + pltpu.roll(qh, RET_QK_DIM // 2, axis=1) * sin
        kr_ref[:, sl] = (kh * cos + pltpu.roll(kh, RET_QK_DIM // 2, axis=1) * sin) * (RET_QK_DIM ** -0.5)
    vr_ref[...] = proj(_OFF_VR, RET_V_WIDTH).astype(BF16)
    gr_ref[...] = proj(_OFF_GR, RET_V_WIDTH)
    d_model = asb_ref.shape[1]
    asb_ref[...] = proj(_OFF_ASB, d_model)
    ar_ref[...] = proj(_OFF_ASB + d_model, d_model)


def _inproj_prompt_kernel(h_ref, g_ref, win_ref, wkvt_ref, cos_ref, sin_ref,
                          kt_ref, vt_ref, ktb_ref, vb_ref, *shared_refs):
    u = _rms(h_ref[...], g_ref[...]).astype(BF16)
    kv_t = lax.dot_general(wkvt_ref[...], u, _NT, preferred_element_type=F32)
    kt_ref[0] = kv_t[:SB_WIDTH]
    vt_ref[0] = kv_t[SB_WIDTH:]
    ktb_ref[0, 0] = kv_t[:SB_WIDTH].astype(BF16)
    vb_ref[...] = jnp.dot(u, win_ref[:, _OFF_VSB:_OFF_VSB + SB_WIDTH],
                          preferred_element_type=F32).astype(BF16)
    _inproj_shared(u, win_ref, cos_ref, sin_ref, *shared_refs)


def _inproj_decode_kernel(h_ref, g_ref, win_ref, cos_ref, sin_ref, k_ref, v_ref, *shared_refs):
    u = _rms(h_ref[...], g_ref[...]).astype(BF16)
    k_ref[...] = jnp.dot(u, win_ref[:, _OFF_KSB:_OFF_KSB + SB_WIDTH], preferred_element_type=F32)
    v_ref[...] = jnp.dot(u, win_ref[:, _OFF_VSB:_OFF_VSB + SB_WIDTH], preferred_element_type=F32)
    _inproj_shared(u, win_ref, cos_ref, sin_ref, *shared_refs)


def _row_spec(width):
    return pl.BlockSpec((ROW_TILE, width), lambda i: (i, 0))


def _shared_outputs(n, d):
    widths_dtypes = [(SB_WIDTH, F32), (RET_QK_WIDTH, F32), (RET_QK_WIDTH, F32), (RET_V_WIDTH, BF16),
                     (RET_V_WIDTH, F32), (d, F32), (d, F32)]
    return ([_row_spec(w) for w, _ in widths_dtypes],
            [jax.ShapeDtypeStruct((n, w), dt) for w, dt in widths_dtypes])


def _inproj_prompt_call(h, g_pre, w_in, w_kv_t, cos_tab, sin_tab, batch, seq):
    n, d = h.shape
    assert ROW_TILE == SB_BLOCK
    tiles = seq // ROW_TILE
    tab = pl.BlockSpec((ROW_TILE, RET_QK_DIM), lambda i: (i % tiles, 0))
    t_spec = pl.BlockSpec((1, SB_WIDTH, ROW_TILE), lambda i: (i // tiles, 0, i % tiles))
    shared_specs, shared_shapes = _shared_outputs(n, d)
    return pl.pallas_call(
        _inproj_prompt_kernel,
        grid=(n // ROW_TILE,),
        in_specs=[_row_spec(d), _const_spec(g_pre.shape), _const_spec(w_in.shape),
                  _const_spec(w_kv_t.shape), tab, tab],
        out_specs=[t_spec, t_spec,
                   pl.BlockSpec((1, 1, SB_WIDTH, SB_BLOCK), lambda i: (i // tiles, i % tiles, 0, 0)),
                   _row_spec(SB_WIDTH)] + shared_specs,
        out_shape=[jax.ShapeDtypeStruct((batch, SB_WIDTH, seq), F32),
                   jax.ShapeDtypeStruct((batch, SB_WIDTH, seq), F32),
                   jax.ShapeDtypeStruct((batch, tiles, SB_WIDTH, SB_BLOCK), BF16),
                   jax.ShapeDtypeStruct((n, SB_WIDTH), BF16)] + shared_shapes,
        compiler_params=_params(1),
        name="in_projection_prompt",
    )(h, g_pre, w_in, w_kv_t, cos_tab, sin_tab)


def _inproj_decode_call(h, g_pre, w_in, cos_tab, sin_tab):
    n, d = h.shape
    tab = pl.BlockSpec((ROW_TILE, RET_QK_DIM), lambda i: (0, 0))
    shared_specs, shared_shapes = _shared_outputs(n, d)
    return pl.pallas_call(
        _inproj_decode_kernel,
        grid=(n // ROW_TILE,),
        in_specs=[_row_spec(d), _const_spec(g_pre.shape), _const_spec(w_in.shape), tab, tab],
        out_specs=[_row_spec(SB_WIDTH), _row_spec(SB_WIDTH)] + shared_specs,
        out_shape=[jax.ShapeDtypeStruct((n, SB_WIDTH), F32),
                   jax.ShapeDtypeStruct((n, SB_WIDTH), F32)] + shared_shapes,
        compiler_params=_params(1),
        name="in_projection_decode",
    )(h, g_pre, w_in, cos_tab, sin_tab)


def _softplus(z):
    return jnp.maximum(z, 0.0) + jnp.log(1.0 + jnp.exp(-jnp.abs(z)))


def _sb_weights(z, tri, run, mask):
    sp = _softplus(z)
    if mask is not None:
        sp = jnp.where(mask, sp, 0.0)
    hi = sp.astype(BF16)
    lo = (sp - hi.astype(F32)).astype(BF16)
    csum = (jnp.dot(hi, tri, preferred_element_type=F32)
            + jnp.dot(lo, tri, preferred_element_type=F32) + run)
    a = jnp.exp(z - csum)
    if mask is not None:
        a = jnp.where(mask, a, 0.0)
    return a.astype(BF16), run + jnp.sum(sp, axis=-1, keepdims=True)


def _sb_prompt_kernel(bias_ref, q_ref, kt_ref, v_ref, tri_ref, o_ref):
    hp = pl.program_id(1)
    qi = pl.program_id(2)
    blk = q_ref.shape[1]
    q = q_ref[0]
    tri = tri_ref[...]
    lane_head = lax.broadcasted_iota(jnp.int32, (1, LANES), 1) // SB_HEAD_DIM
    row = lax.broadcasted_iota(jnp.int32, (blk, blk), 0)
    col = lax.broadcasted_iota(jnp.int32, (blk, blk), 1)
    causal = col < row

    outs = []
    for hh in range(HEADS_PER_LANE_TILE):
        q_h = jnp.where(lane_head == hh, q, 0.0).astype(BF16)
        bias = bias_ref[hp * HEADS_PER_LANE_TILE + hh]

        def visit(j, carry, mask, q_h=q_h, bias=bias):
            run, acc = carry
            start = pl.multiple_of(j * blk, blk)
            z = jnp.dot(q_h, kt_ref[0, j], preferred_element_type=F32) + bias
            a, run = _sb_weights(z, tri, run, mask)
            acc = acc + jnp.dot(a, v_ref[0, pl.ds(start, blk), :], preferred_element_type=F32)
            return run, acc

        carry = (jnp.zeros((blk, 1), F32), jnp.zeros((blk, LANES), F32))
        carry = visit(qi, carry, causal)
        carry = lax.fori_loop(0, qi, lambda i, c: visit(qi - 1 - i, c, None), carry)
        outs.append(carry[1])

    out = outs[0]
    for hh in range(1, HEADS_PER_LANE_TILE):
        out = jnp.where(lane_head == hh, outs[hh], out)
    o_ref[0] = out.astype(o_ref.dtype)


def _sb_prompt_call(q, kt_bf, v_bf, bias, tri):
    b, t, w = q.shape
    blk = SB_BLOCK
    q_spec = pl.BlockSpec((1, blk, LANES), lambda bi, hp, qi: (bi, qi, hp))
    return pl.pallas_call(
        _sb_prompt_kernel,
        grid=(b, w // LANES, t // blk),
        in_specs=[pl.BlockSpec(memory_space=pltpu.SMEM), q_spec,
                  pl.BlockSpec((1, t // blk, LANES, blk), lambda bi, hp, qi: (bi, 0, hp, 0)),
                  pl.BlockSpec((1, t, LANES), lambda bi, hp, qi: (bi, 0, hp)),
                  _const_spec(tri.shape)],
        out_specs=q_spec,
        out_shape=jax.ShapeDtypeStruct((b, t, w), BF16),
        compiler_params=_params(3),
        name="sb_attention_prompt",
    )(bias, q, kt_bf, v_bf, tri)


def _sb_decode_kernel(pt_ref, q_ref, kn_ref, vn_ref, bias_ref, tri_ref, *rest, pages):
    del pt_ref
    k_refs = rest[:pages]
    v_refs = rest[pages:2 * pages]
    o_ref = rest[2 * pages]
    qbd_ref, run_ref, acc_ref = rest[2 * pages + 1:]
    g = pl.program_id(1)
    dec_t = q_ref.shape[1]
    rows = SB_HEADS * dec_t
    blk = tri_ref.shape[0]
    bias = bias_ref[...]
    tri = tri_ref[...]
    lane_head = lax.broadcasted_iota(jnp.int32, (1, SB_WIDTH), 1) // SB_HEAD_DIM

    @pl.when(g == 0)
    def _():
        q = q_ref[0]
        qbd = jnp.concatenate([jnp.where(lane_head == hh, q, 0.0) for hh in range(SB_HEADS)], axis=0)
        qbd_ref[...] = qbd.astype(BF16)
        pad = jnp.zeros((blk - dec_t, SB_WIDTH), F32)
        k_new = jnp.concatenate([kn_ref[0], pad], axis=0).astype(BF16)
        v_new = jnp.concatenate([vn_ref[0], pad], axis=0).astype(BF16)
        t_row = lax.broadcasted_iota(jnp.int32, (rows, blk), 0) % dec_t
        col = lax.broadcasted_iota(jnp.int32, (rows, blk), 1)
        z = lax.dot_general(qbd_ref[...], k_new, _NT, preferred_element_type=F32) + bias
        a, run = _sb_weights(z, tri, jnp.zeros((rows, 1), F32), col < t_row)
        run_ref[...] = jnp.broadcast_to(run, run_ref.shape)
        acc_ref[...] = jnp.dot(a, v_new, preferred_element_type=F32)

    qbd = qbd_ref[...]
    run = run_ref[:, 0:1]
    acc = acc_ref[...]
    pages_per_blk = blk // k_refs[0].shape[2]
    for p in reversed(range(pages // pages_per_blk)):
        sl = slice(p * pages_per_blk, (p + 1) * pages_per_blk)
        k_t = jnp.concatenate([r[0] for r in k_refs[sl]], axis=1).astype(BF16)
        v_t = jnp.concatenate([r[0] for r in v_refs[sl]], axis=1).astype(BF16)
        z = jnp.dot(qbd, k_t, preferred_element_type=F32) + bias
        a, run = _sb_weights(z, tri, run, None)
        acc = acc + lax.dot_general(a, v_t, _NT, preferred_element_type=F32)
    run_ref[...] = jnp.broadcast_to(run, run_ref.shape)
    acc_ref[...] = acc

    @pl.when(g == pl.num_programs(1) - 1)
    def _():
        out = jnp.zeros((dec_t, SB_WIDTH), F32)
        for hh in range(SB_HEADS):
            out = jnp.where(lane_head == hh, acc[hh * dec_t:(hh + 1) * dec_t, :], out)
        o_ref[0] = out


def _sb_decode_call(q, k_new, v_new, cache_kt, cache_vt, page_table, bias_rows, tri):
    n_seq, dec_t, w = q.shape
    n_pages = page_table.shape[1]
    pages = DEC_PAGES_PER_STEP
    page = cache_kt.shape[2]
    rows = SB_HEADS * dec_t
    seq_spec = pl.BlockSpec((1, dec_t, w), lambda s, g, pt: (s, 0, 0))

    def page_spec(i):
        return pl.BlockSpec((1, w, page), lambda s, g, pt: (pt[s, n_pages - (g + 1) * pages + i], 0, 0))

    def const(shape):
        nd = len(shape)
        return pl.BlockSpec(shape, lambda s, g, pt: (0,) * nd)

    grid_spec = pltpu.PrefetchScalarGridSpec(
        num_scalar_prefetch=1,
        grid=(n_seq, n_pages // pages),
        in_specs=([seq_spec, seq_spec, seq_spec, const(bias_rows.shape), const(tri.shape)]
                  + [page_spec(i) for i in range(pages)] * 2),
        out_specs=seq_spec,
        scratch_shapes=[pltpu.VMEM((rows, w), BF16), pltpu.VMEM((rows, LANES), F32),
                        pltpu.VMEM((rows, w), F32)],
    )
    return pl.pallas_call(
        functools.partial(_sb_decode_kernel, pages=pages),
        grid_spec=grid_spec,
        out_shape=jax.ShapeDtypeStruct((n_seq, dec_t, w), F32),
        compiler_params=_params(2),
        name="sb_attention_decode",
    )(page_table, q, k_new, v_new, bias_rows, tri, *([cache_kt] * pages), *([cache_vt] * pages))


def _head_norm_gate(o, gn, gate):
    o = o * lax.rsqrt(jnp.mean(o * o, axis=-1, keepdims=True) + NORM_EPS) * gn
    return (_silu(gate) * o).astype(BF16)


def _ret_prompt_kernel(dc_ref, q_ref, k_ref, v_ref, g_ref, gn_ref, din_ref, dq_ref, dk_ref,
                       o_ref, s_out_ref, s_ref):
    hh = pl.program_id(1)
    c = pl.program_id(2)

    @pl.when(c == 0)
    def _():
        s_ref[...] = jnp.zeros_like(s_ref)

    dc = dc_ref[hh]
    din = din_ref[0]
    dq = dq_ref[0]
    dk = dk_ref[0]
    gn = gn_ref[...]
    state = s_ref[...]
    for i in range(q_ref.shape[1] // RET_CHUNK):
        sl = slice(i * RET_CHUNK, (i + 1) * RET_CHUNK)
        q = q_ref[0, sl, :].astype(BF16)
        k = k_ref[0, sl, :]
        v = v_ref[0, sl, :]
        inner = lax.dot_general(q, k.astype(BF16), _NT, preferred_element_type=F32) * din
        o = (jnp.dot(inner.astype(BF16), v, preferred_element_type=F32)
             + jnp.dot(q, state.astype(BF16), preferred_element_type=F32) * dq)
        kd_t = (k * dk).T.astype(BF16)
        state = state * dc + jnp.dot(kd_t, v, preferred_element_type=F32)
        o_ref[0, sl, :] = _head_norm_gate(o, gn, g_ref[0, sl, :])
    s_ref[...] = state

    @pl.when(c == pl.num_programs(2) - 1)
    def _():
        s_out_ref[0, 0] = state


def _ret_prompt_call(q, k, v_bf, gate, gn, tables):
    b, t, _ = q.shape
    din, dq, dk, dc = tables
    rows = RET_ROWS_PER_STEP
    qk_spec = pl.BlockSpec((1, rows, RET_QK_DIM), lambda bi, h, c: (bi, c, h))
    v_spec = pl.BlockSpec((1, rows, RET_V_DIM), lambda bi, h, c: (bi, c, h))

    def head(shape):
        return pl.BlockSpec((1,) + shape, lambda bi, h, c: (h, 0, 0))

    return pl.pallas_call(
        _ret_prompt_kernel,
        grid=(b, RET_HEADS, t // rows),
        in_specs=[pl.BlockSpec(memory_space=pltpu.SMEM), qk_spec, qk_spec, v_spec, v_spec,
                  pl.BlockSpec((1, RET_V_DIM), lambda bi, h, c: (0, h)),
                  head((RET_CHUNK, RET_CHUNK)), head((RET_CHUNK, RET_V_DIM)),
                  head((RET_CHUNK, RET_QK_DIM))],
        out_specs=[v_spec,
                   pl.BlockSpec((1, 1, RET_QK_DIM, RET_V_DIM), lambda bi, h, c: (bi, h, 0, 0))],
        out_shape=[jax.ShapeDtypeStruct((b, t, RET_V_WIDTH), BF16),
                   jax.ShapeDtypeStruct((b, RET_HEADS, RET_QK_DIM, RET_V_DIM), F32)],
        scratch_shapes=[pltpu.VMEM((RET_QK_DIM, RET_V_DIM), F32)],
        compiler_params=_params(3),
        name="retention_prompt",
    )(dc, q, k, v_bf, gate, gn, din, dq, dk)


def _ret_decode_kernel(dc_ref, q_ref, k_ref, v_ref, g_ref, gn_ref, s_in_ref, din_ref, dq_ref, dk_ref,
                       o_ref, s_out_ref, *, dec_t):
    hh = pl.program_id(0)
    n_seq = s_in_ref.shape[0]
    rows = n_seq * dec_t
    q = q_ref[...]
    k = k_ref[...]
    v = v_ref[...]
    q_bf = q.astype(BF16)
    inner = lax.dot_general(q_bf, k.astype(BF16), _NT, preferred_element_type=F32) * din_ref[0]
    o = jnp.dot(inner.astype(BF16), v, preferred_element_type=F32)

    s_old = s_in_ref[:, 0].reshape(n_seq * RET_QK_DIM, RET_V_DIM)
    row_seq = lax.broadcasted_iota(jnp.int32, (rows, RET_QK_DIM), 0) // dec_t
    q_bd = jnp.concatenate([jnp.where(row_seq == s, q, 0.0) for s in range(n_seq)], axis=1)
    o = o + jnp.dot(q_bd.astype(BF16), s_old.astype(BF16), preferred_element_type=F32) * dq_ref[0]
    o_ref[...] = _head_norm_gate(o, gn_ref[...], g_ref[...])

    kd_t = (k * dk_ref[0]).T
    col_seq = lax.broadcasted_iota(jnp.int32, (RET_QK_DIM, rows), 1) // dec_t
    k_bd_t = jnp.concatenate([jnp.where(col_seq == s, kd_t, 0.0) for s in range(n_seq)], axis=0)
    s_new = s_old * dc_ref[hh] + jnp.dot(k_bd_t.astype(BF16), v, preferred_element_type=F32)
    s_out_ref[:, 0] = s_new.reshape(n_seq, RET_QK_DIM, RET_V_DIM)


def _ret_decode_call(q, k, v_bf, gate, gn, state, tables, dec_t):
    rows = q.shape[0]
    n_seq = state.shape[0]
    din, dq, dk, dc = tables
    qk_spec = pl.BlockSpec((rows, RET_QK_DIM), lambda h: (0, h))
    v_spec = pl.BlockSpec((rows, RET_V_DIM), lambda h: (0, h))
    s_spec = pl.BlockSpec((n_seq, 1, RET_QK_DIM, RET_V_DIM), lambda h: (0, h, 0, 0))

    def head(shape):
        return pl.BlockSpec((1,) + shape, lambda h: (h, 0, 0))

    return pl.pallas_call(
        functools.partial(_ret_decode_kernel, dec_t=dec_t),
        grid=(RET_HEADS,),
        in_specs=[pl.BlockSpec(memory_space=pltpu.SMEM), qk_spec, qk_spec, v_spec, v_spec,
                  pl.BlockSpec((1, RET_V_DIM), lambda h: (0, h)), s_spec,
                  head((rows, rows)), head((rows, RET_V_DIM)), head((rows, RET_QK_DIM))],
        out_specs=[v_spec, s_spec],
        out_shape=[jax.ShapeDtypeStruct((rows, RET_V_WIDTH), BF16),
                   jax.ShapeDtypeStruct(state.shape, F32)],
        compiler_params=_params(1),
        name="retention_decode",
    )(dc, q, k, v_bf, gate, gn, state, din, dq, dk)


def _post_kernel(h_ref, osb_ref, or_ref, asb_ref, ar_ref, wsb_ref, wret_ref, wo_ref, gmix_ref,
                 gpre_ref, wgu_ref, wdown_ref, gpost_ref, o_ref):
    m = (jax.nn.sigmoid(asb_ref[...]) * jnp.dot(osb_ref[...], wsb_ref[...], preferred_element_type=F32)
         + jax.nn.sigmoid(ar_ref[...]) * jnp.dot(or_ref[...], wret_ref[...], preferred_element_type=F32))
    mix = jnp.dot(m.astype(BF16), wo_ref[...], preferred_element_type=F32)
    h = h_ref[...] + _rms(mix, gmix_ref[...])
    o_ref[...] = _ffn_residual(h, gpre_ref[...], wgu_ref, wdown_ref, gpost_ref[...])


def _post_call(h, o_sb, o_r, a_sb, a_r, w_sb, w_ret, w_o, g_mix, g_pre, wgu, wdown, g_post):
    n, d = h.shape
    consts = [w_sb, w_ret, w_o, g_mix, g_pre, wgu, wdown, g_post]
    return pl.pallas_call(
        _post_kernel,
        grid=(n // ROW_TILE,),
        in_specs=[_row_spec(d), _row_spec(SB_WIDTH), _row_spec(RET_V_WIDTH), _row_spec(d), _row_spec(d)]
                 + [_const_spec(c.shape) for c in consts],
        out_specs=_row_spec(d),
        out_shape=jax.ShapeDtypeStruct((n, d), F32),
        compiler_params=_params(1),
        name="merge_out_ffn",
    )(h, o_sb, o_r, a_sb, a_r, *consts)


def _rope_tables(pos):
    half = RET_QK_DIM // 2
    freq = ROPE_BASE ** (-jnp.arange(half, dtype=F32) / half)
    ang = pos.astype(F32)[:, None] * freq[None, :]
    cos, sin = jnp.cos(ang), jnp.sin(ang)
    return jnp.concatenate([cos, cos], axis=1), jnp.concatenate([-sin, sin], axis=1)


def _decay_tables(chunk, reps):
    log_gamma = jnp.log1p(-jnp.exp2(-5.0 - jnp.arange(RET_HEADS, dtype=F32)))
    idx = jnp.arange(chunk, dtype=F32)
    diff = idx[:, None] - idx[None, :]
    d_in = jnp.where(diff >= 0, jnp.exp(log_gamma[:, None, None] * jnp.maximum(diff, 0.0)), 0.0)
    d_q = jnp.exp(log_gamma[:, None] * (idx + 1.0))
    d_k = jnp.exp(log_gamma[:, None] * (chunk - 1.0 - idx))
    d_c = jnp.exp(log_gamma * chunk)
    if reps > 1:
        seq = jnp.arange(chunk * reps) // chunk
        d_in = jnp.where(seq[:, None] == seq[None, :], jnp.tile(d_in, (1, reps, reps)), 0.0)
        d_q = jnp.tile(d_q, (1, reps))
        d_k = jnp.tile(d_k, (1, reps))
    n = chunk * reps
    d_q = jnp.broadcast_to(d_q[:, :, None], (RET_HEADS, n, RET_V_DIM))
    d_k = jnp.broadcast_to(d_k[:, :, None], (RET_HEADS, n, RET_QK_DIM))
    return d_in, d_q, d_k, d_c


def _tri(n):
    i = jnp.arange(n)
    return (i[:, None] >= i[None, :]).astype(BF16)


def kernel(x_prompt, x_sample, cache_k, cache_v, state_ret, page_table, g_ffn1_pre, w_ffn1_gu, w_ffn1_down, g_ffn1_post, g_mix_pre, w_in, sb_bias, ret_gn_g, w_sb_out, w_ret_out, w_o, g_mix_post, g_ffn2_pre, w_ffn2_gu, w_ffn2_down, g_ffn2_post):
    batch, seq, d = x_prompt.shape
    n_seq, dec_t, _ = x_sample.shape
    depth = w_in.shape[0]
    n_pool = cache_k.shape[1]
    n_pages = page_table.shape[1]
    page = cache_k.shape[2]
    past_len = n_pages * page
    assert seq % RET_ROWS_PER_STEP == 0 and seq % SB_BLOCK == 0 and seq % ROW_TILE == 0
    assert (n_seq * dec_t) % ROW_TILE == 0 and ROW_TILE % dec_t == 0
    assert n_pages % DEC_PAGES_PER_STEP == 0 and SB_BLOCK % page == 0
    assert (DEC_PAGES_PER_STEP * page) % SB_BLOCK == 0 and dec_t % 8 == 0 and dec_t <= SB_BLOCK
    assert dec_t % RET_CHUNK != 0

    rope_p = _rope_tables(jnp.arange(seq))
    rope_s = _rope_tables(past_len + jnp.arange(ROW_TILE) % dec_t)
    decay_p = _decay_tables(RET_CHUNK, 1)
    decay_s = _decay_tables(dec_t, n_seq)
    tri = _tri(SB_BLOCK)

    hp = x_prompt.reshape(batch * seq, d)
    hs = x_sample.reshape(n_seq * dec_t, d)
    kp_l, vp_l, sp_l, ks_l, vs_l, ss_l = [], [], [], [], [], []
    for l in range(depth):
        g1pre, g1post = g_ffn1_pre[l][None], g_ffn1_post[l][None]
        g2pre, g2post = g_ffn2_pre[l][None], g_ffn2_post[l][None]
        gmpre, gmpost = g_mix_pre[l][None], g_mix_post[l][None]
        gn = ret_gn_g[l][None]
        w1gu, w1down = w_ffn1_gu[l].astype(BF16), w_ffn1_down[l].astype(BF16)
        w2gu, w2down = w_ffn2_gu[l].astype(BF16), w_ffn2_down[l].astype(BF16)
        win = w_in[l].astype(BF16)
        w_kv_t = w_in[l][:, _OFF_KSB:_OFF_QR].T.astype(BF16)
        wsb, wret, wo = w_sb_out[l].astype(BF16), w_ret_out[l].astype(BF16), w_o[l].astype(BF16)
        bias = sb_bias[l].astype(F32)
        bias_rows = jnp.broadcast_to(jnp.repeat(bias, dec_t)[:, None], (SB_HEADS * dec_t, SB_BLOCK))
        ck_t = cache_k[l].transpose(0, 2, 3, 1).reshape(n_pool, SB_WIDTH, page)
        cv_t = cache_v[l].transpose(0, 2, 3, 1).reshape(n_pool, SB_WIDTH, page)

        h1 = _ffn_call(hp, g1pre, w1gu, w1down, g1post)
        (k_t, v_t, kt_bf, v_bf, q_sb, q_r, k_r, v_r, g_r, a_sb, a_r) = _inproj_prompt_call(
            h1, gmpre, win, w_kv_t, *rope_p, batch, seq)
        o_sb = _sb_prompt_call(q_sb.reshape(batch, seq, SB_WIDTH), kt_bf,
                               v_bf.reshape(batch, seq, SB_WIDTH), bias, tri)
        o_r, s_p = _ret_prompt_call(q_r.reshape(batch, seq, -1), k_r.reshape(batch, seq, -1),
                                    v_r.reshape(batch, seq, -1), g_r.reshape(batch, seq, -1),
                                    gn, decay_p)
        hp = _post_call(h1, o_sb.reshape(batch * seq, SB_WIDTH), o_r.reshape(batch * seq, RET_V_WIDTH),
                        a_sb, a_r, wsb, wret, wo, gmpost, g2pre, w2gu, w2down, g2post)
        kp_l.append(k_t.reshape(batch, SB_HEADS, SB_HEAD_DIM, seq).transpose(0, 3, 1, 2))
        vp_l.append(v_t.reshape(batch, SB_HEADS, SB_HEAD_DIM, seq).transpose(0, 3, 1, 2))
        sp_l.append(s_p)

        h1 = _ffn_call(hs, g1pre, w1gu, w1down, g1post)
        (k_sb, v_sb, q_sb, q_r, k_r, v_r, g_r, a_sb, a_r) = _inproj_decode_call(
            h1, gmpre, win, *rope_s)
        o_sb = _sb_decode_call(q_sb.reshape(n_seq, dec_t, SB_WIDTH), k_sb.reshape(n_seq, dec_t, SB_WIDTH),
                               v_sb.reshape(n_seq, dec_t, SB_WIDTH), ck_t, cv_t, page_table, bias_rows, tri)
        o_r, s_s = _ret_decode_call(q_r, k_r, v_r, g_r, gn, state_ret[l], decay_s, dec_t)
        hs = _post_call(h1, o_sb.reshape(n_seq * dec_t, SB_WIDTH).astype(BF16), o_r, a_sb, a_r,
                        wsb, wret, wo, gmpost, g2pre, w2gu, w2down, g2post)
        ks_l.append(k_sb.reshape(n_seq, dec_t, SB_HEADS, SB_HEAD_DIM))
        vs_l.append(v_sb.reshape(n_seq, dec_t, SB_HEADS, SB_HEAD_DIM))
        ss_l.append(s_s)

    return (hp.reshape(batch, seq, d), hs.reshape(n_seq, dec_t, d),
            jnp.stack(kp_l), jnp.stack(vp_l), jnp.stack(sp_l),
            jnp.stack(ks_l), jnp.stack(vs_l), jnp.stack(ss_l))
```

```python
import functools

import jax
import jax.numpy as jnp
from jax import lax
from jax.experimental import pallas as pl
from jax.experimental.pallas import tpu as pltpu

F32 = jnp.float32
BF16 = jnp.bfloat16

SB_HEADS = 8
SB_HEAD_DIM = 64
SB_WIDTH = SB_HEADS * SB_HEAD_DIM
RET_HEADS = 4
RET_QK_DIM = 128
RET_V_DIM = 256
RET_QK_WIDTH = RET_HEADS * RET_QK_DIM
RET_V_WIDTH = RET_HEADS * RET_V_DIM
RET_CHUNK = 128
ROPE_BASE = 10000.0
NORM_EPS = 1e-6

LANES = 128
HEADS_PER_LANE_TILE = LANES // SB_HEAD_DIM

ROW_TILE = 256
SB_BLOCK = 256
DEC_PAGES_PER_STEP = 8
RET_ROWS_PER_STEP = 512
VMEM_LIMIT = 56 * 1024 * 1024

_NT = (((1,), (1,)), ((), ()))


def _const_spec(shape):
    nd = len(shape)
    return pl.BlockSpec(shape, lambda *_: (0,) * nd, pipeline_mode=pl.Buffered(1))


def _params(n_axes, vmem=VMEM_LIMIT):
    return pltpu.CompilerParams(dimension_semantics=("arbitrary",) * n_axes,
                                vmem_limit_bytes=vmem)


def _rms(x, g):
    ms = jnp.mean(x * x, axis=-1, keepdims=True)
    return x * lax.rsqrt(ms + NORM_EPS) * g


def _silu(x):
    return x * jax.nn.sigmoid(x)


def _ffn_residual(x, g_pre, wgu_ref, wdown_ref, g_post):
    d_ff = wdown_ref.shape[0]
    xn = _rms(x, g_pre).astype(BF16)
    gate = jnp.dot(xn, wgu_ref[:, :d_ff], preferred_element_type=F32)
    up = jnp.dot(xn, wgu_ref[:, d_ff:], preferred_element_type=F32)
    act = (_silu(gate) * up).astype(BF16)
    y = jnp.dot(act, wdown_ref[...], preferred_element_type=F32)
    return x + 0.5 * _rms(y, g_post)


def _ffn_kernel(x_ref, gpre_ref, wgu_ref, wdown_ref, gpost_ref, o_ref):
    o_ref[...] = _ffn_residual(x_ref[...], gpre_ref[...], wgu_ref, wdown_ref, gpost_ref[...])


def _ffn_call(x, g_pre, wgu, wdown, g_post):
    n, d = x.shape
    row = pl.BlockSpec((ROW_TILE, d), lambda i: (i, 0))
    return pl.pallas_call(
        _ffn_kernel,
        grid=(n // ROW_TILE,),
        in_specs=[row, _const_spec(g_pre.shape), _const_spec(wgu.shape),
                  _const_spec(wdown.shape), _const_spec(g_post.shape)],
        out_specs=row,
        out_shape=jax.ShapeDtypeStruct((n, d), F32),
        compiler_params=_params(1),
        name="ffn_block",
    )(x, g_pre, wgu, wdown, g_post)


_OFF_QSB = 0
_OFF_KSB = _OFF_QSB + SB_WIDTH
_OFF_VSB = _OFF_KSB + SB_WIDTH
_OFF_QR = _OFF_VSB + SB_WIDTH
_OFF_KR = _OFF_QR + RET_QK_WIDTH
_OFF_VR = _OFF_KR + RET_QK_WIDTH
_OFF_GR = _OFF_VR + RET_V_WIDTH
_OFF_ASB = _OFF_GR + RET_V_WIDTH


def _inproj_shared(u, win_ref, cos_ref, sin_ref, qsb_ref, qr_ref, kr_ref, vr_ref, gr_ref, asb_ref, ar_ref):
    def proj(lo, width):
        return jnp.dot(u, win_ref[:, lo:lo + width], preferred_element_type=F32)

    qsb_ref[...] = proj(_OFF_QSB, SB_WIDTH) * (SB_HEAD_DIM ** -0.5)
    cos = cos_ref[...]
    sin = sin_ref[...]
    q_r = proj(_OFF_QR, RET_QK_WIDTH)
    k_r = proj(_OFF_KR, RET_QK_WIDTH)
    for hh in range(RET_HEADS):
        sl = slice(hh * RET_QK_DIM, (hh + 1) * RET_QK_DIM)
        qh = q_r[:, sl]
        kh = k_r[:, sl]
        qr_ref[:, sl] = qh * cos + pltpu.roll(qh, RET_QK_DIM // 2, axis=1) * sin
        kr_ref[:, sl] = (kh * cos + pltpu.roll(kh, RET_QK_DIM // 2, axis=1) * sin) * (RET_QK_DIM ** -0.5)
    vr_ref[...] = proj(_OFF_VR, RET_V_WIDTH).astype(BF16)
    gr_ref[...] = proj(_OFF_GR, RET_V_WIDTH)
    d_model = asb_ref.shape[1]
    asb_ref[...] = proj(_OFF_ASB, d_model)
    ar_ref[...] = proj(_OFF_ASB + d_model, d_model)


def _inproj_prompt_kernel(h_ref, g_ref, win_ref, wkvt_ref, cos_ref, sin_ref,
                          kt_ref, vt_ref, ktb_ref, vb_ref, *shared_refs):
    u = _rms(h_ref[...], g_ref[...]).astype(BF16)
    kv_t = lax.dot_general(wkvt_ref[...], u, _NT, preferred_element_type=F32)
    kt_ref[0] = kv_t[:SB_WIDTH]
    vt_ref[0] = kv_t[SB_WIDTH:]
    ktb_ref[0, 0] = kv_t[:SB_WIDTH].astype(BF16)
    vb_ref[...] = jnp.dot(u, win_ref[:, _OFF_VSB:_OFF_VSB + SB_WIDTH],
                          preferred_element_type=F32).astype(BF16)
    _inproj_shared(u, win_ref, cos_ref, sin_ref, *shared_refs)


def _inproj_decode_kernel(h_ref, g_ref, win_ref, cos_ref, sin_ref, k_ref, v_ref, *shared_refs):
    u = _rms(h_ref[...], g_ref[...]).astype(BF16)
    k_ref[...] = jnp.dot(u, win_ref[:, _OFF_KSB:_OFF_KSB + SB_WIDTH], preferred_element_type=F32)
    v_ref[...] = jnp.dot(u, win_ref[:, _OFF_VSB:_OFF_VSB + SB_WIDTH], preferred_element_type=F32)
    _inproj_shared(u, win_ref, cos_ref, sin_ref, *shared_refs)


def _row_spec(width):
    return pl.BlockSpec((ROW_TILE, width), lambda i: (i, 0))


def _shared_outputs(n, d):
    widths_dtypes = [(SB_WIDTH, F32), (RET_QK_WIDTH, F32), (RET_QK_WIDTH, F32), (RET_V_WIDTH, BF16),
                     (RET_V_WIDTH, F32), (d, F32), (d, F32)]
    return ([_row_spec(w) for w, _ in widths_dtypes],
            [jax.ShapeDtypeStruct((n, w), dt) for w, dt in widths_dtypes])


def _inproj_prompt_call(h, g_pre, w_in, w_kv_t, cos_tab, sin_tab, batch, seq):
    n, d = h.shape
    assert ROW_TILE == SB_BLOCK
    tiles = seq // ROW_TILE
    tab = pl.BlockSpec((ROW_TILE, RET_QK_DIM), lambda i: (i % tiles, 0))
    t_spec = pl.BlockSpec((1, SB_WIDTH, ROW_TILE), lambda i: (i // tiles, 0, i % tiles))
    shared_specs, shared_shapes = _shared_outputs(n, d)
    return pl.pallas_call(
        _inproj_prompt_kernel,
        grid=(n // ROW_TILE,),
        in_specs=[_row_spec(d), _const_spec(g_pre.shape), _const_spec(w_in.shape),
                  _const_spec(w_kv_t.shape), tab, tab],
        out_specs=[t_spec, t_spec,
                   pl.BlockSpec((1, 1, SB_WIDTH, SB_BLOCK), lambda i: (i // tiles, i % tiles, 0, 0)),
                   _row_spec(SB_WIDTH)] + shared_specs,
        out_shape=[jax.ShapeDtypeStruct((batch, SB_WIDTH, seq), F32),
                   jax.ShapeDtypeStruct((batch, SB_WIDTH, seq), F32),
                   jax.ShapeDtypeStruct((batch, tiles, SB_WIDTH, SB_BLOCK), BF16),
                   jax.ShapeDtypeStruct((n, SB_WIDTH), BF16)] + shared_shapes,
        compiler_params=_params(1),
        name="in_projection_prompt",
    )(h, g_pre, w_in, w_kv_t, cos_tab, sin_tab)


def _inproj_decode_call(h, g_pre, w_in, cos_tab, sin_tab):
    n, d = h.shape
    tab = pl.BlockSpec((ROW_TILE, RET_QK_DIM), lambda i: (0, 0))
    shared_specs, shared_shapes = _shared_outputs(n, d)
    return pl.pallas_call(
        _inproj_decode_kernel,
        grid=(n // ROW_TILE,),
        in_specs=[_row_spec(d), _const_spec(g_pre.shape), _const_spec(w_in.shape), tab, tab],
        out_specs=[_row_spec(SB_WIDTH), _row_spec(SB_WIDTH)] + shared_specs,
        out_shape=[jax.ShapeDtypeStruct((n, SB_WIDTH), F32),
                   jax.ShapeDtypeStruct((n, SB_WIDTH), F32)] + shared_shapes,
        compiler_params=_params(1),
        name="in_projection_decode",
    )(h, g_pre, w_in, cos_tab, sin_tab)


LOG2E = 1.4426950408889634


NULL_LOGIT = -1e30


def _sb_stay(z2, mask):
    if mask is not None:
        z2 = jnp.where(mask, z2, NULL_LOGIT)
    neg_abs = pltpu.bitcast(pltpu.bitcast(z2, jnp.uint32) | jnp.uint32(0x80000000), F32)
    sp2 = jnp.maximum(z2, 0.0) + jnp.log(1.0 + jnp.exp2(neg_abs)) * LOG2E
    hi = sp2.astype(BF16)
    lo = (sp2 - hi.astype(F32)).astype(BF16)
    return z2, jnp.concatenate([hi, lo], axis=1), jnp.sum(sp2, axis=-1, keepdims=True)


def _sb_break(z2, hilo, tri2, run2):
    csum = jnp.dot(hilo, tri2, preferred_element_type=F32) + run2
    return jnp.exp2(z2 - csum).astype(BF16)


def _sb_weights(z2, tri2, run2, mask):
    z2, hilo, total = _sb_stay(z2, mask)
    return _sb_break(z2, hilo, tri2, run2), run2 + total


def _sb_prompt_kernel(bias_ref, q_ref, kt_ref, v_ref, tri_ref, o_ref,
                      zraw_scr, z_scr, hilo_scr, c_scr, acc_scr, run_scr):
    hp = pl.program_id(1)
    qi = pl.program_id(2)
    blk = q_ref.shape[1]
    q = q_ref[0] * LOG2E
    tri2 = tri_ref[...]
    lane_head = lax.broadcasted_iota(jnp.int32, (1, LANES), 1) // SB_HEAD_DIM
    row = lax.broadcasted_iota(jnp.int32, (blk, blk), 0)
    col = lax.broadcasted_iota(jnp.int32, (blk, blk), 1)
    causal = col < row
    heads = range(HEADS_PER_LANE_TILE)
    q_h = [jnp.where(lane_head == hh, q, 0.0).astype(BF16) for hh in heads]
    bias2 = [bias_ref[hp * HEADS_PER_LANE_TILE + hh] * LOG2E for hh in heads]

    def logits(hh, j, bias):
        return jnp.dot(q_h[hh], kt_ref[0, j], preferred_element_type=F32) + bias

    def trip(t, _):
        slot = (t + 1) % 2
        j_new = jnp.maximum(qi - t, 0)
        j_av = jnp.minimum(qi - t + 3, qi)
        v = v_ref[0, pl.ds(pl.multiple_of(j_av * blk, blk), blk), :]
        for hh in heads:
            a = jnp.exp2(z_scr[slot, hh] - c_scr[hh]).astype(BF16)
            acc_scr[hh] += jnp.dot(a, v, preferred_element_type=F32)
            c_scr[hh] = jnp.dot(hilo_scr[hh], tri2, preferred_element_type=F32)
            z2, hilo, total = _sb_stay(zraw_scr[hh], None)
            run2 = run_scr[hh]
            z_scr[slot, hh] = z2 - run2
            hilo_scr[hh] = hilo
            run_scr[hh] = run2 + total
            zraw_scr[hh] = logits(hh, j_new, jnp.where(t <= qi, bias2[hh], NULL_LOGIT))
        return 0

    for hh in heads:
        run_scr[hh] = jnp.zeros((blk, 1), F32)
        z_scr[0, hh] = jnp.full((blk, blk), NULL_LOGIT, F32)
        z_scr[1, hh] = jnp.full((blk, blk), NULL_LOGIT, F32)
        hilo_scr[hh] = jnp.zeros((blk, 2 * blk), BF16)
        c_scr[hh] = jnp.zeros((blk, blk), F32)
        acc_scr[hh] = jnp.zeros((blk, LANES), F32)
        zraw_scr[hh] = jnp.where(causal, logits(hh, qi, bias2[hh]), NULL_LOGIT)
    lax.fori_loop(1, qi + 4, trip, 0)

    out = acc_scr[0]
    for hh in heads[1:]:
        out = jnp.where(lane_head == hh, acc_scr[hh], out)
    o_ref[0] = out.astype(o_ref.dtype)


def _sb_prompt_call(q, kt_bf, v_bf, bias, tri):
    b, t, w = q.shape
    blk = SB_BLOCK
    nh = HEADS_PER_LANE_TILE
    q_spec = pl.BlockSpec((1, blk, LANES), lambda bi, hp, qi: (bi, qi, hp))
    return pl.pallas_call(
        _sb_prompt_kernel,
        grid=(b, w // LANES, t // blk),
        in_specs=[pl.BlockSpec(memory_space=pltpu.SMEM), q_spec,
                  pl.BlockSpec((1, t // blk, LANES, blk), lambda bi, hp, qi: (bi, 0, hp, 0)),
                  pl.BlockSpec((1, t, LANES), lambda bi, hp, qi: (bi, 0, hp)),
                  _const_spec(tri.shape)],
        out_specs=q_spec,
        out_shape=jax.ShapeDtypeStruct((b, t, w), BF16),
        scratch_shapes=[pltpu.VMEM((nh, blk, blk), F32),
                        pltpu.VMEM((2, nh, blk, blk), F32), pltpu.VMEM((nh, blk, 2 * blk), BF16),
                        pltpu.VMEM((nh, blk, blk), F32),
                        pltpu.VMEM((nh, blk, LANES), F32), pltpu.VMEM((nh, blk, 1), F32)],
        compiler_params=_params(3),
        name="sb_attention_prompt",
    )(bias, q, kt_bf, v_bf, tri)


def _sb_decode_kernel(pt_ref, q_ref, kn_ref, vn_ref, bias_ref, tri_ref, *rest, pages):
    del pt_ref
    k_refs = rest[:pages]
    v_refs = rest[pages:2 * pages]
    o_ref = rest[2 * pages]
    qbd_ref, run_ref, acc_ref = rest[2 * pages + 1:]
    g = pl.program_id(1)
    dec_t = q_ref.shape[1]
    rows = SB_HEADS * dec_t
    blk = tri_ref.shape[1]
    bias = bias_ref[...] * LOG2E
    tri = tri_ref[...]
    lane_head = lax.broadcasted_iota(jnp.int32, (1, SB_WIDTH), 1) // SB_HEAD_DIM

    @pl.when(g == 0)
    def _():
        q = q_ref[0] * LOG2E
        qbd = jnp.concatenate([jnp.where(lane_head == hh, q, 0.0) for hh in range(SB_HEADS)], axis=0)
        qbd_ref[...] = qbd.astype(BF16)
        pad = jnp.zeros((blk - dec_t, SB_WIDTH), F32)
        k_new = jnp.concatenate([kn_ref[0], pad], axis=0).astype(BF16)
        v_new = jnp.concatenate([vn_ref[0], pad], axis=0).astype(BF16)
        t_row = lax.broadcasted_iota(jnp.int32, (rows, blk), 0) % dec_t
        col = lax.broadcasted_iota(jnp.int32, (rows, blk), 1)
        z = lax.dot_general(qbd_ref[...], k_new, _NT, preferred_element_type=F32) + bias
        a, run = _sb_weights(z, tri, jnp.zeros((rows, 1), F32), col < t_row)
        run_ref[...] = jnp.broadcast_to(run, run_ref.shape)
        acc_ref[...] = jnp.dot(a, v_new, preferred_element_type=F32)

    qbd = qbd_ref[...]
    run = run_ref[:, 0:1]
    acc = acc_ref[...]
    pages_per_blk = blk // k_refs[0].shape[2]
    for p in reversed(range(pages // pages_per_blk)):
        sl = slice(p * pages_per_blk, (p + 1) * pages_per_blk)
        k_t = jnp.concatenate([r[0] for r in k_refs[sl]], axis=1).astype(BF16)
        v_t = jnp.concatenate([r[0] for r in v_refs[sl]], axis=1).astype(BF16)
        z = jnp.dot(qbd, k_t, preferred_element_type=F32) + bias
        a, run = _sb_weights(z, tri, run, None)
        acc = acc + lax.dot_general(a, v_t, _NT, preferred_element_type=F32)
    run_ref[...] = jnp.broadcast_to(run, run_ref.shape)
    acc_ref[...] = acc

    @pl.when(g == pl.num_programs(1) - 1)
    def _():
        out = jnp.zeros((dec_t, SB_WIDTH), F32)
        for hh in range(SB_HEADS):
            out = jnp.where(lane_head == hh, acc[hh * dec_t:(hh + 1) * dec_t, :], out)
        o_ref[0] = out


def _sb_decode_call(q, k_new, v_new, cache_kt, cache_vt, page_table, bias_rows, tri):
    n_seq, dec_t, w = q.shape
    n_pages = page_table.shape[1]
    pages = DEC_PAGES_PER_STEP
    page = cache_kt.shape[2]
    rows = SB_HEADS * dec_t
    seq_spec = pl.BlockSpec((1, dec_t, w), lambda s, g, pt: (s, 0, 0))

    def page_spec(i):
        return pl.BlockSpec((1, w, page), lambda s, g, pt: (pt[s, n_pages - (g + 1) * pages + i], 0, 0))

    def const(shape):
        nd = len(shape)
        return pl.BlockSpec(shape, lambda s, g, pt: (0,) * nd)

    grid_spec = pltpu.PrefetchScalarGridSpec(
        num_scalar_prefetch=1,
        grid=(n_seq, n_pages // pages),
        in_specs=([seq_spec, seq_spec, seq_spec, const(bias_rows.shape), const(tri.shape)]
                  + [page_spec(i) for i in range(pages)] * 2),
        out_specs=seq_spec,
        scratch_shapes=[pltpu.VMEM((rows, w), BF16), pltpu.VMEM((rows, LANES), F32),
                        pltpu.VMEM((rows, w), F32)],
    )
    return pl.pallas_call(
        functools.partial(_sb_decode_kernel, pages=pages),
        grid_spec=grid_spec,
        out_shape=jax.ShapeDtypeStruct((n_seq, dec_t, w), F32),
        compiler_params=_params(2),
        name="sb_attention_decode",
    )(page_table, q, k_new, v_new, bias_rows, tri, *([cache_kt] * pages), *([cache_vt] * pages))


def _head_norm_gate(o, gn, gate):
    o = o * lax.rsqrt(jnp.mean(o * o, axis=-1, keepdims=True) + NORM_EPS) * gn
    return (_silu(gate) * o).astype(BF16)


def _ret_prompt_kernel(dc_ref, q_ref, k_ref, v_ref, g_ref, gn_ref, din_ref, dq_ref, dk_ref,
                       o_ref, s_out_ref, s_ref):
    hh = pl.program_id(1)
    c = pl.program_id(2)

    @pl.when(c == 0)
    def _():
        s_ref[...] = jnp.zeros_like(s_ref)

    dc = dc_ref[hh]
    din = din_ref[0]
    dq = dq_ref[0]
    dk = dk_ref[0]
    gn = gn_ref[...]
    state = s_ref[...]
    for i in range(q_ref.shape[1] // RET_CHUNK):
        sl = slice(i * RET_CHUNK, (i + 1) * RET_CHUNK)
        q = q_ref[0, sl, :].astype(BF16)
        k = k_ref[0, sl, :]
        v = v_ref[0, sl, :]
        inner = lax.dot_general(q, k.astype(BF16), _NT, preferred_element_type=F32) * din
        o = (jnp.dot(inner.astype(BF16), v, preferred_element_type=F32)
             + jnp.dot(q, state.astype(BF16), preferred_element_type=F32) * dq)
        kd_t = (k * dk).T.astype(BF16)
        state = state * dc + jnp.dot(kd_t, v, preferred_element_type=F32)
        o_ref[0, sl, :] = _head_norm_gate(o, gn, g_ref[0, sl, :])
    s_ref[...] = state

    @pl.when(c == pl.num_programs(2) - 1)
    def _():
        s_out_ref[0, 0] = state


def _ret_prompt_call(q, k, v_bf, gate, gn, tables):
    b, t, _ = q.shape
    din, dq, dk, dc = tables
    rows = RET_ROWS_PER_STEP
    qk_spec = pl.BlockSpec((1, rows, RET_QK_DIM), lambda bi, h, c: (bi, c, h))
    v_spec = pl.BlockSpec((1, rows, RET_V_DIM), lambda bi, h, c: (bi, c, h))

    def head(shape):
        return pl.BlockSpec((1,) + shape, lambda bi, h, c: (h, 0, 0))

    return pl.pallas_call(
        _ret_prompt_kernel,
        grid=(b, RET_HEADS, t // rows),
        in_specs=[pl.BlockSpec(memory_space=pltpu.SMEM), qk_spec, qk_spec, v_spec, v_spec,
                  pl.BlockSpec((1, RET_V_DIM), lambda bi, h, c: (0, h)),
                  head((RET_CHUNK, RET_CHUNK)), head((RET_CHUNK, RET_V_DIM)),
                  head((RET_CHUNK, RET_QK_DIM))],
        out_specs=[v_spec,
                   pl.BlockSpec((1, 1, RET_QK_DIM, RET_V_DIM), lambda bi, h, c: (bi, h, 0, 0))],
        out_shape=[jax.ShapeDtypeStruct((b, t, RET_V_WIDTH), BF16),
                   jax.ShapeDtypeStruct((b, RET_HEADS, RET_QK_DIM, RET_V_DIM), F32)],
        scratch_shapes=[pltpu.VMEM((RET_QK_DIM, RET_V_DIM), F32)],
        compiler_params=_params(3),
        name="retention_prompt",
    )(dc, q, k, v_bf, gate, gn, din, dq, dk)


def _ret_decode_kernel(dc_ref, q_ref, k_ref, v_ref, g_ref, gn_ref, s_in_ref, din_ref, dq_ref, dk_ref,
                       o_ref, s_out_ref, *, dec_t):
    hh = pl.program_id(0)
    n_seq = s_in_ref.shape[0]
    rows = n_seq * dec_t
    q = q_ref[...]
    k = k_ref[...]
    v = v_ref[...]
    q_bf = q.astype(BF16)
    inner = lax.dot_general(q_bf, k.astype(BF16), _NT, preferred_element_type=F32) * din_ref[0]
    o = jnp.dot(inner.astype(BF16), v, preferred_element_type=F32)

    s_old = s_in_ref[:, 0].reshape(n_seq * RET_QK_DIM, RET_V_DIM)
    row_seq = lax.broadcasted_iota(jnp.int32, (rows, RET_QK_DIM), 0) // dec_t
    q_bd = jnp.concatenate([jnp.where(row_seq == s, q, 0.0) for s in range(n_seq)], axis=1)
    o = o + jnp.dot(q_bd.astype(BF16), s_old.astype(BF16), preferred_element_type=F32) * dq_ref[0]
    o_ref[...] = _head_norm_gate(o, gn_ref[...], g_ref[...])

    kd_t = (k * dk_ref[0]).T
    col_seq = lax.broadcasted_iota(jnp.int32, (RET_QK_DIM, rows), 1) // dec_t
    k_bd_t = jnp.concatenate([jnp.where(col_seq == s, kd_t, 0.0) for s in range(n_seq)], axis=0)
    s_new = s_old * dc_ref[hh] + jnp.dot(k_bd_t.astype(BF16), v, preferred_element_type=F32)
    s_out_ref[:, 0] = s_new.reshape(n_seq, RET_QK_DIM, RET_V_DIM)


def _ret_decode_call(q, k, v_bf, gate, gn, state, tables, dec_t):
    rows = q.shape[0]
    n_seq = state.shape[0]
    din, dq, dk, dc = tables
    qk_spec = pl.BlockSpec((rows, RET_QK_DIM), lambda h: (0, h))
    v_spec = pl.BlockSpec((rows, RET_V_DIM), lambda h: (0, h))
    s_spec = pl.BlockSpec((n_seq, 1, RET_QK_DIM, RET_V_DIM), lambda h: (0, h, 0, 0))

    def head(shape):
        return pl.BlockSpec((1,) + shape, lambda h: (h, 0, 0))

    return pl.pallas_call(
        functools.partial(_ret_decode_kernel, dec_t=dec_t),
        grid=(RET_HEADS,),
        in_specs=[pl.BlockSpec(memory_space=pltpu.SMEM), qk_spec, qk_spec, v_spec, v_spec,
                  pl.BlockSpec((1, RET_V_DIM), lambda h: (0, h)), s_spec,
                  head((rows, rows)), head((rows, RET_V_DIM)), head((rows, RET_QK_DIM))],
        out_specs=[v_spec, s_spec],
        out_shape=[jax.ShapeDtypeStruct((rows, RET_V_WIDTH), BF16),
                   jax.ShapeDtypeStruct(state.shape, F32)],
        compiler_params=_params(1),
        name="retention_decode",
    )(dc, q, k, v_bf, gate, gn, state, din, dq, dk)


def _post_kernel(h_ref, osb_ref, or_ref, asb_ref, ar_ref, wsb_ref, wret_ref, wo_ref, gmix_ref,
                 gpre_ref, wgu_ref, wdown_ref, gpost_ref, o_ref):
    m = (jax.nn.sigmoid(asb_ref[...]) * jnp.dot(osb_ref[...], wsb_ref[...], preferred_element_type=F32)
         + jax.nn.sigmoid(ar_ref[...]) * jnp.dot(or_ref[...], wret_ref[...], preferred_element_type=F32))
    mix = jnp.dot(m.astype(BF16), wo_ref[...], preferred_element_type=F32)
    h = h_ref[...] + _rms(mix, gmix_ref[...])
    o_ref[...] = _ffn_residual(h, gpre_ref[...], wgu_ref, wdown_ref, gpost_ref[...])


def _post_call(h, o_sb, o_r, a_sb, a_r, w_sb, w_ret, w_o, g_mix, g_pre, wgu, wdown, g_post):
    n, d = h.shape
    consts = [w_sb, w_ret, w_o, g_mix, g_pre, wgu, wdown, g_post]
    return pl.pallas_call(
        _post_kernel,
        grid=(n // ROW_TILE,),
        in_specs=[_row_spec(d), _row_spec(SB_WIDTH), _row_spec(RET_V_WIDTH), _row_spec(d), _row_spec(d)]
                 + [_const_spec(c.shape) for c in consts],
        out_specs=_row_spec(d),
        out_shape=jax.ShapeDtypeStruct((n, d), F32),
        compiler_params=_params(1),
        name="merge_out_ffn",
    )(h, o_sb, o_r, a_sb, a_r, *consts)


def _rope_tables(pos):
    half = RET_QK_DIM // 2
    freq = ROPE_BASE ** (-jnp.arange(half, dtype=F32) / half)
    ang = pos.astype(F32)[:, None] * freq[None, :]
    cos, sin = jnp.cos(ang), jnp.sin(ang)
    return jnp.concatenate([cos, cos], axis=1), jnp.concatenate([-sin, sin], axis=1)


def _decay_tables(chunk, reps):
    log_gamma = jnp.log1p(-jnp.exp2(-5.0 - jnp.arange(RET_HEADS, dtype=F32)))
    idx = jnp.arange(chunk, dtype=F32)
    diff = idx[:, None] - idx[None, :]
    d_in = jnp.where(diff >= 0, jnp.exp(log_gamma[:, None, None] * jnp.maximum(diff, 0.0)), 0.0)
    d_q = jnp.exp(log_gamma[:, None] * (idx + 1.0))
    d_k = jnp.exp(log_gamma[:, None] * (chunk - 1.0 - idx))
    d_c = jnp.exp(log_gamma * chunk)
    if reps > 1:
        seq = jnp.arange(chunk * reps) // chunk
        d_in = jnp.where(seq[:, None] == seq[None, :], jnp.tile(d_in, (1, reps, reps)), 0.0)
        d_q = jnp.tile(d_q, (1, reps))
        d_k = jnp.tile(d_k, (1, reps))
    n = chunk * reps
    d_q = jnp.broadcast_to(d_q[:, :, None], (RET_HEADS, n, RET_V_DIM))
    d_k = jnp.broadcast_to(d_k[:, :, None], (RET_HEADS, n, RET_QK_DIM))
    return d_in, d_q, d_k, d_c


def _tri2(n):
    i = jnp.arange(n)
    tri = (i[:, None] >= i[None, :]).astype(BF16)
    return jnp.concatenate([tri, tri], axis=0)


def kernel(x_prompt, x_sample, cache_k, cache_v, state_ret, page_table, g_ffn1_pre, w_ffn1_gu, w_ffn1_down, g_ffn1_post, g_mix_pre, w_in, sb_bias, ret_gn_g, w_sb_out, w_ret_out, w_o, g_mix_post, g_ffn2_pre, w_ffn2_gu, w_ffn2_down, g_ffn2_post):
    batch, seq, d = x_prompt.shape
    n_seq, dec_t, _ = x_sample.shape
    depth = w_in.shape[0]
    n_pool = cache_k.shape[1]
    n_pages = page_table.shape[1]
    page = cache_k.shape[2]
    past_len = n_pages * page
    assert seq % RET_ROWS_PER_STEP == 0 and seq % SB_BLOCK == 0 and seq % ROW_TILE == 0
    assert (n_seq * dec_t) % ROW_TILE == 0 and ROW_TILE % dec_t == 0
    assert n_pages % DEC_PAGES_PER_STEP == 0 and SB_BLOCK % page == 0
    assert (DEC_PAGES_PER_STEP * page) % SB_BLOCK == 0 and dec_t % 8 == 0 and dec_t <= SB_BLOCK
    assert dec_t % RET_CHUNK != 0

    rope_p = _rope_tables(jnp.arange(seq))
    rope_s = _rope_tables(past_len + jnp.arange(ROW_TILE) % dec_t)
    decay_p = _decay_tables(RET_CHUNK, 1)
    decay_s = _decay_tables(dec_t, n_seq)
    tri = _tri2(SB_BLOCK)

    hp = x_prompt.reshape(batch * seq, d)
    hs = x_sample.reshape(n_seq * dec_t, d)
    kp_l, vp_l, sp_l, ks_l, vs_l, ss_l = [], [], [], [], [], []
    for l in range(depth):
        g1pre, g1post = g_ffn1_pre[l][None], g_ffn1_post[l][None]
        g2pre, g2post = g_ffn2_pre[l][None], g_ffn2_post[l][None]
        gmpre, gmpost = g_mix_pre[l][None], g_mix_post[l][None]
        gn = ret_gn_g[l][None]
        w1gu, w1down = w_ffn1_gu[l].astype(BF16), w_ffn1_down[l].astype(BF16)
        w2gu, w2down = w_ffn2_gu[l].astype(BF16), w_ffn2_down[l].astype(BF16)
        win = w_in[l].astype(BF16)
        w_kv_t = w_in[l][:, _OFF_KSB:_OFF_QR].T.astype(BF16)
        wsb, wret, wo = w_sb_out[l].astype(BF16), w_ret_out[l].astype(BF16), w_o[l].astype(BF16)
        bias = sb_bias[l].astype(F32)
        bias_rows = jnp.broadcast_to(jnp.repeat(bias, dec_t)[:, None], (SB_HEADS * dec_t, SB_BLOCK))
        ck_t = cache_k[l].transpose(0, 2, 3, 1).reshape(n_pool, SB_WIDTH, page)
        cv_t = cache_v[l].transpose(0, 2, 3, 1).reshape(n_pool, SB_WIDTH, page)

        h1 = _ffn_call(hp, g1pre, w1gu, w1down, g1post)
        (k_t, v_t, kt_bf, v_bf, q_sb, q_r, k_r, v_r, g_r, a_sb, a_r) = _inproj_prompt_call(
            h1, gmpre, win, w_kv_t, *rope_p, batch, seq)
        o_sb = _sb_prompt_call(q_sb.reshape(batch, seq, SB_WIDTH), kt_bf,
                               v_bf.reshape(batch, seq, SB_WIDTH), bias, tri)
        o_r, s_p = _ret_prompt_call(q_r.reshape(batch, seq, -1), k_r.reshape(batch, seq, -1),
                                    v_r.reshape(batch, seq, -1), g_r.reshape(batch, seq, -1),
                                    gn, decay_p)
        hp = _post_call(h1, o_sb.reshape(batch * seq, SB_WIDTH), o_r.reshape(batch * seq, RET_V_WIDTH),
                        a_sb, a_r, wsb, wret, wo, gmpost, g2pre, w2gu, w2down, g2post)
        kp_l.append(k_t.reshape(batch, SB_HEADS, SB_HEAD_DIM, seq).transpose(0, 3, 1, 2))
        vp_l.append(v_t.reshape(batch, SB_HEADS, SB_HEAD_DIM, seq).transpose(0, 3, 1, 2))
        sp_l.append(s_p)

        h1 = _ffn_call(hs, g1pre, w1gu, w1down, g1post)
        (k_sb, v_sb, q_sb, q_r, k_r, v_r, g_r, a_sb, a_r) = _inproj_decode_call(
            h1, gmpre, win, *rope_s)
        o_sb = _sb_decode_call(q_sb.reshape(n_seq, dec_t, SB_WIDTH), k_sb.reshape(n_seq, dec_t, SB_WIDTH),
                               v_sb.reshape(n_seq, dec_t, SB_WIDTH), ck_t, cv_t, page_table, bias_rows, tri)
        o_r, s_s = _ret_decode_call(q_r, k_r, v_r, g_r, gn, state_ret[l], decay_s, dec_t)
        hs = _post_call(h1, o_sb.reshape(n_seq * dec_t, SB_WIDTH).astype(BF16), o_r, a_sb, a_r,
                        wsb, wret, wo, gmpost, g2pre, w2gu, w2down, g2post)
        ks_l.append(k_sb.reshape(n_seq, dec_t, SB_HEADS, SB_HEAD_DIM))
        vs_l.append(v_sb.reshape(n_seq, dec_t, SB_HEADS, SB_HEAD_DIM))
        ss_l.append(s_s)

    return (hp.reshape(batch, seq, d), hs.reshape(n_seq, dec_t, d),
            jnp.stack(kp_l), jnp.stack(vp_l), jnp.stack(sp_l),
            jnp.stack(ks_l), jnp.stack(vs_l), jnp.stack(ss_l))
```

```python
import functools

import jax
import jax.numpy as jnp
from jax import lax
from jax.experimental import pallas as pl
from jax.experimental.pallas import tpu as pltpu

F32 = jnp.float32
BF16 = jnp.bfloat16

SB_HEADS = 8
SB_HEAD_DIM = 64
SB_WIDTH = SB_HEADS * SB_HEAD_DIM
RET_HEADS = 4
RET_QK_DIM = 128
RET_V_DIM = 256
RET_QK_WIDTH = RET_HEADS * RET_QK_DIM
RET_V_WIDTH = RET_HEADS * RET_V_DIM
RET_CHUNK = 128
ROPE_BASE = 10000.0
NORM_EPS = 1e-6
LOG2E = 1.4426950408889634

LANES = 128
HEADS_PER_LANE_TILE = LANES // SB_HEAD_DIM

ROW_TILE = 256
SB_BLOCK = 256
DEC_PAGES_PER_STEP = 8
RET_ROWS_PER_STEP = 512
VMEM_LIMIT = 56 * 1024 * 1024

_NT = (((1,), (1,)), ((), ()))


def _const_spec(shape):
    nd = len(shape)
    return pl.BlockSpec(shape, lambda *_: (0,) * nd, pipeline_mode=pl.Buffered(1))


def _params(n_axes, vmem=VMEM_LIMIT):
    return pltpu.CompilerParams(dimension_semantics=("arbitrary",) * n_axes,
                                vmem_limit_bytes=vmem)


def _rms(x, g):
    ms = jnp.mean(x * x, axis=-1, keepdims=True)
    return x * lax.rsqrt(ms + NORM_EPS) * g


def _silu(x):
    return x * jax.nn.sigmoid(x)


def _ffn_residual(x, g_pre, wgu_ref, wdown_ref, g_post):
    d_ff = wdown_ref.shape[0]
    xn = _rms(x, g_pre).astype(BF16)
    gate = jnp.dot(xn, wgu_ref[:, :d_ff], preferred_element_type=F32)
    up = jnp.dot(xn, wgu_ref[:, d_ff:], preferred_element_type=F32)
    act = (_silu(gate) * up).astype(BF16)
    y = jnp.dot(act, wdown_ref[...], preferred_element_type=F32)
    return x + 0.5 * _rms(y, g_post)


def _ffn_kernel(x_ref, gpre_ref, wgu_ref, wdown_ref, gpost_ref, o_ref):
    o_ref[...] = _ffn_residual(x_ref[...], gpre_ref[...], wgu_ref, wdown_ref, gpost_ref[...])


def _ffn_call(x, g_pre, wgu, wdown, g_post):
    n, d = x.shape
    row = pl.BlockSpec((ROW_TILE, d), lambda i: (i, 0))
    return pl.pallas_call(
        _ffn_kernel,
        grid=(n // ROW_TILE,),
        in_specs=[row, _const_spec(g_pre.shape), _const_spec(wgu.shape),
                  _const_spec(wdown.shape), _const_spec(g_post.shape)],
        out_specs=row,
        out_shape=jax.ShapeDtypeStruct((n, d), F32),
        compiler_params=_params(1),
        name="ffn_block",
    )(x, g_pre, wgu, wdown, g_post)


_OFF_QSB = 0
_OFF_KSB = _OFF_QSB + SB_WIDTH
_OFF_VSB = _OFF_KSB + SB_WIDTH
_OFF_QR = _OFF_VSB + SB_WIDTH
_OFF_KR = _OFF_QR + RET_QK_WIDTH
_OFF_VR = _OFF_KR + RET_QK_WIDTH
_OFF_GR = _OFF_VR + RET_V_WIDTH
_OFF_ASB = _OFF_GR + RET_V_WIDTH


def _inproj_shared(u, win_ref, cos_ref, sin_ref, qr_ref, kr_ref, vr_ref, gr_ref, asb_ref, ar_ref):
    def proj(lo, width):
        return jnp.dot(u, win_ref[:, lo:lo + width], preferred_element_type=F32)

    cos = cos_ref[...]
    sin = sin_ref[...]
    q_r = proj(_OFF_QR, RET_QK_WIDTH)
    k_r = proj(_OFF_KR, RET_QK_WIDTH)
    for hh in range(RET_HEADS):
        sl = slice(hh * RET_QK_DIM, (hh + 1) * RET_QK_DIM)
        qh = q_r[:, sl]
        kh = k_r[:, sl]
        qr_ref[:, sl] = qh * cos + pltpu.roll(qh, RET_QK_DIM // 2, axis=1) * sin
        kr_ref[:, sl] = (kh * cos + pltpu.roll(kh, RET_QK_DIM // 2, axis=1) * sin) * (RET_QK_DIM ** -0.5)
    vr_ref[...] = proj(_OFF_VR, RET_V_WIDTH).astype(BF16)
    gr_ref[...] = proj(_OFF_GR, RET_V_WIDTH)
    d_model = asb_ref.shape[1]
    asb_ref[...] = proj(_OFF_ASB, d_model)
    ar_ref[...] = proj(_OFF_ASB + d_model, d_model)


def _inproj_prompt_kernel(h_ref, g_ref, win_ref, wkvt_ref, cos_ref, sin_ref,
                          kt_ref, vt_ref, ktb_ref, vb_ref, *rest):
    q_refs, shared_refs = rest[:HEADS_PER_LANE_TILE], rest[HEADS_PER_LANE_TILE:]
    u = _rms(h_ref[...], g_ref[...]).astype(BF16)
    q = jnp.dot(u, win_ref[:, _OFF_QSB:_OFF_QSB + SB_WIDTH], preferred_element_type=F32)
    q = q * (SB_HEAD_DIM ** -0.5 * LOG2E)
    head_slot = (lax.broadcasted_iota(jnp.int32, (1, SB_WIDTH), 1) // SB_HEAD_DIM) % HEADS_PER_LANE_TILE
    for hh, q_ref in enumerate(q_refs):
        q_ref[...] = jnp.where(head_slot == hh, q, 0.0).astype(BF16)
    kv_t = lax.dot_general(wkvt_ref[...], u, _NT, preferred_element_type=F32)
    kt_ref[0] = kv_t[:SB_WIDTH]
    vt_ref[0] = kv_t[SB_WIDTH:]
    ktb_ref[0, 0] = kv_t[:SB_WIDTH].astype(BF16)
    vb_ref[...] = jnp.dot(u, win_ref[:, _OFF_VSB:_OFF_VSB + SB_WIDTH],
                          preferred_element_type=F32).astype(BF16)
    _inproj_shared(u, win_ref, cos_ref, sin_ref, *shared_refs)


def _inproj_decode_kernel(h_ref, g_ref, win_ref, cos_ref, sin_ref, k_ref, v_ref, q_ref, *shared_refs):
    u = _rms(h_ref[...], g_ref[...]).astype(BF16)
    q_ref[...] = jnp.dot(u, win_ref[:, _OFF_QSB:_OFF_QSB + SB_WIDTH],
                         preferred_element_type=F32) * (SB_HEAD_DIM ** -0.5)
    k_ref[...] = jnp.dot(u, win_ref[:, _OFF_KSB:_OFF_KSB + SB_WIDTH], preferred_element_type=F32)
    v_ref[...] = jnp.dot(u, win_ref[:, _OFF_VSB:_OFF_VSB + SB_WIDTH], preferred_element_type=F32)
    _inproj_shared(u, win_ref, cos_ref, sin_ref, *shared_refs)


def _row_spec(width):
    return pl.BlockSpec((ROW_TILE, width), lambda i: (i, 0))


def _shared_outputs(n, d):
    widths_dtypes = [(RET_QK_WIDTH, F32), (RET_QK_WIDTH, F32), (RET_V_WIDTH, BF16),
                     (RET_V_WIDTH, F32), (d, F32), (d, F32)]
    return ([_row_spec(w) for w, _ in widths_dtypes],
            [jax.ShapeDtypeStruct((n, w), dt) for w, dt in widths_dtypes])


def _inproj_prompt_call(h, g_pre, w_in, w_kv_t, cos_tab, sin_tab, batch, seq):
    n, d = h.shape
    assert ROW_TILE == SB_BLOCK
    tiles = seq // ROW_TILE
    tab = pl.BlockSpec((ROW_TILE, RET_QK_DIM), lambda i: (i % tiles, 0))
    t_spec = pl.BlockSpec((1, SB_WIDTH, ROW_TILE), lambda i: (i // tiles, 0, i % tiles))
    shared_specs, shared_shapes = _shared_outputs(n, d)
    return pl.pallas_call(
        _inproj_prompt_kernel,
        grid=(n // ROW_TILE,),
        in_specs=[_row_spec(d), _const_spec(g_pre.shape), _const_spec(w_in.shape),
                  _const_spec(w_kv_t.shape), tab, tab],
        out_specs=[t_spec, t_spec,
                   pl.BlockSpec((1, 1, SB_WIDTH, SB_BLOCK), lambda i: (i // tiles, i % tiles, 0, 0)),
                   _row_spec(SB_WIDTH)] + [_row_spec(SB_WIDTH)] * HEADS_PER_LANE_TILE + shared_specs,
        out_shape=[jax.ShapeDtypeStruct((batch, SB_WIDTH, seq), F32),
                   jax.ShapeDtypeStruct((batch, SB_WIDTH, seq), F32),
                   jax.ShapeDtypeStruct((batch, tiles, SB_WIDTH, SB_BLOCK), BF16),
                   jax.ShapeDtypeStruct((n, SB_WIDTH), BF16)]
                  + [jax.ShapeDtypeStruct((n, SB_WIDTH), BF16)] * HEADS_PER_LANE_TILE + shared_shapes,
        compiler_params=_params(1),
        name="in_projection_prompt",
    )(h, g_pre, w_in, w_kv_t, cos_tab, sin_tab)


def _inproj_decode_call(h, g_pre, w_in, cos_tab, sin_tab):
    n, d = h.shape
    tab = pl.BlockSpec((ROW_TILE, RET_QK_DIM), lambda i: (0, 0))
    shared_specs, shared_shapes = _shared_outputs(n, d)
    return pl.pallas_call(
        _inproj_decode_kernel,
        grid=(n // ROW_TILE,),
        in_specs=[_row_spec(d), _const_spec(g_pre.shape), _const_spec(w_in.shape), tab, tab],
        out_specs=[_row_spec(SB_WIDTH)] * 3 + shared_specs,
        out_shape=[jax.ShapeDtypeStruct((n, SB_WIDTH), F32)] * 3 + shared_shapes,
        compiler_params=_params(1),
        name="in_projection_decode",
    )(h, g_pre, w_in, cos_tab, sin_tab)


NULL_LOGIT = -1e30


def _softplus2(z2):
    neg_abs = pltpu.bitcast(pltpu.bitcast(z2, jnp.uint32) | jnp.uint32(0x80000000), F32)
    return jnp.maximum(z2, 0.0) + jnp.log(1.0 + jnp.exp2(neg_abs)) * LOG2E


def _sb_weights(z2, tri2, run2, mask):
    if mask is not None:
        z2 = jnp.where(mask, z2, NULL_LOGIT)
    sp2 = _softplus2(z2)
    hi = sp2.astype(BF16)
    lo = (sp2 - hi.astype(F32)).astype(BF16)
    csum = jnp.dot(jnp.concatenate([hi, lo], axis=1), tri2, preferred_element_type=F32) + run2
    return jnp.exp2(z2 - csum).astype(BF16), run2 + jnp.sum(sp2, axis=-1, keepdims=True)


_ITEM_QI, _ITEM_J, _ITEM_BIAS, _ITEM_FIRST, _ITEM_FIELDS = 0, 1, 2, 3, 4
_BIAS_FULL, _BIAS_DIAG, _BIAS_NULL = 0, 1, 2
_PIPE_DEPTH = 4


def _sb_items(n_blocks):
    pad = _PIPE_DEPTH - 1
    null = (0, 0, _BIAS_NULL, 1)
    items = [null] * pad
    for qi in range(n_blocks):
        for j in range(qi, -1, -1):
            items.append((qi, j, _BIAS_DIAG if j == qi else _BIAS_FULL, int(j == qi)))
    items += [null] * pad
    return jnp.asarray(items, jnp.int32).T.reshape(-1), len(items)


def _sb_prompt_kernel(items_ref, bias_ref, *refs, n_items):
    nh = HEADS_PER_LANE_TILE
    q_refs, (kt_ref, v_ref, tri_ref, o_ref) = refs[:nh], refs[nh:nh + 4]
    bsel_scr, zraw_scr, z_scr, sp_scr, c_scr, acc_scr, run_scr = refs[nh + 4:]
    hp = pl.program_id(1)
    blk = tri_ref.shape[0]
    tri = tri_ref[...]
    lane_head = lax.broadcasted_iota(jnp.int32, (1, LANES), 1) // SB_HEAD_DIM
    row = lax.broadcasted_iota(jnp.int32, (blk, blk), 0)
    col = lax.broadcasted_iota(jnp.int32, (blk, blk), 1)
    heads = range(nh)

    def item(field, i):
        return items_ref[field * n_items + i]

    for hh in heads:
        bias2 = bias_ref[hp * nh + hh] * LOG2E
        bsel_scr[hh, _BIAS_FULL] = jnp.full((blk, blk), bias2, F32)
        bsel_scr[hh, _BIAS_DIAG] = jnp.where(col < row, bias2, NULL_LOGIT)
        bsel_scr[hh, _BIAS_NULL] = jnp.full((blk, blk), NULL_LOGIT, F32)
        zraw_scr[hh] = jnp.full((blk, blk), NULL_LOGIT, F32)
        z_scr[0, hh] = jnp.full((blk, blk), NULL_LOGIT, F32)
        z_scr[1, hh] = jnp.full((blk, blk), NULL_LOGIT, F32)
        sp_scr[hh] = jnp.zeros((blk, blk), BF16)
        c_scr[hh] = jnp.zeros((blk, blk), F32)
        acc_scr[hh] = jnp.zeros((blk, LANES), F32)
        run_scr[hh] = jnp.zeros((blk, 1), F32)

    def trip(t, _):
        slot = t % 2
        qi_w = item(_ITEM_QI, t)
        keep_w = 1.0 - item(_ITEM_FIRST, t).astype(F32)
        keep_s = 1.0 - item(_ITEM_FIRST, t + 2).astype(F32)
        qi_l = item(_ITEM_QI, t + 3)
        j_l = item(_ITEM_J, t + 3)
        bias_l = item(_ITEM_BIAS, t + 3)
        v = v_ref[0, pl.ds(pl.multiple_of(item(_ITEM_J, t) * blk, blk), blk), :]
        q_rows = pl.ds(pl.multiple_of(qi_l * blk, blk), blk)
        out = None
        for hh in heads:
            a = jnp.exp2(z_scr[slot, hh] - c_scr[hh]).astype(BF16)
            acc = acc_scr[hh] * keep_w + jnp.dot(a, v, preferred_element_type=F32)
            acc_scr[hh] = acc
            out = acc if out is None else jnp.where(lane_head == hh, acc, out)
            c_scr[hh] = jnp.dot(sp_scr[hh], tri, preferred_element_type=F32)
            z2 = zraw_scr[hh]
            sp2 = _softplus2(z2)
            run2 = run_scr[hh] * keep_s
            z_scr[slot, hh] = z2 - run2
            sp_scr[hh] = sp2.astype(BF16)
            run_scr[hh] = run2 + jnp.sum(sp2, axis=-1, keepdims=True)
            zraw_scr[hh] = (jnp.dot(q_refs[hh][0, q_rows, :], kt_ref[0, j_l], preferred_element_type=F32)
                            + bsel_scr[hh, bias_l])
        o_ref[0, pl.ds(pl.multiple_of(qi_w * blk, blk), blk), :] = out.astype(o_ref.dtype)
        return 0

    lax.fori_loop(0, n_items - (_PIPE_DEPTH - 1), trip, 0)


def _sb_prompt_call(q_slots, kt_bf, v_bf, bias, tri):
    b, t, w = v_bf.shape
    blk = SB_BLOCK
    nh = HEADS_PER_LANE_TILE
    items, n_items = _sb_items(t // blk)
    seq_spec = pl.BlockSpec((1, t, LANES), lambda bi, hp, it: (bi, 0, hp))
    grid_spec = pltpu.PrefetchScalarGridSpec(
        num_scalar_prefetch=1,
        grid=(b, w // LANES),
        in_specs=([pl.BlockSpec(memory_space=pltpu.SMEM)] + [seq_spec] * nh
                  + [pl.BlockSpec((1, t // blk, LANES, blk), lambda bi, hp, it: (bi, 0, hp, 0)),
                     seq_spec,
                     pl.BlockSpec(tri.shape, lambda bi, hp, it: (0, 0))]),
        out_specs=seq_spec,
        scratch_shapes=[pltpu.VMEM((nh, 3, blk, blk), F32), pltpu.VMEM((nh, blk, blk), F32),
                        pltpu.VMEM((2, nh, blk, blk), F32), pltpu.VMEM((nh, blk, blk), BF16),
                        pltpu.VMEM((nh, blk, blk), F32), pltpu.VMEM((nh, blk, LANES), F32),
                        pltpu.VMEM((nh, blk, 1), F32)],
    )
    return pl.pallas_call(
        functools.partial(_sb_prompt_kernel, n_items=n_items),
        grid_spec=grid_spec,
        out_shape=jax.ShapeDtypeStruct((b, t, w), BF16),
        compiler_params=_params(2),
        name="sb_attention_prompt",
    )(items, bias, *q_slots, kt_bf, v_bf, tri)


def _sb_decode_kernel(pt_ref, q_ref, kn_ref, vn_ref, bias_ref, tri_ref, *rest, pages):
    del pt_ref
    k_refs = rest[:pages]
    v_refs = rest[pages:2 * pages]
    o_ref = rest[2 * pages]
    qbd_ref, run_ref, acc_ref = rest[2 * pages + 1:]
    g = pl.program_id(1)
    dec_t = q_ref.shape[1]
    rows = SB_HEADS * dec_t
    blk = tri_ref.shape[1]
    bias = bias_ref[...] * LOG2E
    tri = tri_ref[...]
    lane_head = lax.broadcasted_iota(jnp.int32, (1, SB_WIDTH), 1) // SB_HEAD_DIM

    @pl.when(g == 0)
    def _():
        q = q_ref[0] * LOG2E
        qbd = jnp.concatenate([jnp.where(lane_head == hh, q, 0.0) for hh in range(SB_HEADS)], axis=0)
        qbd_ref[...] = qbd.astype(BF16)
        pad = jnp.zeros((blk - dec_t, SB_WIDTH), F32)
        k_new = jnp.concatenate([kn_ref[0], pad], axis=0).astype(BF16)
        v_new = jnp.concatenate([vn_ref[0], pad], axis=0).astype(BF16)
        t_row = lax.broadcasted_iota(jnp.int32, (rows, blk), 0) % dec_t
        col = lax.broadcasted_iota(jnp.int32, (rows, blk), 1)
        z = lax.dot_general(qbd_ref[...], k_new, _NT, preferred_element_type=F32) + bias
        a, run = _sb_weights(z, tri, jnp.zeros((rows, 1), F32), col < t_row)
        run_ref[...] = jnp.broadcast_to(run, run_ref.shape)
        acc_ref[...] = jnp.dot(a, v_new, preferred_element_type=F32)

    qbd = qbd_ref[...]
    run = run_ref[:, 0:1]
    acc = acc_ref[...]
    pages_per_blk = blk // k_refs[0].shape[2]
    for p in reversed(range(pages // pages_per_blk)):
        sl = slice(p * pages_per_blk, (p + 1) * pages_per_blk)
        k_t = jnp.concatenate([r[0] for r in k_refs[sl]], axis=1).astype(BF16)
        v_t = jnp.concatenate([r[0] for r in v_refs[sl]], axis=1).astype(BF16)
        z = jnp.dot(qbd, k_t, preferred_element_type=F32) + bias
        a, run = _sb_weights(z, tri, run, None)
        acc = acc + lax.dot_general(a, v_t, _NT, preferred_element_type=F32)
    run_ref[...] = jnp.broadcast_to(run, run_ref.shape)
    acc_ref[...] = acc

    @pl.when(g == pl.num_programs(1) - 1)
    def _():
        out = jnp.zeros((dec_t, SB_WIDTH), F32)
        for hh in range(SB_HEADS):
            out = jnp.where(lane_head == hh, acc[hh * dec_t:(hh + 1) * dec_t, :], out)
        o_ref[0] = out


def _sb_decode_call(q, k_new, v_new, cache_kt, cache_vt, page_table, bias_rows, tri):
    n_seq, dec_t, w = q.shape
    n_pages = page_table.shape[1]
    pages = DEC_PAGES_PER_STEP
    page = cache_kt.shape[2]
    rows = SB_HEADS * dec_t
    seq_spec = pl.BlockSpec((1, dec_t, w), lambda s, g, pt: (s, 0, 0))

    def page_spec(i):
        return pl.BlockSpec((1, w, page), lambda s, g, pt: (pt[s, n_pages - (g + 1) * pages + i], 0, 0))

    def const(shape):
        nd = len(shape)
        return pl.BlockSpec(shape, lambda s, g, pt: (0,) * nd)

    grid_spec = pltpu.PrefetchScalarGridSpec(
        num_scalar_prefetch=1,
        grid=(n_seq, n_pages // pages),
        in_specs=([seq_spec, seq_spec, seq_spec, const(bias_rows.shape), const(tri.shape)]
                  + [page_spec(i) for i in range(pages)] * 2),
        out_specs=seq_spec,
        scratch_shapes=[pltpu.VMEM((rows, w), BF16), pltpu.VMEM((rows, LANES), F32),
                        pltpu.VMEM((rows, w), F32)],
    )
    return pl.pallas_call(
        functools.partial(_sb_decode_kernel, pages=pages),
        grid_spec=grid_spec,
        out_shape=jax.ShapeDtypeStruct((n_seq, dec_t, w), F32),
        compiler_params=_params(2),
        name="sb_attention_decode",
    )(page_table, q, k_new, v_new, bias_rows, tri, *([cache_kt] * pages), *([cache_vt] * pages))


def _head_norm_gate(o, gn, gate):
    o = o * lax.rsqrt(jnp.mean(o * o, axis=-1, keepdims=True) + NORM_EPS) * gn
    return (_silu(gate) * o).astype(BF16)


def _ret_prompt_kernel(dc_ref, q_ref, k_ref, v_ref, g_ref, gn_ref, din_ref, dq_ref, dk_ref,
                       o_ref, s_out_ref, s_ref):
    hh = pl.program_id(1)
    c = pl.program_id(2)

    @pl.when(c == 0)
    def _():
        s_ref[...] = jnp.zeros_like(s_ref)

    dc = dc_ref[hh]
    din = din_ref[0]
    dq = dq_ref[0]
    dk = dk_ref[0]
    gn = gn_ref[...]
    state = s_ref[...]
    for i in range(q_ref.shape[1] // RET_CHUNK):
        sl = slice(i * RET_CHUNK, (i + 1) * RET_CHUNK)
        q = q_ref[0, sl, :].astype(BF16)
        k = k_ref[0, sl, :]
        v = v_ref[0, sl, :]
        inner = lax.dot_general(q, k.astype(BF16), _NT, preferred_element_type=F32) * din
        o = (jnp.dot(inner.astype(BF16), v, preferred_element_type=F32)
             + jnp.dot(q, state.astype(BF16), preferred_element_type=F32) * dq)
        kd_t = (k * dk).T.astype(BF16)
        state = state * dc + jnp.dot(kd_t, v, preferred_element_type=F32)
        o_ref[0, sl, :] = _head_norm_gate(o, gn, g_ref[0, sl, :])
    s_ref[...] = state

    @pl.when(c == pl.num_programs(2) - 1)
    def _():
        s_out_ref[0, 0] = state


def _ret_prompt_call(q, k, v_bf, gate, gn, tables):
    b, t, _ = q.shape
    din, dq, dk, dc = tables
    rows = RET_ROWS_PER_STEP
    qk_spec = pl.BlockSpec((1, rows, RET_QK_DIM), lambda bi, h, c: (bi, c, h))
    v_spec = pl.BlockSpec((1, rows, RET_V_DIM), lambda bi, h, c: (bi, c, h))

    def head(shape):
        return pl.BlockSpec((1,) + shape, lambda bi, h, c: (h, 0, 0))

    return pl.pallas_call(
        _ret_prompt_kernel,
        grid=(b, RET_HEADS, t // rows),
        in_specs=[pl.BlockSpec(memory_space=pltpu.SMEM), qk_spec, qk_spec, v_spec, v_spec,
                  pl.BlockSpec((1, RET_V_DIM), lambda bi, h, c: (0, h)),
                  head((RET_CHUNK, RET_CHUNK)), head((RET_CHUNK, RET_V_DIM)),
                  head((RET_CHUNK, RET_QK_DIM))],
        out_specs=[v_spec,
                   pl.BlockSpec((1, 1, RET_QK_DIM, RET_V_DIM), lambda bi, h, c: (bi, h, 0, 0))],
        out_shape=[jax.ShapeDtypeStruct((b, t, RET_V_WIDTH), BF16),
                   jax.ShapeDtypeStruct((b, RET_HEADS, RET_QK_DIM, RET_V_DIM), F32)],
        scratch_shapes=[pltpu.VMEM((RET_QK_DIM, RET_V_DIM), F32)],
        compiler_params=_params(3),
        name="retention_prompt",
    )(dc, q, k, v_bf, gate, gn, din, dq, dk)


def _ret_decode_kernel(dc_ref, q_ref, k_ref, v_ref, g_ref, gn_ref, s_in_ref, din_ref, dq_ref, dk_ref,
                       o_ref, s_out_ref, *, dec_t):
    hh = pl.program_id(0)
    n_seq = s_in_ref.shape[0]
    rows = n_seq * dec_t
    q = q_ref[...]
    k = k_ref[...]
    v = v_ref[...]
    q_bf = q.astype(BF16)
    inner = lax.dot_general(q_bf, k.astype(BF16), _NT, preferred_element_type=F32) * din_ref[0]
    o = jnp.dot(inner.astype(BF16), v, preferred_element_type=F32)

    s_old = s_in_ref[:, 0].reshape(n_seq * RET_QK_DIM, RET_V_DIM)
    row_seq = lax.broadcasted_iota(jnp.int32, (rows, RET_QK_DIM), 0) // dec_t
    q_bd = jnp.concatenate([jnp.where(row_seq == s, q, 0.0) for s in range(n_seq)], axis=1)
    o = o + jnp.dot(q_bd.astype(BF16), s_old.astype(BF16), preferred_element_type=F32) * dq_ref[0]
    o_ref[...] = _head_norm_gate(o, gn_ref[...], g_ref[...])

    kd_t = (k * dk_ref[0]).T
    col_seq = lax.broadcasted_iota(jnp.int32, (RET_QK_DIM, rows), 1) // dec_t
    k_bd_t = jnp.concatenate([jnp.where(col_seq == s, kd_t, 0.0) for s in range(n_seq)], axis=0)
    s_new = s_old * dc_ref[hh] + jnp.dot(k_bd_t.astype(BF16), v, preferred_element_type=F32)
    s_out_ref[:, 0] = s_new.reshape(n_seq, RET_QK_DIM, RET_V_DIM)


def _ret_decode_call(q, k, v_bf, gate, gn, state, tables, dec_t):
    rows = q.shape[0]
    n_seq = state.shape[0]
    din, dq, dk, dc = tables
    qk_spec = pl.BlockSpec((rows, RET_QK_DIM), lambda h: (0, h))
    v_spec = pl.BlockSpec((rows, RET_V_DIM), lambda h: (0, h))
    s_spec = pl.BlockSpec((n_seq, 1, RET_QK_DIM, RET_V_DIM), lambda h: (0, h, 0, 0))

    def head(shape):
        return pl.BlockSpec((1,) + shape, lambda h: (h, 0, 0))

    return pl.pallas_call(
        functools.partial(_ret_decode_kernel, dec_t=dec_t),
        grid=(RET_HEADS,),
        in_specs=[pl.BlockSpec(memory_space=pltpu.SMEM), qk_spec, qk_spec, v_spec, v_spec,
                  pl.BlockSpec((1, RET_V_DIM), lambda h: (0, h)), s_spec,
                  head((rows, rows)), head((rows, RET_V_DIM)), head((rows, RET_QK_DIM))],
        out_specs=[v_spec, s_spec],
        out_shape=[jax.ShapeDtypeStruct((rows, RET_V_WIDTH), BF16),
                   jax.ShapeDtypeStruct(state.shape, F32)],
        compiler_params=_params(1),
        name="retention_decode",
    )(dc, q, k, v_bf, gate, gn, state, din, dq, dk)


def _post_kernel(h_ref, osb_ref, or_ref, asb_ref, ar_ref, wsb_ref, wret_ref, wo_ref, gmix_ref,
                 gpre_ref, wgu_ref, wdown_ref, gpost_ref, o_ref):
    m = (jax.nn.sigmoid(asb_ref[...]) * jnp.dot(osb_ref[...], wsb_ref[...], preferred_element_type=F32)
         + jax.nn.sigmoid(ar_ref[...]) * jnp.dot(or_ref[...], wret_ref[...], preferred_element_type=F32))
    mix = jnp.dot(m.astype(BF16), wo_ref[...], preferred_element_type=F32)
    h = h_ref[...] + _rms(mix, gmix_ref[...])
    o_ref[...] = _ffn_residual(h, gpre_ref[...], wgu_ref, wdown_ref, gpost_ref[...])


def _post_call(h, o_sb, o_r, a_sb, a_r, w_sb, w_ret, w_o, g_mix, g_pre, wgu, wdown, g_post):
    n, d = h.shape
    consts = [w_sb, w_ret, w_o, g_mix, g_pre, wgu, wdown, g_post]
    return pl.pallas_call(
        _post_kernel,
        grid=(n // ROW_TILE,),
        in_specs=[_row_spec(d), _row_spec(SB_WIDTH), _row_spec(RET_V_WIDTH), _row_spec(d), _row_spec(d)]
                 + [_const_spec(c.shape) for c in consts],
        out_specs=_row_spec(d),
        out_shape=jax.ShapeDtypeStruct((n, d), F32),
        compiler_params=_params(1),
        name="merge_out_ffn",
    )(h, o_sb, o_r, a_sb, a_r, *consts)


def _rope_tables(pos):
    half = RET_QK_DIM // 2
    freq = ROPE_BASE ** (-jnp.arange(half, dtype=F32) / half)
    ang = pos.astype(F32)[:, None] * freq[None, :]
    cos, sin = jnp.cos(ang), jnp.sin(ang)
    return jnp.concatenate([cos, cos], axis=1), jnp.concatenate([-sin, sin], axis=1)


def _decay_tables(chunk, reps):
    log_gamma = jnp.log1p(-jnp.exp2(-5.0 - jnp.arange(RET_HEADS, dtype=F32)))
    idx = jnp.arange(chunk, dtype=F32)
    diff = idx[:, None] - idx[None, :]
    d_in = jnp.where(diff >= 0, jnp.exp(log_gamma[:, None, None] * jnp.maximum(diff, 0.0)), 0.0)
    d_q = jnp.exp(log_gamma[:, None] * (idx + 1.0))
    d_k = jnp.exp(log_gamma[:, None] * (chunk - 1.0 - idx))
    d_c = jnp.exp(log_gamma * chunk)
    if reps > 1:
        seq = jnp.arange(chunk * reps) // chunk
        d_in = jnp.where(seq[:, None] == seq[None, :], jnp.tile(d_in, (1, reps, reps)), 0.0)
        d_q = jnp.tile(d_q, (1, reps))
        d_k = jnp.tile(d_k, (1, reps))
    n = chunk * reps
    d_q = jnp.broadcast_to(d_q[:, :, None], (RET_HEADS, n, RET_V_DIM))
    d_k = jnp.broadcast_to(d_k[:, :, None], (RET_HEADS, n, RET_QK_DIM))
    return d_in, d_q, d_k, d_c


def _tri(n):
    i = jnp.arange(n)
    return (i[:, None] >= i[None, :]).astype(BF16)


def kernel(x_prompt, x_sample, cache_k, cache_v, state_ret, page_table, g_ffn1_pre, w_ffn1_gu, w_ffn1_down, g_ffn1_post, g_mix_pre, w_in, sb_bias, ret_gn_g, w_sb_out, w_ret_out, w_o, g_mix_post, g_ffn2_pre, w_ffn2_gu, w_ffn2_down, g_ffn2_post):
    batch, seq, d = x_prompt.shape
    n_seq, dec_t, _ = x_sample.shape
    depth = w_in.shape[0]
    n_pool = cache_k.shape[1]
    n_pages = page_table.shape[1]
    page = cache_k.shape[2]
    past_len = n_pages * page
    assert HEADS_PER_LANE_TILE == 2
    assert seq % RET_ROWS_PER_STEP == 0 and seq % SB_BLOCK == 0 and seq % ROW_TILE == 0
    assert (n_seq * dec_t) % ROW_TILE == 0 and ROW_TILE % dec_t == 0
    assert n_pages % DEC_PAGES_PER_STEP == 0 and SB_BLOCK % page == 0
    assert (DEC_PAGES_PER_STEP * page) % SB_BLOCK == 0 and dec_t % 8 == 0 and dec_t <= SB_BLOCK
    assert dec_t % RET_CHUNK != 0

    rope_p = _rope_tables(jnp.arange(seq))
    rope_s = _rope_tables(past_len + jnp.arange(ROW_TILE) % dec_t)
    decay_p = _decay_tables(RET_CHUNK, 1)
    decay_s = _decay_tables(dec_t, n_seq)
    tri = _tri(SB_BLOCK)
    tri2 = jnp.concatenate([tri, tri], axis=0)

    hp = x_prompt.reshape(batch * seq, d)
    hs = x_sample.reshape(n_seq * dec_t, d)
    kp_l, vp_l, sp_l, ks_l, vs_l, ss_l = [], [], [], [], [], []
    for l in range(depth):
        g1pre, g1post = g_ffn1_pre[l][None], g_ffn1_post[l][None]
        g2pre, g2post = g_ffn2_pre[l][None], g_ffn2_post[l][None]
        gmpre, gmpost = g_mix_pre[l][None], g_mix_post[l][None]
        gn = ret_gn_g[l][None]
        w1gu, w1down = w_ffn1_gu[l].astype(BF16), w_ffn1_down[l].astype(BF16)
        w2gu, w2down = w_ffn2_gu[l].astype(BF16), w_ffn2_down[l].astype(BF16)
        win = w_in[l].astype(BF16)
        w_kv_t = w_in[l][:, _OFF_KSB:_OFF_QR].T.astype(BF16)
        wsb, wret, wo = w_sb_out[l].astype(BF16), w_ret_out[l].astype(BF16), w_o[l].astype(BF16)
        bias = sb_bias[l].astype(F32)
        bias_rows = jnp.broadcast_to(jnp.repeat(bias, dec_t)[:, None], (SB_HEADS * dec_t, SB_BLOCK))
        ck_t = cache_k[l].transpose(0, 2, 3, 1).reshape(n_pool, SB_WIDTH, page)
        cv_t = cache_v[l].transpose(0, 2, 3, 1).reshape(n_pool, SB_WIDTH, page)

        h1 = _ffn_call(hp, g1pre, w1gu, w1down, g1post)
        (k_t, v_t, kt_bf, v_bf, q_s0, q_s1, q_r, k_r, v_r, g_r, a_sb, a_r) = _inproj_prompt_call(
            h1, gmpre, win, w_kv_t, *rope_p, batch, seq)
        o_sb = _sb_prompt_call([q_s0.reshape(batch, seq, SB_WIDTH), q_s1.reshape(batch, seq, SB_WIDTH)],
                               kt_bf, v_bf.reshape(batch, seq, SB_WIDTH), bias, tri)
        o_r, s_p = _ret_prompt_call(q_r.reshape(batch, seq, -1), k_r.reshape(batch, seq, -1),
                                    v_r.reshape(batch, seq, -1), g_r.reshape(batch, seq, -1),
                                    gn, decay_p)
        hp = _post_call(h1, o_sb.reshape(batch * seq, SB_WIDTH), o_r.reshape(batch * seq, RET_V_WIDTH),
                        a_sb, a_r, wsb, wret, wo, gmpost, g2pre, w2gu, w2down, g2post)
        kp_l.append(k_t.reshape(batch, SB_HEADS, SB_HEAD_DIM, seq).transpose(0, 3, 1, 2))
        vp_l.append(v_t.reshape(batch, SB_HEADS, SB_HEAD_DIM, seq).transpose(0, 3, 1, 2))
        sp_l.append(s_p)

        h1 = _ffn_call(hs, g1pre, w1gu, w1down, g1post)
        (k_sb, v_sb, q_sb, q_r, k_r, v_r, g_r, a_sb, a_r) = _inproj_decode_call(
            h1, gmpre, win, *rope_s)
        o_sb = _sb_decode_call(q_sb.reshape(n_seq, dec_t, SB_WIDTH), k_sb.reshape(n_seq, dec_t, SB_WIDTH),
                               v_sb.reshape(n_seq, dec_t, SB_WIDTH), ck_t, cv_t, page_table, bias_rows, tri2)
        o_r, s_s = _ret_decode_call(q_r, k_r, v_r, g_r, gn, state_ret[l], decay_s, dec_t)
        hs = _post_call(h1, o_sb.reshape(n_seq * dec_t, SB_WIDTH).astype(BF16), o_r, a_sb, a_r,
                        wsb, wret, wo, gmpost, g2pre, w2gu, w2down, g2post)
        ks_l.append(k_sb.reshape(n_seq, dec_t, SB_HEADS, SB_HEAD_DIM))
        vs_l.append(v_sb.reshape(n_seq, dec_t, SB_HEADS, SB_HEAD_DIM))
        ss_l.append(s_s)

    return (hp.reshape(batch, seq, d), hs.reshape(n_seq, dec_t, d),
            jnp.stack(kp_l), jnp.stack(vp_l), jnp.stack(sp_l),
            jnp.stack(ks_l), jnp.stack(vs_l), jnp.stack(ss_l))
```

```python
import functools

import jax
import jax.numpy as jnp
from jax import lax
from jax.experimental import pallas as pl
from jax.experimental.pallas import tpu as pltpu

F32 = jnp.float32
BF16 = jnp.bfloat16

SB_HEADS = 8
SB_HEAD_DIM = 64
SB_WIDTH = SB_HEADS * SB_HEAD_DIM
RET_HEADS = 4
RET_QK_DIM = 128
RET_V_DIM = 256
RET_QK_WIDTH = RET_HEADS * RET_QK_DIM
RET_V_WIDTH = RET_HEADS * RET_V_DIM
RET_CHUNK = 128
ROPE_BASE = 10000.0
NORM_EPS = 1e-6
LOG2E = 1.4426950408889634

LANES = 128
HEADS_PER_LANE_TILE = LANES // SB_HEAD_DIM

ROW_TILE = 256
SB_BLOCK = 256
DEC_PAGES_PER_STEP = 16
RET_ROWS_PER_STEP = 1024
VMEM_LIMIT = 56 * 1024 * 1024

_NT = (((1,), (1,)), ((), ()))


def _const_spec(shape):
    nd = len(shape)
    return pl.BlockSpec(shape, lambda *_: (0,) * nd, pipeline_mode=pl.Buffered(1))


def _params(n_axes, vmem=VMEM_LIMIT):
    return pltpu.CompilerParams(dimension_semantics=("arbitrary",) * n_axes,
                                vmem_limit_bytes=vmem)


def _rms(x, g):
    ms = jnp.mean(x * x, axis=-1, keepdims=True)
    return x * lax.rsqrt(ms + NORM_EPS) * g


def _silu(x):
    return x * jax.nn.sigmoid(x)


def _ffn_residual(x, g_pre, wgu_ref, wdown_ref, g_post):
    d_ff = wdown_ref.shape[0]
    xn = _rms(x, g_pre).astype(BF16)
    gate = jnp.dot(xn, wgu_ref[:, :d_ff], preferred_element_type=F32)
    up = jnp.dot(xn, wgu_ref[:, d_ff:], preferred_element_type=F32)
    act = (_silu(gate) * up).astype(BF16)
    y = jnp.dot(act, wdown_ref[...], preferred_element_type=F32)
    return x + 0.5 * _rms(y, g_post)


def _ffn_kernel(x_ref, gpre_ref, wgu_ref, wdown_ref, gpost_ref, o_ref):
    o_ref[...] = _ffn_residual(x_ref[...], gpre_ref[...], wgu_ref, wdown_ref, gpost_ref[...])


def _ffn_call(x, g_pre, wgu, wdown, g_post):
    n, d = x.shape
    row = pl.BlockSpec((ROW_TILE, d), lambda i: (i, 0))
    return pl.pallas_call(
        _ffn_kernel,
        grid=(n // ROW_TILE,),
        in_specs=[row, _const_spec(g_pre.shape), _const_spec(wgu.shape),
                  _const_spec(wdown.shape), _const_spec(g_post.shape)],
        out_specs=row,
        out_shape=jax.ShapeDtypeStruct((n, d), F32),
        compiler_params=_params(1),
        name="ffn_block",
    )(x, g_pre, wgu, wdown, g_post)


_OFF_QSB = 0
_OFF_KSB = _OFF_QSB + SB_WIDTH
_OFF_VSB = _OFF_KSB + SB_WIDTH
_OFF_QR = _OFF_VSB + SB_WIDTH
_OFF_KR = _OFF_QR + RET_QK_WIDTH
_OFF_VR = _OFF_KR + RET_QK_WIDTH
_OFF_GR = _OFF_VR + RET_V_WIDTH
_OFF_ASB = _OFF_GR + RET_V_WIDTH


def _inproj_shared(u, win_ref, cos_ref, sin_ref, qr_ref, kr_ref, vr_ref, gr_ref, asb_ref, ar_ref):
    def proj(lo, width):
        return jnp.dot(u, win_ref[:, lo:lo + width], preferred_element_type=F32)

    cos = cos_ref[...]
    sin = sin_ref[...]
    q_r = proj(_OFF_QR, RET_QK_WIDTH)
    k_r = proj(_OFF_KR, RET_QK_WIDTH)
    for hh in range(RET_HEADS):
        sl = slice(hh * RET_QK_DIM, (hh + 1) * RET_QK_DIM)
        qh = q_r[:, sl]
        kh = k_r[:, sl]
        qr_ref[:, sl] = qh * cos + pltpu.roll(qh, RET_QK_DIM // 2, axis=1) * sin
        kr_ref[:, sl] = (kh * cos + pltpu.roll(kh, RET_QK_DIM // 2, axis=1) * sin) * (RET_QK_DIM ** -0.5)
    vr_ref[...] = proj(_OFF_VR, RET_V_WIDTH).astype(BF16)
    gr_ref[...] = proj(_OFF_GR, RET_V_WIDTH)
    d_model = asb_ref.shape[1]
    asb_ref[...] = proj(_OFF_ASB, d_model)
    ar_ref[...] = proj(_OFF_ASB + d_model, d_model)


def _inproj_prompt_kernel(h_ref, g_ref, win_ref, wkvt_ref, cos_ref, sin_ref,
                          kt_ref, vt_ref, ktb_ref, vb_ref, *rest):
    q_refs, shared_refs = rest[:HEADS_PER_LANE_TILE], rest[HEADS_PER_LANE_TILE:]
    u = _rms(h_ref[...], g_ref[...]).astype(BF16)
    q = jnp.dot(u, win_ref[:, _OFF_QSB:_OFF_QSB + SB_WIDTH], preferred_element_type=F32)
    q = q * (SB_HEAD_DIM ** -0.5 * LOG2E)
    head_slot = (lax.broadcasted_iota(jnp.int32, (1, SB_WIDTH), 1) // SB_HEAD_DIM) % HEADS_PER_LANE_TILE
    for hh, q_ref in enumerate(q_refs):
        q_ref[...] = jnp.where(head_slot == hh, q, 0.0).astype(BF16)
    kv_t = lax.dot_general(wkvt_ref[...], u, _NT, preferred_element_type=F32)
    kt_ref[0] = kv_t[:SB_WIDTH]
    vt_ref[0] = kv_t[SB_WIDTH:]
    ktb_ref[0, 0] = kv_t[:SB_WIDTH].astype(BF16)
    vb_ref[...] = jnp.dot(u, win_ref[:, _OFF_VSB:_OFF_VSB + SB_WIDTH],
                          preferred_element_type=F32).astype(BF16)
    _inproj_shared(u, win_ref, cos_ref, sin_ref, *shared_refs)


def _inproj_decode_kernel(h_ref, g_ref, win_ref, cos_ref, sin_ref, k_ref, v_ref, q_ref, *shared_refs):
    u = _rms(h_ref[...], g_ref[...]).astype(BF16)
    q_ref[...] = jnp.dot(u, win_ref[:, _OFF_QSB:_OFF_QSB + SB_WIDTH],
                         preferred_element_type=F32) * (SB_HEAD_DIM ** -0.5)
    k_ref[...] = jnp.dot(u, win_ref[:, _OFF_KSB:_OFF_KSB + SB_WIDTH], preferred_element_type=F32)
    v_ref[...] = jnp.dot(u, win_ref[:, _OFF_VSB:_OFF_VSB + SB_WIDTH], preferred_element_type=F32)
    _inproj_shared(u, win_ref, cos_ref, sin_ref, *shared_refs)


def _row_spec(width):
    return pl.BlockSpec((ROW_TILE, width), lambda i: (i, 0))


def _shared_outputs(n, d):
    widths_dtypes = [(RET_QK_WIDTH, F32), (RET_QK_WIDTH, F32), (RET_V_WIDTH, BF16),
                     (RET_V_WIDTH, F32), (d, F32), (d, F32)]
    return ([_row_spec(w) for w, _ in widths_dtypes],
            [jax.ShapeDtypeStruct((n, w), dt) for w, dt in widths_dtypes])


def _inproj_prompt_call(h, g_pre, w_in, w_kv_t, cos_tab, sin_tab, batch, seq):
    n, d = h.shape
    assert ROW_TILE == SB_BLOCK
    tiles = seq // ROW_TILE
    tab = pl.BlockSpec((ROW_TILE, RET_QK_DIM), lambda i: (i % tiles, 0))
    t_spec = pl.BlockSpec((1, SB_WIDTH, ROW_TILE), lambda i: (i // tiles, 0, i % tiles))
    shared_specs, shared_shapes = _shared_outputs(n, d)
    return pl.pallas_call(
        _inproj_prompt_kernel,
        grid=(n // ROW_TILE,),
        in_specs=[_row_spec(d), _const_spec(g_pre.shape), _const_spec(w_in.shape),
                  _const_spec(w_kv_t.shape), tab, tab],
        out_specs=[t_spec, t_spec,
                   pl.BlockSpec((1, 1, SB_WIDTH, SB_BLOCK), lambda i: (i // tiles, i % tiles, 0, 0)),
                   _row_spec(SB_WIDTH)] + [_row_spec(SB_WIDTH)] * HEADS_PER_LANE_TILE + shared_specs,
        out_shape=[jax.ShapeDtypeStruct((batch, SB_WIDTH, seq), F32),
                   jax.ShapeDtypeStruct((batch, SB_WIDTH, seq), F32),
                   jax.ShapeDtypeStruct((batch, tiles, SB_WIDTH, SB_BLOCK), BF16),
                   jax.ShapeDtypeStruct((n, SB_WIDTH), BF16)]
                  + [jax.ShapeDtypeStruct((n, SB_WIDTH), BF16)] * HEADS_PER_LANE_TILE + shared_shapes,
        compiler_params=_params(1),
        name="in_projection_prompt",
    )(h, g_pre, w_in, w_kv_t, cos_tab, sin_tab)


def _inproj_decode_call(h, g_pre, w_in, cos_tab, sin_tab):
    n, d = h.shape
    tab = pl.BlockSpec((ROW_TILE, RET_QK_DIM), lambda i: (0, 0))
    shared_specs, shared_shapes = _shared_outputs(n, d)
    return pl.pallas_call(
        _inproj_decode_kernel,
        grid=(n // ROW_TILE,),
        in_specs=[_row_spec(d), _const_spec(g_pre.shape), _const_spec(w_in.shape), tab, tab],
        out_specs=[_row_spec(SB_WIDTH)] * 3 + shared_specs,
        out_shape=[jax.ShapeDtypeStruct((n, SB_WIDTH), F32)] * 3 + shared_shapes,
        compiler_params=_params(1),
        name="in_projection_decode",
    )(h, g_pre, w_in, cos_tab, sin_tab)


NULL_LOGIT = -1e30


def _softplus2(z2):
    neg_abs = pltpu.bitcast(pltpu.bitcast(z2, jnp.uint32) | jnp.uint32(0x80000000), F32)
    return jnp.maximum(z2, 0.0) + jnp.log(1.0 + jnp.exp2(neg_abs)) * LOG2E


def _sb_weights(z2, tri2, run2, mask):
    if mask is not None:
        z2 = jnp.where(mask, z2, NULL_LOGIT)
    sp2 = _softplus2(z2)
    hi = sp2.astype(BF16)
    lo = (sp2 - hi.astype(F32)).astype(BF16)
    csum = jnp.dot(jnp.concatenate([hi, lo], axis=1), tri2, preferred_element_type=F32) + run2
    return jnp.exp2(z2 - csum).astype(BF16), run2 + jnp.sum(sp2, axis=-1, keepdims=True)


_ITEM_QI, _ITEM_J, _ITEM_BIAS, _ITEM_FIRST, _ITEM_FIELDS = 0, 1, 2, 3, 4
_BIAS_FULL, _BIAS_DIAG, _BIAS_NULL = 0, 1, 2
_PIPE_DEPTH = 4


def _sb_items(n_blocks):
    pad = _PIPE_DEPTH - 1
    null = (0, 0, _BIAS_NULL, 1)
    items = [null] * pad
    for qi in range(n_blocks):
        for j in range(qi, -1, -1):
            items.append((qi, j, _BIAS_DIAG if j == qi else _BIAS_FULL, int(j == qi)))
    items += [null] * pad
    return jnp.asarray(items, jnp.int32).T.reshape(-1), len(items)


def _sb_prompt_kernel(items_ref, bias_ref, *refs, n_items):
    nh = HEADS_PER_LANE_TILE
    q_refs, (kt_ref, v_ref, tri_ref, o_ref) = refs[:nh], refs[nh:nh + 4]
    bsel_scr, zraw_scr, z_scr, sp_scr, c_scr, acc_scr, run_scr = refs[nh + 4:]
    hp = pl.program_id(1)
    blk = tri_ref.shape[0]
    tri = tri_ref[...]
    lane_head = lax.broadcasted_iota(jnp.int32, (1, LANES), 1) // SB_HEAD_DIM
    row = lax.broadcasted_iota(jnp.int32, (blk, blk), 0)
    col = lax.broadcasted_iota(jnp.int32, (blk, blk), 1)
    heads = range(nh)

    def item(field, i):
        return items_ref[field * n_items + i]

    for hh in heads:
        bias2 = bias_ref[hp * nh + hh] * LOG2E
        bsel_scr[hh, _BIAS_FULL] = jnp.full((blk, blk), bias2, F32)
        bsel_scr[hh, _BIAS_DIAG] = jnp.where(col < row, bias2, NULL_LOGIT)
        bsel_scr[hh, _BIAS_NULL] = jnp.full((blk, blk), NULL_LOGIT, F32)
        zraw_scr[hh] = jnp.full((blk, blk), NULL_LOGIT, F32)
        z_scr[0, hh] = jnp.full((blk, blk), NULL_LOGIT, F32)
        z_scr[1, hh] = jnp.full((blk, blk), NULL_LOGIT, F32)
        sp_scr[hh] = jnp.zeros((blk, blk), BF16)
        c_scr[hh] = jnp.zeros((blk, blk), F32)
        acc_scr[hh] = jnp.zeros((blk, LANES), F32)
        run_scr[hh] = jnp.zeros((blk, 1), F32)

    def trip(t, _):
        slot = t % 2
        qi_w = item(_ITEM_QI, t)
        keep_w = 1.0 - item(_ITEM_FIRST, t).astype(F32)
        keep_s = 1.0 - item(_ITEM_FIRST, t + 2).astype(F32)
        qi_l = item(_ITEM_QI, t + 3)
        j_l = item(_ITEM_J, t + 3)
        bias_l = item(_ITEM_BIAS, t + 3)
        v = v_ref[0, pl.ds(pl.multiple_of(item(_ITEM_J, t) * blk, blk), blk), :]
        q_rows = pl.ds(pl.multiple_of(qi_l * blk, blk), blk)
        out = None
        for hh in heads:
            a = jnp.exp2(z_scr[slot, hh] - c_scr[hh]).astype(BF16)
            acc = acc_scr[hh] * keep_w + jnp.dot(a, v, preferred_element_type=F32)
            acc_scr[hh] = acc
            out = acc if out is None else jnp.where(lane_head == hh, acc, out)
            c_scr[hh] = jnp.dot(sp_scr[hh], tri, preferred_element_type=F32)
            z2 = zraw_scr[hh]
            sp2 = _softplus2(z2)
            run2 = run_scr[hh] * keep_s
            z_scr[slot, hh] = z2 - run2
            sp_scr[hh] = sp2.astype(BF16)
            run_scr[hh] = run2 + jnp.sum(sp2, axis=-1, keepdims=True)
            zraw_scr[hh] = (jnp.dot(q_refs[hh][0, q_rows, :], kt_ref[0, j_l], preferred_element_type=F32)
                            + bsel_scr[hh, bias_l])
        o_ref[0, pl.ds(pl.multiple_of(qi_w * blk, blk), blk), :] = out.astype(o_ref.dtype)
        return 0

    lax.fori_loop(0, n_items - (_PIPE_DEPTH - 1), trip, 0)


def _sb_prompt_call(q_slots, kt_bf, v_bf, bias, tri):
    b, t, w = v_bf.shape
    blk = SB_BLOCK
    nh = HEADS_PER_LANE_TILE
    items, n_items = _sb_items(t // blk)
    seq_spec = pl.BlockSpec((1, t, LANES), lambda bi, hp, it: (bi, 0, hp))
    grid_spec = pltpu.PrefetchScalarGridSpec(
        num_scalar_prefetch=1,
        grid=(b, w // LANES),
        in_specs=([pl.BlockSpec(memory_space=pltpu.SMEM)] + [seq_spec] * nh
                  + [pl.BlockSpec((1, t // blk, LANES, blk), lambda bi, hp, it: (bi, 0, hp, 0)),
                     seq_spec,
                     pl.BlockSpec(tri.shape, lambda bi, hp, it: (0, 0))]),
        out_specs=seq_spec,
        scratch_shapes=[pltpu.VMEM((nh, 3, blk, blk), F32), pltpu.VMEM((nh, blk, blk), F32),
                        pltpu.VMEM((2, nh, blk, blk), F32), pltpu.VMEM((nh, blk, blk), BF16),
                        pltpu.VMEM((nh, blk, blk), F32), pltpu.VMEM((nh, blk, LANES), F32),
                        pltpu.VMEM((nh, blk, 1), F32)],
    )
    return pl.pallas_call(
        functools.partial(_sb_prompt_kernel, n_items=n_items),
        grid_spec=grid_spec,
        out_shape=jax.ShapeDtypeStruct((b, t, w), BF16),
        compiler_params=_params(2),
        name="sb_attention_prompt",
    )(items, bias, *q_slots, kt_bf, v_bf, tri)


def _sb_decode_kernel(pt_ref, q_ref, kn_ref, vn_ref, bias_ref, tri_ref, *rest, pages):
    del pt_ref
    k_refs = rest[:pages]
    v_refs = rest[pages:2 * pages]
    o_ref = rest[2 * pages]
    qbd_ref, run_ref, acc_ref = rest[2 * pages + 1:]
    g = pl.program_id(1)
    dec_t = q_ref.shape[1]
    rows = SB_HEADS * dec_t
    blk = tri_ref.shape[1]
    bias = bias_ref[...] * LOG2E
    tri = tri_ref[...]
    lane_head = lax.broadcasted_iota(jnp.int32, (1, SB_WIDTH), 1) // SB_HEAD_DIM

    @pl.when(g == 0)
    def _():
        q = q_ref[0] * LOG2E
        qbd = jnp.concatenate([jnp.where(lane_head == hh, q, 0.0) for hh in range(SB_HEADS)], axis=0)
        qbd_ref[...] = qbd.astype(BF16)
        pad = jnp.zeros((blk - dec_t, SB_WIDTH), F32)
        k_new = jnp.concatenate([kn_ref[0], pad], axis=0).astype(BF16)
        v_new = jnp.concatenate([vn_ref[0], pad], axis=0).astype(BF16)
        t_row = lax.broadcasted_iota(jnp.int32, (rows, blk), 0) % dec_t
        col = lax.broadcasted_iota(jnp.int32, (rows, blk), 1)
        z = lax.dot_general(qbd_ref[...], k_new, _NT, preferred_element_type=F32) + bias
        a, run = _sb_weights(z, tri, jnp.zeros((rows, 1), F32), col < t_row)
        run_ref[...] = jnp.broadcast_to(run, run_ref.shape)
        acc_ref[...] = jnp.dot(a, v_new, preferred_element_type=F32)

    qbd = qbd_ref[...]
    run = run_ref[:, 0:1]
    acc = acc_ref[...]
    pages_per_blk = blk // k_refs[0].shape[2]
    groups = [slice(p * pages_per_blk, (p + 1) * pages_per_blk)
              for p in reversed(range(pages // pages_per_blk))]
    zs = [jnp.dot(qbd, jnp.concatenate([r[0] for r in k_refs[sl]], axis=1).astype(BF16),
                  preferred_element_type=F32) + bias for sl in groups]
    sps = [_softplus2(z) for z in zs]
    halves = []
    for sp2 in sps:
        hi = sp2.astype(BF16)
        halves.append(jnp.concatenate([hi, (sp2 - hi.astype(F32)).astype(BF16)], axis=1))
    csums = [jnp.dot(h, tri, preferred_element_type=F32) for h in halves]
    for z, sp2, csum, sl in zip(zs, sps, csums, groups):
        a = jnp.exp2(z - csum - run).astype(BF16)
        v_t = jnp.concatenate([r[0] for r in v_refs[sl]], axis=1).astype(BF16)
        acc = acc + lax.dot_general(a, v_t, _NT, preferred_element_type=F32)
        run = run + jnp.sum(sp2, axis=-1, keepdims=True)
    run_ref[...] = jnp.broadcast_to(run, run_ref.shape)
    acc_ref[...] = acc

    @pl.when(g == pl.num_programs(1) - 1)
    def _():
        out = jnp.zeros((dec_t, SB_WIDTH), F32)
        for hh in range(SB_HEADS):
            out = jnp.where(lane_head == hh, acc[hh * dec_t:(hh + 1) * dec_t, :], out)
        o_ref[0] = out


def _sb_decode_call(q, k_new, v_new, cache_kt, cache_vt, page_table, bias_rows, tri):
    n_seq, dec_t, w = q.shape
    n_pages = page_table.shape[1]
    pages = DEC_PAGES_PER_STEP
    page = cache_kt.shape[2]
    rows = SB_HEADS * dec_t
    seq_spec = pl.BlockSpec((1, dec_t, w), lambda s, g, pt: (s, 0, 0))

    def page_spec(i):
        return pl.BlockSpec((1, w, page), lambda s, g, pt: (pt[s, n_pages - (g + 1) * pages + i], 0, 0))

    def const(shape):
        nd = len(shape)
        return pl.BlockSpec(shape, lambda s, g, pt: (0,) * nd)

    grid_spec = pltpu.PrefetchScalarGridSpec(
        num_scalar_prefetch=1,
        grid=(n_seq, n_pages // pages),
        in_specs=([seq_spec, seq_spec, seq_spec, const(bias_rows.shape), const(tri.shape)]
                  + [page_spec(i) for i in range(pages)] * 2),
        out_specs=seq_spec,
        scratch_shapes=[pltpu.VMEM((rows, w), BF16), pltpu.VMEM((rows, LANES), F32),
                        pltpu.VMEM((rows, w), F32)],
    )
    return pl.pallas_call(
        functools.partial(_sb_decode_kernel, pages=pages),
        grid_spec=grid_spec,
        out_shape=jax.ShapeDtypeStruct((n_seq, dec_t, w), F32),
        compiler_params=_params(2),
        name="sb_attention_decode",
    )(page_table, q, k_new, v_new, bias_rows, tri, *([cache_kt] * pages), *([cache_vt] * pages))


def _head_norm_gate(o, gn, gate):
    o = o * lax.rsqrt(jnp.mean(o * o, axis=-1, keepdims=True) + NORM_EPS) * gn
    return (_silu(gate) * o).astype(BF16)


def _ret_prompt_kernel(dc_ref, q_ref, k_ref, v_ref, g_ref, gn_ref, din_ref, dq_ref, dk_ref,
                       o_ref, s_out_ref, s_ref):
    hh = pl.program_id(1)
    c = pl.program_id(2)

    @pl.when(c == 0)
    def _():
        s_ref[...] = jnp.zeros_like(s_ref)

    dc = dc_ref[hh]
    din = din_ref[0]
    dq = dq_ref[0]
    dk = dk_ref[0]
    gn = gn_ref[...]
    chunks = [slice(i * RET_CHUNK, (i + 1) * RET_CHUNK) for i in range(q_ref.shape[1] // RET_CHUNK)]
    qs = [q_ref[0, sl, :].astype(BF16) for sl in chunks]
    ks = [k_ref[0, sl, :] for sl in chunks]
    vs = [v_ref[0, sl, :] for sl in chunks]
    inners = [lax.dot_general(q, k.astype(BF16), _NT, preferred_element_type=F32) * din
              for q, k in zip(qs, ks)]
    gains = [jnp.dot((k * dk).T.astype(BF16), v, preferred_element_type=F32) for k, v in zip(ks, vs)]
    intras = [jnp.dot(inner.astype(BF16), v, preferred_element_type=F32) for inner, v in zip(inners, vs)]
    states = [s_ref[...]]
    for gain in gains:
        states.append(states[-1] * dc + gain)
    for sl, q, intra, state in zip(chunks, qs, intras, states):
        o = intra + jnp.dot(q, state.astype(BF16), preferred_element_type=F32) * dq
        o_ref[0, sl, :] = _head_norm_gate(o, gn, g_ref[0, sl, :])
    state = states[-1]
    s_ref[...] = state

    @pl.when(c == pl.num_programs(2) - 1)
    def _():
        s_out_ref[0, 0] = state


def _ret_prompt_call(q, k, v_bf, gate, gn, tables):
    b, t, _ = q.shape
    din, dq, dk, dc = tables
    rows = RET_ROWS_PER_STEP
    qk_spec = pl.BlockSpec((1, rows, RET_QK_DIM), lambda bi, h, c: (bi, c, h))
    v_spec = pl.BlockSpec((1, rows, RET_V_DIM), lambda bi, h, c: (bi, c, h))

    def head(shape):
        return pl.BlockSpec((1,) + shape, lambda bi, h, c: (h, 0, 0))

    return pl.pallas_call(
        _ret_prompt_kernel,
        grid=(b, RET_HEADS, t // rows),
        in_specs=[pl.BlockSpec(memory_space=pltpu.SMEM), qk_spec, qk_spec, v_spec, v_spec,
                  pl.BlockSpec((1, RET_V_DIM), lambda bi, h, c: (0, h)),
                  head((RET_CHUNK, RET_CHUNK)), head((RET_CHUNK, RET_V_DIM)),
                  head((RET_CHUNK, RET_QK_DIM))],
        out_specs=[v_spec,
                   pl.BlockSpec((1, 1, RET_QK_DIM, RET_V_DIM), lambda bi, h, c: (bi, h, 0, 0))],
        out_shape=[jax.ShapeDtypeStruct((b, t, RET_V_WIDTH), BF16),
                   jax.ShapeDtypeStruct((b, RET_HEADS, RET_QK_DIM, RET_V_DIM), F32)],
        scratch_shapes=[pltpu.VMEM((RET_QK_DIM, RET_V_DIM), F32)],
        compiler_params=_params(3),
        name="retention_prompt",
    )(dc, q, k, v_bf, gate, gn, din, dq, dk)


def _ret_decode_kernel(dc_ref, q_ref, k_ref, v_ref, g_ref, gn_ref, s_in_ref, din_ref, dq_ref, dk_ref,
                       o_ref, s_out_ref, *, dec_t):
    hh = pl.program_id(0)
    n_seq = s_in_ref.shape[0]
    rows = n_seq * dec_t
    q = q_ref[...]
    k = k_ref[...]
    v = v_ref[...]
    q_bf = q.astype(BF16)
    inner = lax.dot_general(q_bf, k.astype(BF16), _NT, preferred_element_type=F32) * din_ref[0]
    o = jnp.dot(inner.astype(BF16), v, preferred_element_type=F32)

    s_old = s_in_ref[:, 0].reshape(n_seq * RET_QK_DIM, RET_V_DIM)
    row_seq = lax.broadcasted_iota(jnp.int32, (rows, RET_QK_DIM), 0) // dec_t
    q_bd = jnp.concatenate([jnp.where(row_seq == s, q, 0.0) for s in range(n_seq)], axis=1)
    o = o + jnp.dot(q_bd.astype(BF16), s_old.astype(BF16), preferred_element_type=F32) * dq_ref[0]
    o_ref[...] = _head_norm_gate(o, gn_ref[...], g_ref[...])

    kd_t = (k * dk_ref[0]).T
    col_seq = lax.broadcasted_iota(jnp.int32, (RET_QK_DIM, rows), 1) // dec_t
    k_bd_t = jnp.concatenate([jnp.where(col_seq == s, kd_t, 0.0) for s in range(n_seq)], axis=0)
    s_new = s_old * dc_ref[hh] + jnp.dot(k_bd_t.astype(BF16), v, preferred_element_type=F32)
    s_out_ref[:, 0] = s_new.reshape(n_seq, RET_QK_DIM, RET_V_DIM)


def _ret_decode_call(q, k, v_bf, gate, gn, state, tables, dec_t):
    rows = q.shape[0]
    n_seq = state.shape[0]
    din, dq, dk, dc = tables
    qk_spec = pl.BlockSpec((rows, RET_QK_DIM), lambda h: (0, h))
    v_spec = pl.BlockSpec((rows, RET_V_DIM), lambda h: (0, h))
    s_spec = pl.BlockSpec((n_seq, 1, RET_QK_DIM, RET_V_DIM), lambda h: (0, h, 0, 0))

    def head(shape):
        return pl.BlockSpec((1,) + shape, lambda h: (h, 0, 0))

    return pl.pallas_call(
        functools.partial(_ret_decode_kernel, dec_t=dec_t),
        grid=(RET_HEADS,),
        in_specs=[pl.BlockSpec(memory_space=pltpu.SMEM), qk_spec, qk_spec, v_spec, v_spec,
                  pl.BlockSpec((1, RET_V_DIM), lambda h: (0, h)), s_spec,
                  head((rows, rows)), head((rows, RET_V_DIM)), head((rows, RET_QK_DIM))],
        out_specs=[v_spec, s_spec],
        out_shape=[jax.ShapeDtypeStruct((rows, RET_V_WIDTH), BF16),
                   jax.ShapeDtypeStruct(state.shape, F32)],
        compiler_params=_params(1),
        name="retention_decode",
    )(dc, q, k, v_bf, gate, gn, state, din, dq, dk)


def _post_kernel(h_ref, osb_ref, or_ref, asb_ref, ar_ref, wsb_ref, wret_ref, wo_ref, gmix_ref,
                 gpre_ref, wgu_ref, wdown_ref, gpost_ref, o_ref):
    m = (jax.nn.sigmoid(asb_ref[...]) * jnp.dot(osb_ref[...], wsb_ref[...], preferred_element_type=F32)
         + jax.nn.sigmoid(ar_ref[...]) * jnp.dot(or_ref[...], wret_ref[...], preferred_element_type=F32))
    mix = jnp.dot(m.astype(BF16), wo_ref[...], preferred_element_type=F32)
    h = h_ref[...] + _rms(mix, gmix_ref[...])
    o_ref[...] = _ffn_residual(h, gpre_ref[...], wgu_ref, wdown_ref, gpost_ref[...])


def _post_call(h, o_sb, o_r, a_sb, a_r, w_sb, w_ret, w_o, g_mix, g_pre, wgu, wdown, g_post):
    n, d = h.shape
    consts = [w_sb, w_ret, w_o, g_mix, g_pre, wgu, wdown, g_post]
    return pl.pallas_call(
        _post_kernel,
        grid=(n // ROW_TILE,),
        in_specs=[_row_spec(d), _row_spec(SB_WIDTH), _row_spec(RET_V_WIDTH), _row_spec(d), _row_spec(d)]
                 + [_const_spec(c.shape) for c in consts],
        out_specs=_row_spec(d),
        out_shape=jax.ShapeDtypeStruct((n, d), F32),
        compiler_params=_params(1),
        name="merge_out_ffn",
    )(h, o_sb, o_r, a_sb, a_r, *consts)


def _rope_tables(pos):
    half = RET_QK_DIM // 2
    freq = ROPE_BASE ** (-jnp.arange(half, dtype=F32) / half)
    ang = pos.astype(F32)[:, None] * freq[None, :]
    cos, sin = jnp.cos(ang), jnp.sin(ang)
    return jnp.concatenate([cos, cos], axis=1), jnp.concatenate([-sin, sin], axis=1)


def _decay_tables(chunk, reps):
    log_gamma = jnp.log1p(-jnp.exp2(-5.0 - jnp.arange(RET_HEADS, dtype=F32)))
    idx = jnp.arange(chunk, dtype=F32)
    diff = idx[:, None] - idx[None, :]
    d_in = jnp.where(diff >= 0, jnp.exp(log_gamma[:, None, None] * jnp.maximum(diff, 0.0)), 0.0)
    d_q = jnp.exp(log_gamma[:, None] * (idx + 1.0))
    d_k = jnp.exp(log_gamma[:, None] * (chunk - 1.0 - idx))
    d_c = jnp.exp(log_gamma * chunk)
    if reps > 1:
        seq = jnp.arange(chunk * reps) // chunk
        d_in = jnp.where(seq[:, None] == seq[None, :], jnp.tile(d_in, (1, reps, reps)), 0.0)
        d_q = jnp.tile(d_q, (1, reps))
        d_k = jnp.tile(d_k, (1, reps))
    n = chunk * reps
    d_q = jnp.broadcast_to(d_q[:, :, None], (RET_HEADS, n, RET_V_DIM))
    d_k = jnp.broadcast_to(d_k[:, :, None], (RET_HEADS, n, RET_QK_DIM))
    return d_in, d_q, d_k, d_c


def _tri(n):
    i = jnp.arange(n)
    return (i[:, None] >= i[None, :]).astype(BF16)


def kernel(x_prompt, x_sample, cache_k, cache_v, state_ret, page_table, g_ffn1_pre, w_ffn1_gu, w_ffn1_down, g_ffn1_post, g_mix_pre, w_in, sb_bias, ret_gn_g, w_sb_out, w_ret_out, w_o, g_mix_post, g_ffn2_pre, w_ffn2_gu, w_ffn2_down, g_ffn2_post):
    batch, seq, d = x_prompt.shape
    n_seq, dec_t, _ = x_sample.shape
    depth = w_in.shape[0]
    n_pool = cache_k.shape[1]
    n_pages = page_table.shape[1]
    page = cache_k.shape[2]
    past_len = n_pages * page
    assert HEADS_PER_LANE_TILE == 2
    assert seq % RET_ROWS_PER_STEP == 0 and seq % SB_BLOCK == 0 and seq % ROW_TILE == 0
    assert (n_seq * dec_t) % ROW_TILE == 0 and ROW_TILE % dec_t == 0
    assert n_pages % DEC_PAGES_PER_STEP == 0 and SB_BLOCK % page == 0
    assert (DEC_PAGES_PER_STEP * page) % SB_BLOCK == 0 and dec_t % 8 == 0 and dec_t <= SB_BLOCK
    assert dec_t % RET_CHUNK != 0

    rope_p = _rope_tables(jnp.arange(seq))
    rope_s = _rope_tables(past_len + jnp.arange(ROW_TILE) % dec_t)
    decay_p = _decay_tables(RET_CHUNK, 1)
    decay_s = _decay_tables(dec_t, n_seq)
    tri = _tri(SB_BLOCK)
    tri2 = jnp.concatenate([tri, tri], axis=0)

    hp = x_prompt.reshape(batch * seq, d)
    hs = x_sample.reshape(n_seq * dec_t, d)
    kp_l, vp_l, sp_l, ks_l, vs_l, ss_l = [], [], [], [], [], []
    for l in range(depth):
        g1pre, g1post = g_ffn1_pre[l][None], g_ffn1_post[l][None]
        g2pre, g2post = g_ffn2_pre[l][None], g_ffn2_post[l][None]
        gmpre, gmpost = g_mix_pre[l][None], g_mix_post[l][None]
        gn = ret_gn_g[l][None]
        w1gu, w1down = w_ffn1_gu[l].astype(BF16), w_ffn1_down[l].astype(BF16)
        w2gu, w2down = w_ffn2_gu[l].astype(BF16), w_ffn2_down[l].astype(BF16)
        win = w_in[l].astype(BF16)
        w_kv_t = w_in[l][:, _OFF_KSB:_OFF_QR].T.astype(BF16)
        wsb, wret, wo = w_sb_out[l].astype(BF16), w_ret_out[l].astype(BF16), w_o[l].astype(BF16)
        bias = sb_bias[l].astype(F32)
        bias_rows = jnp.broadcast_to(jnp.repeat(bias, dec_t)[:, None], (SB_HEADS * dec_t, SB_BLOCK))
        ck_t = cache_k[l].transpose(0, 2, 3, 1).reshape(n_pool, SB_WIDTH, page)
        cv_t = cache_v[l].transpose(0, 2, 3, 1).reshape(n_pool, SB_WIDTH, page)

        h1 = _ffn_call(hp, g1pre, w1gu, w1down, g1post)
        (k_t, v_t, kt_bf, v_bf, q_s0, q_s1, q_r, k_r, v_r, g_r, a_sb, a_r) = _inproj_prompt_call(
            h1, gmpre, win, w_kv_t, *rope_p, batch, seq)
        o_sb = _sb_prompt_call([q_s0.reshape(batch, seq, SB_WIDTH), q_s1.reshape(batch, seq, SB_WIDTH)],
                               kt_bf, v_bf.reshape(batch, seq, SB_WIDTH), bias, tri)
        o_r, s_p = _ret_prompt_call(q_r.reshape(batch, seq, -1), k_r.reshape(batch, seq, -1),
                                    v_r.reshape(batch, seq, -1), g_r.reshape(batch, seq, -1),
                                    gn, decay_p)
        hp = _post_call(h1, o_sb.reshape(batch * seq, SB_WIDTH), o_r.reshape(batch * seq, RET_V_WIDTH),
                        a_sb, a_r, wsb, wret, wo, gmpost, g2pre, w2gu, w2down, g2post)
        kp_l.append(k_t.reshape(batch, SB_HEADS, SB_HEAD_DIM, seq).transpose(0, 3, 1, 2))
        vp_l.append(v_t.reshape(batch, SB_HEADS, SB_HEAD_DIM, seq).transpose(0, 3, 1, 2))
        sp_l.append(s_p)

        h1 = _ffn_call(hs, g1pre, w1gu, w1down, g1post)
        (k_sb, v_sb, q_sb, q_r, k_r, v_r, g_r, a_sb, a_r) = _inproj_decode_call(
            h1, gmpre, win, *rope_s)
        o_sb = _sb_decode_call(q_sb.reshape(n_seq, dec_t, SB_WIDTH), k_sb.reshape(n_seq, dec_t, SB_WIDTH),
                               v_sb.reshape(n_seq, dec_t, SB_WIDTH), ck_t, cv_t, page_table, bias_rows, tri2)
        o_r, s_s = _ret_decode_call(q_r, k_r, v_r, g_r, gn, state_ret[l], decay_s, dec_t)
        hs = _post_call(h1, o_sb.reshape(n_seq * dec_t, SB_WIDTH).astype(BF16), o_r, a_sb, a_r,
                        wsb, wret, wo, gmpost, g2pre, w2gu, w2down, g2post)
        ks_l.append(k_sb.reshape(n_seq, dec_t, SB_HEADS, SB_HEAD_DIM))
        vs_l.append(v_sb.reshape(n_seq, dec_t, SB_HEADS, SB_HEAD_DIM))
        ss_l.append(s_s)

    return (hp.reshape(batch, seq, d), hs.reshape(n_seq, dec_t, d),
            jnp.stack(kp_l), jnp.stack(vp_l), jnp.stack(sp_l),
            jnp.stack(ks_l), jnp.stack(vs_l), jnp.stack(ss_l))
```

```python
import functools

import jax
import jax.numpy as jnp
from jax import lax
from jax.experimental import pallas as pl
from jax.experimental.pallas import tpu as pltpu

F32 = jnp.float32
BF16 = jnp.bfloat16

SB_HEADS = 8
SB_HEAD_DIM = 64
SB_WIDTH = SB_HEADS * SB_HEAD_DIM
RET_HEADS = 4
RET_QK_DIM = 128
RET_V_DIM = 256
RET_QK_WIDTH = RET_HEADS * RET_QK_DIM
RET_V_WIDTH = RET_HEADS * RET_V_DIM
RET_CHUNK = 128
ROPE_BASE = 10000.0
NORM_EPS = 1e-6
LOG2E = 1.4426950408889634

LANES = 128
HEADS_PER_LANE_TILE = LANES // SB_HEAD_DIM

ROW_TILE = 256
SB_BLOCK = 256
SB_HEADS_PER_STEP = 4
DEC_PAGES_PER_STEP = 16
RET_ROWS_PER_STEP = 1024
VMEM_LIMIT = 56 * 1024 * 1024

_NT = (((1,), (1,)), ((), ()))


def _const_spec(shape):
    nd = len(shape)
    return pl.BlockSpec(shape, lambda *_: (0,) * nd, pipeline_mode=pl.Buffered(1))


def _params(n_axes, vmem=VMEM_LIMIT):
    return pltpu.CompilerParams(dimension_semantics=("arbitrary",) * n_axes,
                                vmem_limit_bytes=vmem)


def _rms(x, g):
    ms = jnp.mean(x * x, axis=-1, keepdims=True)
    return x * lax.rsqrt(ms + NORM_EPS) * g


def _silu(x):
    return x * jax.nn.sigmoid(x)


def _ffn_residual(x, g_pre, wgu_ref, wdown_ref, g_post):
    d_ff = wdown_ref.shape[0]
    xn = _rms(x, g_pre).astype(BF16)
    gate = jnp.dot(xn, wgu_ref[:, :d_ff], preferred_element_type=F32)
    up = jnp.dot(xn, wgu_ref[:, d_ff:], preferred_element_type=F32)
    act = (_silu(gate) * up).astype(BF16)
    y = jnp.dot(act, wdown_ref[...], preferred_element_type=F32)
    return x + 0.5 * _rms(y, g_post)


def _ffn_kernel(x_ref, gpre_ref, wgu_ref, wdown_ref, gpost_ref, o_ref):
    o_ref[...] = _ffn_residual(x_ref[...], gpre_ref[...], wgu_ref, wdown_ref, gpost_ref[...])


def _ffn_call(x, g_pre, wgu, wdown, g_post):
    n, d = x.shape
    row = pl.BlockSpec((ROW_TILE, d), lambda i: (i, 0))
    return pl.pallas_call(
        _ffn_kernel,
        grid=(n // ROW_TILE,),
        in_specs=[row, _const_spec(g_pre.shape), _const_spec(wgu.shape),
                  _const_spec(wdown.shape), _const_spec(g_post.shape)],
        out_specs=row,
        out_shape=jax.ShapeDtypeStruct((n, d), F32),
        compiler_params=_params(1),
        name="ffn_block",
    )(x, g_pre, wgu, wdown, g_post)


_OFF_QSB = 0
_OFF_KSB = _OFF_QSB + SB_WIDTH
_OFF_VSB = _OFF_KSB + SB_WIDTH
_OFF_QR = _OFF_VSB + SB_WIDTH
_OFF_KR = _OFF_QR + RET_QK_WIDTH
_OFF_VR = _OFF_KR + RET_QK_WIDTH
_OFF_GR = _OFF_VR + RET_V_WIDTH
_OFF_ASB = _OFF_GR + RET_V_WIDTH


def _inproj_shared(u, win_ref, cos_ref, sin_ref, qr_ref, kr_ref, vr_ref, gr_ref, asb_ref, ar_ref):
    def proj(lo, width):
        return jnp.dot(u, win_ref[:, lo:lo + width], preferred_element_type=F32)

    cos = cos_ref[...]
    sin = sin_ref[...]
    q_r = proj(_OFF_QR, RET_QK_WIDTH)
    k_r = proj(_OFF_KR, RET_QK_WIDTH)
    for hh in range(RET_HEADS):
        sl = slice(hh * RET_QK_DIM, (hh + 1) * RET_QK_DIM)
        qh = q_r[:, sl]
        kh = k_r[:, sl]
        qr_ref[:, sl] = qh * cos + pltpu.roll(qh, RET_QK_DIM // 2, axis=1) * sin
        kr_ref[:, sl] = (kh * cos + pltpu.roll(kh, RET_QK_DIM // 2, axis=1) * sin) * (RET_QK_DIM ** -0.5)
    vr_ref[...] = proj(_OFF_VR, RET_V_WIDTH).astype(BF16)
    gr_ref[...] = proj(_OFF_GR, RET_V_WIDTH)
    d_model = asb_ref.shape[1]
    asb_ref[...] = proj(_OFF_ASB, d_model)
    ar_ref[...] = proj(_OFF_ASB + d_model, d_model)


def _inproj_prompt_kernel(h_ref, g_ref, win_ref, wkvt_ref, cos_ref, sin_ref,
                          kt_ref, vt_ref, ktb_ref, vb_ref, *rest):
    q_refs, shared_refs = rest[:HEADS_PER_LANE_TILE], rest[HEADS_PER_LANE_TILE:]
    u = _rms(h_ref[...], g_ref[...]).astype(BF16)
    q = jnp.dot(u, win_ref[:, _OFF_QSB:_OFF_QSB + SB_WIDTH], preferred_element_type=F32)
    q = q * (SB_HEAD_DIM ** -0.5 * LOG2E)
    head_slot = (lax.broadcasted_iota(jnp.int32, (1, SB_WIDTH), 1) // SB_HEAD_DIM) % HEADS_PER_LANE_TILE
    for hh, q_ref in enumerate(q_refs):
        q_ref[...] = jnp.where(head_slot == hh, q, 0.0).astype(BF16)
    kv_t = lax.dot_general(wkvt_ref[...], u, _NT, preferred_element_type=F32)
    kt_ref[0] = kv_t[:SB_WIDTH]
    vt_ref[0] = kv_t[SB_WIDTH:]
    ktb_ref[0, 0] = kv_t[:SB_WIDTH].astype(BF16)
    vb_ref[...] = jnp.dot(u, win_ref[:, _OFF_VSB:_OFF_VSB + SB_WIDTH],
                          preferred_element_type=F32).astype(BF16)
    _inproj_shared(u, win_ref, cos_ref, sin_ref, *shared_refs)


def _inproj_decode_kernel(h_ref, g_ref, win_ref, cos_ref, sin_ref, k_ref, v_ref, q_ref, *shared_refs):
    u = _rms(h_ref[...], g_ref[...]).astype(BF16)
    q_ref[...] = jnp.dot(u, win_ref[:, _OFF_QSB:_OFF_QSB + SB_WIDTH],
                         preferred_element_type=F32) * (SB_HEAD_DIM ** -0.5)
    k_ref[...] = jnp.dot(u, win_ref[:, _OFF_KSB:_OFF_KSB + SB_WIDTH], preferred_element_type=F32)
    v_ref[...] = jnp.dot(u, win_ref[:, _OFF_VSB:_OFF_VSB + SB_WIDTH], preferred_element_type=F32)
    _inproj_shared(u, win_ref, cos_ref, sin_ref, *shared_refs)


def _row_spec(width):
    return pl.BlockSpec((ROW_TILE, width), lambda i: (i, 0))


def _shared_outputs(n, d):
    widths_dtypes = [(RET_QK_WIDTH, F32), (RET_QK_WIDTH, F32), (RET_V_WIDTH, BF16),
                     (RET_V_WIDTH, F32), (d, F32), (d, F32)]
    return ([_row_spec(w) for w, _ in widths_dtypes],
            [jax.ShapeDtypeStruct((n, w), dt) for w, dt in widths_dtypes])


def _inproj_prompt_call(h, g_pre, w_in, w_kv_t, cos_tab, sin_tab, batch, seq):
    n, d = h.shape
    assert ROW_TILE == SB_BLOCK
    tiles = seq // ROW_TILE
    tab = pl.BlockSpec((ROW_TILE, RET_QK_DIM), lambda i: (i % tiles, 0))
    t_spec = pl.BlockSpec((1, SB_WIDTH, ROW_TILE), lambda i: (i // tiles, 0, i % tiles))
    shared_specs, shared_shapes = _shared_outputs(n, d)
    return pl.pallas_call(
        _inproj_prompt_kernel,
        grid=(n // ROW_TILE,),
        in_specs=[_row_spec(d), _const_spec(g_pre.shape), _const_spec(w_in.shape),
                  _const_spec(w_kv_t.shape), tab, tab],
        out_specs=[t_spec, t_spec,
                   pl.BlockSpec((1, 1, SB_WIDTH, SB_BLOCK), lambda i: (i // tiles, i % tiles, 0, 0)),
                   _row_spec(SB_WIDTH)] + [_row_spec(SB_WIDTH)] * HEADS_PER_LANE_TILE + shared_specs,
        out_shape=[jax.ShapeDtypeStruct((batch, SB_WIDTH, seq), F32),
                   jax.ShapeDtypeStruct((batch, SB_WIDTH, seq), F32),
                   jax.ShapeDtypeStruct((batch, tiles, SB_WIDTH, SB_BLOCK), BF16),
                   jax.ShapeDtypeStruct((n, SB_WIDTH), BF16)]
                  + [jax.ShapeDtypeStruct((n, SB_WIDTH), BF16)] * HEADS_PER_LANE_TILE + shared_shapes,
        compiler_params=_params(1),
        name="in_projection_prompt",
    )(h, g_pre, w_in, w_kv_t, cos_tab, sin_tab)


def _inproj_decode_call(h, g_pre, w_in, cos_tab, sin_tab):
    n, d = h.shape
    tab = pl.BlockSpec((ROW_TILE, RET_QK_DIM), lambda i: (0, 0))
    shared_specs, shared_shapes = _shared_outputs(n, d)
    return pl.pallas_call(
        _inproj_decode_kernel,
        grid=(n // ROW_TILE,),
        in_specs=[_row_spec(d), _const_spec(g_pre.shape), _const_spec(w_in.shape), tab, tab],
        out_specs=[_row_spec(SB_WIDTH)] * 3 + shared_specs,
        out_shape=[jax.ShapeDtypeStruct((n, SB_WIDTH), F32)] * 3 + shared_shapes,
        compiler_params=_params(1),
        name="in_projection_decode",
    )(h, g_pre, w_in, cos_tab, sin_tab)


NULL_LOGIT = -1e30


def _softplus2(z2):
    neg_abs = pltpu.bitcast(pltpu.bitcast(z2, jnp.uint32) | jnp.uint32(0x80000000), F32)
    return jnp.maximum(z2, 0.0) + jnp.log(1.0 + jnp.exp2(neg_abs)) * LOG2E


def _sb_weights(z2, tri2, run2, mask):
    if mask is not None:
        z2 = jnp.where(mask, z2, NULL_LOGIT)
    sp2 = _softplus2(z2)
    hi = sp2.astype(BF16)
    lo = (sp2 - hi.astype(F32)).astype(BF16)
    csum = jnp.dot(jnp.concatenate([hi, lo], axis=1), tri2, preferred_element_type=F32) + run2
    return jnp.exp2(z2 - csum).astype(BF16), run2 + jnp.sum(sp2, axis=-1, keepdims=True)


_ITEM_QI, _ITEM_J, _ITEM_BIAS, _ITEM_FIRST, _ITEM_FIELDS = 0, 1, 2, 3, 4
_BIAS_FULL, _BIAS_DIAG, _BIAS_NULL = 0, 1, 2
_PIPE_DEPTH = 4


def _sb_items(n_blocks):
    pad = _PIPE_DEPTH - 1
    null = (0, 0, _BIAS_NULL, 1)
    items = [null] * pad
    for qi in range(n_blocks):
        for j in range(qi, -1, -1):
            items.append((qi, j, _BIAS_DIAG if j == qi else _BIAS_FULL, int(j == qi)))
    items += [null] * pad
    return jnp.asarray(items, jnp.int32).T.reshape(-1), len(items)


def _sb_prompt_kernel(items_ref, bias_ref, *refs, n_items):
    slots = HEADS_PER_LANE_TILE
    q_refs, (kt_ref, v_ref, tri_ref, o_ref) = refs[:slots], refs[slots:slots + 4]
    bsel_scr, zraw_scr, z_scr, sp_scr, c_scr, acc_scr, run_scr = refs[slots + 4:]
    nh = acc_scr.shape[0]
    hg = pl.program_id(1)
    blk = tri_ref.shape[0]
    tri = tri_ref[...]
    lane_head = lax.broadcasted_iota(jnp.int32, (1, LANES), 1) // SB_HEAD_DIM
    row = lax.broadcasted_iota(jnp.int32, (blk, blk), 0)
    col = lax.broadcasted_iota(jnp.int32, (blk, blk), 1)
    heads = range(nh)

    def item(field, i):
        return items_ref[field * n_items + i]

    for hh in heads:
        bias2 = bias_ref[hg * nh + hh] * LOG2E
        bsel_scr[hh, _BIAS_FULL] = jnp.full((blk, blk), bias2, F32)
        bsel_scr[hh, _BIAS_DIAG] = jnp.where(col < row, bias2, NULL_LOGIT)
        bsel_scr[hh, _BIAS_NULL] = jnp.full((blk, blk), NULL_LOGIT, F32)
        zraw_scr[hh] = jnp.full((blk, blk), NULL_LOGIT, F32)
        z_scr[0, hh] = jnp.full((blk, blk), NULL_LOGIT, F32)
        z_scr[1, hh] = jnp.full((blk, blk), NULL_LOGIT, F32)
        sp_scr[hh] = jnp.zeros((blk, blk), BF16)
        c_scr[hh] = jnp.zeros((blk, blk), F32)
        acc_scr[hh] = jnp.zeros((blk, LANES), F32)
        run_scr[hh] = jnp.zeros((blk, 1), F32)

    def trip(t, _):
        slot = t % 2
        keep_w = 1.0 - item(_ITEM_FIRST, t).astype(F32)
        keep_s = 1.0 - item(_ITEM_FIRST, t + 2).astype(F32)
        j_l = item(_ITEM_J, t + 3)
        bias_l = item(_ITEM_BIAS, t + 3)
        v_rows = pl.ds(pl.multiple_of(item(_ITEM_J, t) * blk, blk), blk)
        q_rows = pl.ds(pl.multiple_of(item(_ITEM_QI, t + 3) * blk, blk), blk)
        o_rows = pl.ds(pl.multiple_of(item(_ITEM_QI, t) * blk, blk), blk)
        for lt in range(nh // slots):
            hs = slice(lt * slots, (lt + 1) * slots)
            lanes = slice(lt * LANES, (lt + 1) * LANES)
            a = jnp.exp2(z_scr[slot, hs] - c_scr[hs]).astype(BF16).reshape(slots * blk, blk)
            av = jnp.dot(a, v_ref[0, v_rows, lanes], preferred_element_type=F32).reshape(slots, blk, LANES)
            acc = acc_scr[hs] * keep_w + av
            acc_scr[hs] = acc
            out = acc[0]
            for s in range(1, slots):
                out = jnp.where(lane_head == s, acc[s], out)
            o_ref[0, o_rows, lanes] = out.astype(o_ref.dtype)
            c_scr[hs] = jnp.dot(sp_scr[hs].reshape(slots * blk, blk), tri,
                                preferred_element_type=F32).reshape(slots, blk, blk)
            z2 = zraw_scr[hs]
            sp2 = _softplus2(z2)
            run2 = run_scr[hs] * keep_s
            z_scr[slot, hs] = z2 - run2
            sp_scr[hs] = sp2.astype(BF16)
            run_scr[hs] = run2 + jnp.sum(sp2, axis=-1, keepdims=True)
            q = jnp.concatenate([q_refs[s][0, q_rows, lanes] for s in range(slots)], axis=0)
            z_new = jnp.dot(q, kt_ref[0, j_l, lanes, :], preferred_element_type=F32)
            zraw_scr[hs] = z_new.reshape(slots, blk, blk) + bsel_scr[hs, bias_l]
        return 0

    lax.fori_loop(0, n_items - (_PIPE_DEPTH - 1), trip, 0)


def _sb_prompt_call(q_slots, kt_bf, v_bf, bias, tri):
    b, t, w = v_bf.shape
    blk = SB_BLOCK
    nh = SB_HEADS_PER_STEP
    gw = nh * SB_HEAD_DIM
    items, n_items = _sb_items(t // blk)
    seq_spec = pl.BlockSpec((1, t, gw), lambda bi, hg, it: (bi, 0, hg))
    grid_spec = pltpu.PrefetchScalarGridSpec(
        num_scalar_prefetch=1,
        grid=(b, w // gw),
        in_specs=([pl.BlockSpec(memory_space=pltpu.SMEM)] + [seq_spec] * HEADS_PER_LANE_TILE
                  + [pl.BlockSpec((1, t // blk, gw, blk), lambda bi, hg, it: (bi, 0, hg, 0)),
                     seq_spec,
                     pl.BlockSpec(tri.shape, lambda bi, hg, it: (0, 0))]),
        out_specs=seq_spec,
        scratch_shapes=[pltpu.VMEM((nh, 3, blk, blk), F32), pltpu.VMEM((nh, blk, blk), F32),
                        pltpu.VMEM((2, nh, blk, blk), F32), pltpu.VMEM((nh, blk, blk), BF16),
                        pltpu.VMEM((nh, blk, blk), F32), pltpu.VMEM((nh, blk, LANES), F32),
                        pltpu.VMEM((nh, blk, 1), F32)],
    )
    return pl.pallas_call(
        functools.partial(_sb_prompt_kernel, n_items=n_items),
        grid_spec=grid_spec,
        out_shape=jax.ShapeDtypeStruct((b, t, w), BF16),
        compiler_params=_params(2),
        name="sb_attention_prompt",
    )(items, bias, *q_slots, kt_bf, v_bf, tri)


def _sb_decode_kernel(pt_ref, q_ref, kn_ref, vn_ref, bias_ref, tri_ref, *rest, pages):
    del pt_ref
    k_refs = rest[:pages]
    v_refs = rest[pages:2 * pages]
    o_ref = rest[2 * pages]
    qbd_ref, run_ref, acc_ref = rest[2 * pages + 1:]
    g = pl.program_id(1)
    dec_t = q_ref.shape[1]
    rows = SB_HEADS * dec_t
    blk = tri_ref.shape[1]
    bias = bias_ref[...] * LOG2E
    tri = tri_ref[...]
    lane_head = lax.broadcasted_iota(jnp.int32, (1, SB_WIDTH), 1) // SB_HEAD_DIM

    @pl.when(g == 0)
    def _():
        q = q_ref[0] * LOG2E
        qbd = jnp.concatenate([jnp.where(lane_head == hh, q, 0.0) for hh in range(SB_HEADS)], axis=0)
        qbd_ref[...] = qbd.astype(BF16)
        pad = jnp.zeros((blk - dec_t, SB_WIDTH), F32)
        k_new = jnp.concatenate([kn_ref[0], pad], axis=0).astype(BF16)
        v_new = jnp.concatenate([vn_ref[0], pad], axis=0).astype(BF16)
        t_row = lax.broadcasted_iota(jnp.int32, (rows, blk), 0) % dec_t
        col = lax.broadcasted_iota(jnp.int32, (rows, blk), 1)
        z = lax.dot_general(qbd_ref[...], k_new, _NT, preferred_element_type=F32) + bias
        a, run = _sb_weights(z, tri, jnp.zeros((rows, 1), F32), col < t_row)
        run_ref[...] = jnp.broadcast_to(run, run_ref.shape)
        acc_ref[...] = jnp.dot(a, v_new, preferred_element_type=F32)

    qbd = qbd_ref[...]
    run = run_ref[:, 0:1]
    acc = acc_ref[...]
    pages_per_blk = blk // k_refs[0].shape[2]
    groups = [slice(p * pages_per_blk, (p + 1) * pages_per_blk)
              for p in reversed(range(pages // pages_per_blk))]
    zs = [jnp.dot(qbd, jnp.concatenate([r[0] for r in k_refs[sl]], axis=1).astype(BF16),
                  preferred_element_type=F32) + bias for sl in groups]
    sps = [_softplus2(z) for z in zs]
    halves = []
    for sp2 in sps:
        hi = sp2.astype(BF16)
        halves.append(jnp.concatenate([hi, (sp2 - hi.astype(F32)).astype(BF16)], axis=1))
    csums = [jnp.dot(h, tri, preferred_element_type=F32) for h in halves]
    for z, sp2, csum, sl in zip(zs, sps, csums, groups):
        a = jnp.exp2(z - csum - run).astype(BF16)
        v_t = jnp.concatenate([r[0] for r in v_refs[sl]], axis=1).astype(BF16)
        acc = acc + lax.dot_general(a, v_t, _NT, preferred_element_type=F32)
        run = run + jnp.sum(sp2, axis=-1, keepdims=True)
    run_ref[...] = jnp.broadcast_to(run, run_ref.shape)
    acc_ref[...] = acc

    @pl.when(g == pl.num_programs(1) - 1)
    def _():
        out = jnp.zeros((dec_t, SB_WIDTH), F32)
        for hh in range(SB_HEADS):
            out = jnp.where(lane_head == hh, acc[hh * dec_t:(hh + 1) * dec_t, :], out)
        o_ref[0] = out


def _sb_decode_call(q, k_new, v_new, cache_kt, cache_vt, page_table, bias_rows, tri):
    n_seq, dec_t, w = q.shape
    n_pages = page_table.shape[1]
    pages = DEC_PAGES_PER_STEP
    page = cache_kt.shape[2]
    rows = SB_HEADS * dec_t
    seq_spec = pl.BlockSpec((1, dec_t, w), lambda s, g, pt: (s, 0, 0))

    def page_spec(i):
        return pl.BlockSpec((1, w, page), lambda s, g, pt: (pt[s, n_pages - (g + 1) * pages + i], 0, 0))

    def const(shape):
        nd = len(shape)
        return pl.BlockSpec(shape, lambda s, g, pt: (0,) * nd)

    grid_spec = pltpu.PrefetchScalarGridSpec(
        num_scalar_prefetch=1,
        grid=(n_seq, n_pages // pages),
        in_specs=([seq_spec, seq_spec, seq_spec, const(bias_rows.shape), const(tri.shape)]
                  + [page_spec(i) for i in range(pages)] * 2),
        out_specs=seq_spec,
        scratch_shapes=[pltpu.VMEM((rows, w), BF16), pltpu.VMEM((rows, LANES), F32),
                        pltpu.VMEM((rows, w), F32)],
    )
    return pl.pallas_call(
        functools.partial(_sb_decode_kernel, pages=pages),
        grid_spec=grid_spec,
        out_shape=jax.ShapeDtypeStruct((n_seq, dec_t, w), F32),
        compiler_params=_params(2),
        name="sb_attention_decode",
    )(page_table, q, k_new, v_new, bias_rows, tri, *([cache_kt] * pages), *([cache_vt] * pages))


def _head_norm_gate(o, gn, gate):
    o = o * lax.rsqrt(jnp.mean(o * o, axis=-1, keepdims=True) + NORM_EPS) * gn
    return (_silu(gate) * o).astype(BF16)


def _ret_prompt_kernel(dc_ref, q_ref, k_ref, v_ref, g_ref, gn_ref, din_ref, dq_ref, dk_ref,
                       o_ref, s_out_ref, s_ref):
    hh = pl.program_id(1)
    c = pl.program_id(2)

    @pl.when(c == 0)
    def _():
        s_ref[...] = jnp.zeros_like(s_ref)

    dc = dc_ref[hh]
    din = din_ref[0]
    dq = dq_ref[0]
    dk = dk_ref[0]
    gn = gn_ref[...]
    chunks = [slice(i * RET_CHUNK, (i + 1) * RET_CHUNK) for i in range(q_ref.shape[1] // RET_CHUNK)]
    qs = [q_ref[0, sl, :].astype(BF16) for sl in chunks]
    ks = [k_ref[0, sl, :] for sl in chunks]
    vs = [v_ref[0, sl, :] for sl in chunks]
    inners = [lax.dot_general(q, k.astype(BF16), _NT, preferred_element_type=F32) * din
              for q, k in zip(qs, ks)]
    gains = [jnp.dot((k * dk).T.astype(BF16), v, preferred_element_type=F32) for k, v in zip(ks, vs)]
    intras = [jnp.dot(inner.astype(BF16), v, preferred_element_type=F32) for inner, v in zip(inners, vs)]
    states = [s_ref[...]]
    for gain in gains:
        states.append(states[-1] * dc + gain)
    for sl, q, intra, state in zip(chunks, qs, intras, states):
        o = intra + jnp.dot(q, state.astype(BF16), preferred_element_type=F32) * dq
        o_ref[0, sl, :] = _head_norm_gate(o, gn, g_ref[0, sl, :])
    state = states[-1]
    s_ref[...] = state

    @pl.when(c == pl.num_programs(2) - 1)
    def _():
        s_out_ref[0, 0] = state


def _ret_prompt_call(q, k, v_bf, gate, gn, tables):
    b, t, _ = q.shape
    din, dq, dk, dc = tables
    rows = RET_ROWS_PER_STEP
    qk_spec = pl.BlockSpec((1, rows, RET_QK_DIM), lambda bi, h, c: (bi, c, h))
    v_spec = pl.BlockSpec((1, rows, RET_V_DIM), lambda bi, h, c: (bi, c, h))

    def head(shape):
        return pl.BlockSpec((1,) + shape, lambda bi, h, c: (h, 0, 0))

    return pl.pallas_call(
        _ret_prompt_kernel,
        grid=(b, RET_HEADS, t // rows),
        in_specs=[pl.BlockSpec(memory_space=pltpu.SMEM), qk_spec, qk_spec, v_spec, v_spec,
                  pl.BlockSpec((1, RET_V_DIM), lambda bi, h, c: (0, h)),
                  head((RET_CHUNK, RET_CHUNK)), head((RET_CHUNK, RET_V_DIM)),
                  head((RET_CHUNK, RET_QK_DIM))],
        out_specs=[v_spec,
                   pl.BlockSpec((1, 1, RET_QK_DIM, RET_V_DIM), lambda bi, h, c: (bi, h, 0, 0))],
        out_shape=[jax.ShapeDtypeStruct((b, t, RET_V_WIDTH), BF16),
                   jax.ShapeDtypeStruct((b, RET_HEADS, RET_QK_DIM, RET_V_DIM), F32)],
        scratch_shapes=[pltpu.VMEM((RET_QK_DIM, RET_V_DIM), F32)],
        compiler_params=_params(3),
        name="retention_prompt",
    )(dc, q, k, v_bf, gate, gn, din, dq, dk)


def _ret_decode_kernel(dc_ref, q_ref, k_ref, v_ref, g_ref, gn_ref, s_in_ref, din_ref, dq_ref, dk_ref,
                       o_ref, s_out_ref, *, dec_t):
    hh = pl.program_id(0)
    n_seq = s_in_ref.shape[0]
    rows = n_seq * dec_t
    q = q_ref[...]
    k = k_ref[...]
    v = v_ref[...]
    q_bf = q.astype(BF16)
    inner = lax.dot_general(q_bf, k.astype(BF16), _NT, preferred_element_type=F32) * din_ref[0]
    o = jnp.dot(inner.astype(BF16), v, preferred_element_type=F32)

    s_old = s_in_ref[:, 0].reshape(n_seq * RET_QK_DIM, RET_V_DIM)
    row_seq = lax.broadcasted_iota(jnp.int32, (rows, RET_QK_DIM), 0) // dec_t
    q_bd = jnp.concatenate([jnp.where(row_seq == s, q, 0.0) for s in range(n_seq)], axis=1)
    o = o + jnp.dot(q_bd.astype(BF16), s_old.astype(BF16), preferred_element_type=F32) * dq_ref[0]
    o_ref[...] = _head_norm_gate(o, gn_ref[...], g_ref[...])

    kd_t = (k * dk_ref[0]).T
    col_seq = lax.broadcasted_iota(jnp.int32, (RET_QK_DIM, rows), 1) // dec_t
    k_bd_t = jnp.concatenate([jnp.where(col_seq == s, kd_t, 0.0) for s in range(n_seq)], axis=0)
    s_new = s_old * dc_ref[hh] + jnp.dot(k_bd_t.astype(BF16), v, preferred_element_type=F32)
    s_out_ref[:, 0] = s_new.reshape(n_seq, RET_QK_DIM, RET_V_DIM)


def _ret_decode_call(q, k, v_bf, gate, gn, state, tables, dec_t):
    rows = q.shape[0]
    n_seq = state.shape[0]
    din, dq, dk, dc = tables
    qk_spec = pl.BlockSpec((rows, RET_QK_DIM), lambda h: (0, h))
    v_spec = pl.BlockSpec((rows, RET_V_DIM), lambda h: (0, h))
    s_spec = pl.BlockSpec((n_seq, 1, RET_QK_DIM, RET_V_DIM), lambda h: (0, h, 0, 0))

    def head(shape):
        return pl.BlockSpec((1,) + shape, lambda h: (h, 0, 0))

    return pl.pallas_call(
        functools.partial(_ret_decode_kernel, dec_t=dec_t),
        grid=(RET_HEADS,),
        in_specs=[pl.BlockSpec(memory_space=pltpu.SMEM), qk_spec, qk_spec, v_spec, v_spec,
                  pl.BlockSpec((1, RET_V_DIM), lambda h: (0, h)), s_spec,
                  head((rows, rows)), head((rows, RET_V_DIM)), head((rows, RET_QK_DIM))],
        out_specs=[v_spec, s_spec],
        out_shape=[jax.ShapeDtypeStruct((rows, RET_V_WIDTH), BF16),
                   jax.ShapeDtypeStruct(state.shape, F32)],
        compiler_params=_params(1),
        name="retention_decode",
    )(dc, q, k, v_bf, gate, gn, state, din, dq, dk)


def _post_kernel(h_ref, osb_ref, or_ref, asb_ref, ar_ref, wsb_ref, wret_ref, wo_ref, gmix_ref,
                 gpre_ref, wgu_ref, wdown_ref, gpost_ref, o_ref):
    m = (jax.nn.sigmoid(asb_ref[...]) * jnp.dot(osb_ref[...], wsb_ref[...], preferred_element_type=F32)
         + jax.nn.sigmoid(ar_ref[...]) * jnp.dot(or_ref[...], wret_ref[...], preferred_element_type=F32))
    mix = jnp.dot(m.astype(BF16), wo_ref[...], preferred_element_type=F32)
    h = h_ref[...] + _rms(mix, gmix_ref[...])
    o_ref[...] = _ffn_residual(h, gpre_ref[...], wgu_ref, wdown_ref, gpost_ref[...])


def _post_call(h, o_sb, o_r, a_sb, a_r, w_sb, w_ret, w_o, g_mix, g_pre, wgu, wdown, g_post):
    n, d = h.shape
    consts = [w_sb, w_ret, w_o, g_mix, g_pre, wgu, wdown, g_post]
    return pl.pallas_call(
        _post_kernel,
        grid=(n // ROW_TILE,),
        in_specs=[_row_spec(d), _row_spec(SB_WIDTH), _row_spec(RET_V_WIDTH), _row_spec(d), _row_spec(d)]
                 + [_const_spec(c.shape) for c in consts],
        out_specs=_row_spec(d),
        out_shape=jax.ShapeDtypeStruct((n, d), F32),
        compiler_params=_params(1),
        name="merge_out_ffn",
    )(h, o_sb, o_r, a_sb, a_r, *consts)


def _rope_tables(pos):
    half = RET_QK_DIM // 2
    freq = ROPE_BASE ** (-jnp.arange(half, dtype=F32) / half)
    ang = pos.astype(F32)[:, None] * freq[None, :]
    cos, sin = jnp.cos(ang), jnp.sin(ang)
    return jnp.concatenate([cos, cos], axis=1), jnp.concatenate([-sin, sin], axis=1)


def _decay_tables(chunk, reps):
    log_gamma = jnp.log1p(-jnp.exp2(-5.0 - jnp.arange(RET_HEADS, dtype=F32)))
    idx = jnp.arange(chunk, dtype=F32)
    diff = idx[:, None] - idx[None, :]
    d_in = jnp.where(diff >= 0, jnp.exp(log_gamma[:, None, None] * jnp.maximum(diff, 0.0)), 0.0)
    d_q = jnp.exp(log_gamma[:, None] * (idx + 1.0))
    d_k = jnp.exp(log_gamma[:, None] * (chunk - 1.0 - idx))
    d_c = jnp.exp(log_gamma * chunk)
    if reps > 1:
        seq = jnp.arange(chunk * reps) // chunk
        d_in = jnp.where(seq[:, None] == seq[None, :], jnp.tile(d_in, (1, reps, reps)), 0.0)
        d_q = jnp.tile(d_q, (1, reps))
        d_k = jnp.tile(d_k, (1, reps))
    n = chunk * reps
    d_q = jnp.broadcast_to(d_q[:, :, None], (RET_HEADS, n, RET_V_DIM))
    d_k = jnp.broadcast_to(d_k[:, :, None], (RET_HEADS, n, RET_QK_DIM))
    return d_in, d_q, d_k, d_c


def _tri(n):
    i = jnp.arange(n)
    return (i[:, None] >= i[None, :]).astype(BF16)


def kernel(x_prompt, x_sample, cache_k, cache_v, state_ret, page_table, g_ffn1_pre, w_ffn1_gu, w_ffn1_down, g_ffn1_post, g_mix_pre, w_in, sb_bias, ret_gn_g, w_sb_out, w_ret_out, w_o, g_mix_post, g_ffn2_pre, w_ffn2_gu, w_ffn2_down, g_ffn2_post):
    batch, seq, d = x_prompt.shape
    n_seq, dec_t, _ = x_sample.shape
    depth = w_in.shape[0]
    n_pool = cache_k.shape[1]
    n_pages = page_table.shape[1]
    page = cache_k.shape[2]
    past_len = n_pages * page
    assert HEADS_PER_LANE_TILE == 2
    assert SB_HEADS_PER_STEP % HEADS_PER_LANE_TILE == 0 and SB_HEADS % SB_HEADS_PER_STEP == 0
    assert seq % RET_ROWS_PER_STEP == 0 and seq % SB_BLOCK == 0 and seq % ROW_TILE == 0
    assert (n_seq * dec_t) % ROW_TILE == 0 and ROW_TILE % dec_t == 0
    assert n_pages % DEC_PAGES_PER_STEP == 0 and SB_BLOCK % page == 0
    assert (DEC_PAGES_PER_STEP * page) % SB_BLOCK == 0 and dec_t % 8 == 0 and dec_t <= SB_BLOCK
    assert dec_t % RET_CHUNK != 0

    rope_p = _rope_tables(jnp.arange(seq))
    rope_s = _rope_tables(past_len + jnp.arange(ROW_TILE) % dec_t)
    decay_p = _decay_tables(RET_CHUNK, 1)
    decay_s = _decay_tables(dec_t, n_seq)
    tri = _tri(SB_BLOCK)
    tri2 = jnp.concatenate([tri, tri], axis=0)

    hp = x_prompt.reshape(batch * seq, d)
    hs = x_sample.reshape(n_seq * dec_t, d)
    kp_l, vp_l, sp_l, ks_l, vs_l, ss_l = [], [], [], [], [], []
    for l in range(depth):
        g1pre, g1post = g_ffn1_pre[l][None], g_ffn1_post[l][None]
        g2pre, g2post = g_ffn2_pre[l][None], g_ffn2_post[l][None]
        gmpre, gmpost = g_mix_pre[l][None], g_mix_post[l][None]
        gn = ret_gn_g[l][None]
        w1gu, w1down = w_ffn1_gu[l].astype(BF16), w_ffn1_down[l].astype(BF16)
        w2gu, w2down = w_ffn2_gu[l].astype(BF16), w_ffn2_down[l].astype(BF16)
        win = w_in[l].astype(BF16)
        w_kv_t = w_in[l][:, _OFF_KSB:_OFF_QR].T.astype(BF16)
        wsb, wret, wo = w_sb_out[l].astype(BF16), w_ret_out[l].astype(BF16), w_o[l].astype(BF16)
        bias = sb_bias[l].astype(F32)
        bias_rows = jnp.broadcast_to(jnp.repeat(bias, dec_t)[:, None], (SB_HEADS * dec_t, SB_BLOCK))
        ck_t = cache_k[l].transpose(0, 2, 3, 1).reshape(n_pool, SB_WIDTH, page)
        cv_t = cache_v[l].transpose(0, 2, 3, 1).reshape(n_pool, SB_WIDTH, page)

        h1 = _ffn_call(hp, g1pre, w1gu, w1down, g1post)
        (k_t, v_t, kt_bf, v_bf, q_s0, q_s1, q_r, k_r, v_r, g_r, a_sb, a_r) = _inproj_prompt_call(
            h1, gmpre, win, w_kv_t, *rope_p, batch, seq)
        o_sb = _sb_prompt_call([q_s0.reshape(batch, seq, SB_WIDTH), q_s1.reshape(batch, seq, SB_WIDTH)],
                               kt_bf, v_bf.reshape(batch, seq, SB_WIDTH), bias, tri)
        o_r, s_p = _ret_prompt_call(q_r.reshape(batch, seq, -1), k_r.reshape(batch, seq, -1),
                                    v_r.reshape(batch, seq, -1), g_r.reshape(batch, seq, -1),
                                    gn, decay_p)
        hp = _post_call(h1, o_sb.reshape(batch * seq, SB_WIDTH), o_r.reshape(batch * seq, RET_V_WIDTH),
                        a_sb, a_r, wsb, wret, wo, gmpost, g2pre, w2gu, w2down, g2post)
        kp_l.append(k_t.reshape(batch, SB_HEADS, SB_HEAD_DIM, seq).transpose(0, 3, 1, 2))
        vp_l.append(v_t.reshape(batch, SB_HEADS, SB_HEAD_DIM, seq).transpose(0, 3, 1, 2))
        sp_l.append(s_p)

        h1 = _ffn_call(hs, g1pre, w1gu, w1down, g1post)
        (k_sb, v_sb, q_sb, q_r, k_r, v_r, g_r, a_sb, a_r) = _inproj_decode_call(
            h1, gmpre, win, *rope_s)
        o_sb = _sb_decode_call(q_sb.reshape(n_seq, dec_t, SB_WIDTH), k_sb.reshape(n_seq, dec_t, SB_WIDTH),
                               v_sb.reshape(n_seq, dec_t, SB_WIDTH), ck_t, cv_t, page_table, bias_rows, tri2)
        o_r, s_s = _ret_decode_call(q_r, k_r, v_r, g_r, gn, state_ret[l], decay_s, dec_t)
        hs = _post_call(h1, o_sb.reshape(n_seq * dec_t, SB_WIDTH).astype(BF16), o_r, a_sb, a_r,
                        wsb, wret, wo, gmpost, g2pre, w2gu, w2down, g2post)
        ks_l.append(k_sb.reshape(n_seq, dec_t, SB_HEADS, SB_HEAD_DIM))
        vs_l.append(v_sb.reshape(n_seq, dec_t, SB_HEADS, SB_HEAD_DIM))
        ss_l.append(s_s)

    return (hp.reshape(batch, seq, d), hs.reshape(n_seq, dec_t, d),
            jnp.stack(kp_l), jnp.stack(vp_l), jnp.stack(sp_l),
            jnp.stack(ks_l), jnp.stack(vs_l), jnp.stack(ss_l))
```

```python
import functools

import jax
import jax.numpy as jnp
import numpy as np
from jax import lax
from jax.experimental import pallas as pl
from jax.experimental.pallas import tpu as pltpu

F32 = jnp.float32
BF16 = jnp.bfloat16

SB_HEADS = 8
SB_HEAD_DIM = 64
SB_WIDTH = SB_HEADS * SB_HEAD_DIM
RET_HEADS = 4
RET_QK_DIM = 128
RET_V_DIM = 256
RET_QK_WIDTH = RET_HEADS * RET_QK_DIM
RET_V_WIDTH = RET_HEADS * RET_V_DIM
RET_CHUNK = 128
ROPE_BASE = 10000.0
NORM_EPS = 1e-6
LOG2E = 1.4426950408889634

LANES = 128
HEADS_PER_LANE_TILE = LANES // SB_HEAD_DIM

ROW_TILE = 256
FFN_ROW_TILE = 512
SB_BLOCK = 256
SB_HEADS_PER_STEP = 4
DEC_PAGES_PER_STEP = 16
RET_ROWS_PER_STEP = 1024
VMEM_LIMIT = 56 * 1024 * 1024

_NT = (((1,), (1,)), ((), ()))


def _const_spec(shape):
    nd = len(shape)
    return pl.BlockSpec(shape, lambda *_: (0,) * nd, pipeline_mode=pl.Buffered(1))


def _params(n_axes, vmem=VMEM_LIMIT):
    return pltpu.CompilerParams(dimension_semantics=("arbitrary",) * n_axes,
                                vmem_limit_bytes=vmem)


def _rms(x, g):
    ms = jnp.mean(x * x, axis=-1, keepdims=True)
    return x * lax.rsqrt(ms + NORM_EPS) * g


def _silu(x):
    return x * jax.nn.sigmoid(x)


def _ffn_residual(x, g_pre, wgu_ref, wdown_ref, g_post):
    d_ff = wdown_ref.shape[0]
    xn = _rms(x, g_pre).astype(BF16)
    gate = jnp.dot(xn, wgu_ref[:, :d_ff], preferred_element_type=F32)
    up = jnp.dot(xn, wgu_ref[:, d_ff:], preferred_element_type=F32)
    act = (_silu(gate) * up).astype(BF16)
    y = jnp.dot(act, wdown_ref[...], preferred_element_type=F32)
    return x + 0.5 * _rms(y, g_post)


def _ffn_kernel(x_ref, gpre_ref, wgu_ref, wdown_ref, gpost_ref, o_ref):
    o_ref[...] = _ffn_residual(x_ref[...], gpre_ref[...], wgu_ref, wdown_ref, gpost_ref[...])


def _ffn_call(x, g_pre, wgu, wdown, g_post):
    n, d = x.shape
    tile = FFN_ROW_TILE if n % FFN_ROW_TILE == 0 else ROW_TILE
    row = pl.BlockSpec((tile, d), lambda i: (i, 0))
    return pl.pallas_call(
        _ffn_kernel,
        grid=(n // tile,),
        in_specs=[row, _const_spec(g_pre.shape), _const_spec(wgu.shape),
                  _const_spec(wdown.shape), _const_spec(g_post.shape)],
        out_specs=row,
        out_shape=jax.ShapeDtypeStruct((n, d), F32),
        compiler_params=_params(1),
        name="ffn_block",
    )(x, g_pre, wgu, wdown, g_post)


_OFF_QSB = 0
_OFF_KSB = _OFF_QSB + SB_WIDTH
_OFF_VSB = _OFF_KSB + SB_WIDTH
_OFF_QR = _OFF_VSB + SB_WIDTH
_OFF_KR = _OFF_QR + RET_QK_WIDTH
_OFF_VR = _OFF_KR + RET_QK_WIDTH
_OFF_GR = _OFF_VR + RET_V_WIDTH
_OFF_ASB = _OFF_GR + RET_V_WIDTH


def _inproj_shared(u, win_ref, cos_ref, sin_ref, qr_ref, kr_ref, vr_ref, gr_ref, asb_ref, ar_ref):
    def proj(lo, width):
        return jnp.dot(u, win_ref[:, lo:lo + width], preferred_element_type=F32)

    cos = cos_ref[...]
    sin = sin_ref[...]
    q_r = proj(_OFF_QR, RET_QK_WIDTH)
    k_r = proj(_OFF_KR, RET_QK_WIDTH)
    for hh in range(RET_HEADS):
        sl = slice(hh * RET_QK_DIM, (hh + 1) * RET_QK_DIM)
        qh = q_r[:, sl]
        kh = k_r[:, sl]
        qr_ref[:, sl] = qh * cos + pltpu.roll(qh, RET_QK_DIM // 2, axis=1) * sin
        kr_ref[:, sl] = (kh * cos + pltpu.roll(kh, RET_QK_DIM // 2, axis=1) * sin) * (RET_QK_DIM ** -0.5)
    vr_ref[...] = proj(_OFF_VR, RET_V_WIDTH).astype(BF16)
    gr_ref[...] = proj(_OFF_GR, RET_V_WIDTH)
    d_model = asb_ref.shape[1]
    asb_ref[...] = proj(_OFF_ASB, d_model)
    ar_ref[...] = proj(_OFF_ASB + d_model, d_model)


def _inproj_prompt_kernel(h_ref, g_ref, win_ref, wkvt_ref, cos_ref, sin_ref,
                          kt_ref, vt_ref, ktb_ref, vb_ref, *rest):
    q_refs, shared_refs = rest[:HEADS_PER_LANE_TILE], rest[HEADS_PER_LANE_TILE:]
    u = _rms(h_ref[...], g_ref[...]).astype(BF16)
    q = jnp.dot(u, win_ref[:, _OFF_QSB:_OFF_QSB + SB_WIDTH], preferred_element_type=F32)
    q = q * (SB_HEAD_DIM ** -0.5 * LOG2E)
    head_slot = (lax.broadcasted_iota(jnp.int32, (1, SB_WIDTH), 1) // SB_HEAD_DIM) % HEADS_PER_LANE_TILE
    for hh, q_ref in enumerate(q_refs):
        q_ref[...] = jnp.where(head_slot == hh, q, 0.0).astype(BF16)
    kv_t = lax.dot_general(wkvt_ref[...], u, _NT, preferred_element_type=F32)
    kt_ref[0] = kv_t[:SB_WIDTH]
    vt_ref[0] = kv_t[SB_WIDTH:]
    ktb_ref[0, 0] = kv_t[:SB_WIDTH].astype(BF16)
    vb_ref[...] = jnp.dot(u, win_ref[:, _OFF_VSB:_OFF_VSB + SB_WIDTH],
                          preferred_element_type=F32).astype(BF16)
    _inproj_shared(u, win_ref, cos_ref, sin_ref, *shared_refs)


def _inproj_decode_kernel(h_ref, g_ref, win_ref, cos_ref, sin_ref, k_ref, v_ref, q_ref, *shared_refs):
    u = _rms(h_ref[...], g_ref[...]).astype(BF16)
    q_ref[...] = jnp.dot(u, win_ref[:, _OFF_QSB:_OFF_QSB + SB_WIDTH],
                         preferred_element_type=F32) * (SB_HEAD_DIM ** -0.5)
    k_ref[...] = jnp.dot(u, win_ref[:, _OFF_KSB:_OFF_KSB + SB_WIDTH], preferred_element_type=F32)
    v_ref[...] = jnp.dot(u, win_ref[:, _OFF_VSB:_OFF_VSB + SB_WIDTH], preferred_element_type=F32)
    _inproj_shared(u, win_ref, cos_ref, sin_ref, *shared_refs)


def _row_spec(width):
    return pl.BlockSpec((ROW_TILE, width), lambda i: (i, 0))


def _shared_outputs(n, d):
    widths_dtypes = [(RET_QK_WIDTH, F32), (RET_QK_WIDTH, F32), (RET_V_WIDTH, BF16),
                     (RET_V_WIDTH, F32), (d, F32), (d, F32)]
    return ([_row_spec(w) for w, _ in widths_dtypes],
            [jax.ShapeDtypeStruct((n, w), dt) for w, dt in widths_dtypes])


def _inproj_prompt_call(h, g_pre, w_in, w_kv_t, cos_tab, sin_tab, batch, seq):
    n, d = h.shape
    assert ROW_TILE == SB_BLOCK
    tiles = seq // ROW_TILE
    tab = pl.BlockSpec((ROW_TILE, RET_QK_DIM), lambda i: (i % tiles, 0))
    t_spec = pl.BlockSpec((1, SB_WIDTH, ROW_TILE), lambda i: (i // tiles, 0, i % tiles))
    shared_specs, shared_shapes = _shared_outputs(n, d)
    return pl.pallas_call(
        _inproj_prompt_kernel,
        grid=(n // ROW_TILE,),
        in_specs=[_row_spec(d), _const_spec(g_pre.shape), _const_spec(w_in.shape),
                  _const_spec(w_kv_t.shape), tab, tab],
        out_specs=[t_spec, t_spec,
                   pl.BlockSpec((1, 1, SB_WIDTH, SB_BLOCK), lambda i: (i // tiles, i % tiles, 0, 0)),
                   _row_spec(SB_WIDTH)] + [_row_spec(SB_WIDTH)] * HEADS_PER_LANE_TILE + shared_specs,
        out_shape=[jax.ShapeDtypeStruct((batch, SB_WIDTH, seq), F32),
                   jax.ShapeDtypeStruct((batch, SB_WIDTH, seq), F32),
                   jax.ShapeDtypeStruct((batch, tiles, SB_WIDTH, SB_BLOCK), BF16),
                   jax.ShapeDtypeStruct((n, SB_WIDTH), BF16)]
                  + [jax.ShapeDtypeStruct((n, SB_WIDTH), BF16)] * HEADS_PER_LANE_TILE + shared_shapes,
        compiler_params=_params(1),
        name="in_projection_prompt",
    )(h, g_pre, w_in, w_kv_t, cos_tab, sin_tab)


def _inproj_decode_call(h, g_pre, w_in, cos_tab, sin_tab):
    n, d = h.shape
    tab = pl.BlockSpec((ROW_TILE, RET_QK_DIM), lambda i: (0, 0))
    shared_specs, shared_shapes = _shared_outputs(n, d)
    return pl.pallas_call(
        _inproj_decode_kernel,
        grid=(n // ROW_TILE,),
        in_specs=[_row_spec(d), _const_spec(g_pre.shape), _const_spec(w_in.shape), tab, tab],
        out_specs=[_row_spec(SB_WIDTH)] * 3 + shared_specs,
        out_shape=[jax.ShapeDtypeStruct((n, SB_WIDTH), F32)] * 3 + shared_shapes,
        compiler_params=_params(1),
        name="in_projection_decode",
    )(h, g_pre, w_in, cos_tab, sin_tab)


NULL_LOGIT = -1e30


def _softplus2(z2):
    neg_abs = pltpu.bitcast(pltpu.bitcast(z2, jnp.uint32) | jnp.uint32(0x80000000), F32)
    return jnp.maximum(z2, 0.0) + jnp.log(1.0 + jnp.exp2(neg_abs)) * LOG2E


def _sb_weights(z2, tri2, run2, mask):
    if mask is not None:
        z2 = jnp.where(mask, z2, NULL_LOGIT)
    sp2 = _softplus2(z2)
    hi = sp2.astype(BF16)
    lo = (sp2 - hi.astype(F32)).astype(BF16)
    csum = jnp.dot(jnp.concatenate([hi, lo], axis=1), tri2, preferred_element_type=F32) + run2
    return jnp.exp2(z2 - csum).astype(BF16), run2 + jnp.sum(sp2, axis=-1, keepdims=True)


_ITEM_QI, _ITEM_J, _ITEM_BIAS, _ITEM_FIRST, _ITEM_FIELDS = 0, 1, 2, 3, 4
_BIAS_FULL, _BIAS_DIAG, _BIAS_NULL = 0, 1, 2
_PIPE_DEPTH = 4


def _sb_items(n_blocks):
    pad = _PIPE_DEPTH - 1
    null = (0, 0, _BIAS_NULL, 1)
    items = [null] * pad
    for qi in range(n_blocks):
        for j in range(qi, -1, -1):
            items.append((qi, j, _BIAS_DIAG if j == qi else _BIAS_FULL, int(j == qi)))
    items += [null] * pad
    return jnp.asarray(items, jnp.int32).T.reshape(-1), len(items)


def _sb_prompt_kernel(items_ref, bias_ref, *refs, n_items):
    slots = HEADS_PER_LANE_TILE
    q_refs, (kt_ref, v_ref, tri_ref, o_ref) = refs[:slots], refs[slots:slots + 4]
    bsel_scr, zraw_scr, z_scr, sp_scr, c_scr, acc_scr, run_scr = refs[slots + 4:]
    nh = acc_scr.shape[0]
    hg = pl.program_id(1)
    blk = tri_ref.shape[0]
    tri = tri_ref[...]
    lane_head = lax.broadcasted_iota(jnp.int32, (1, LANES), 1) // SB_HEAD_DIM
    row = lax.broadcasted_iota(jnp.int32, (blk, blk), 0)
    col = lax.broadcasted_iota(jnp.int32, (blk, blk), 1)
    heads = range(nh)

    def item(field, i):
        return items_ref[field * n_items + i]

    for hh in heads:
        bias2 = bias_ref[hg * nh + hh] * LOG2E
        bsel_scr[hh, _BIAS_FULL] = jnp.full((blk, blk), bias2, F32)
        bsel_scr[hh, _BIAS_DIAG] = jnp.where(col < row, bias2, NULL_LOGIT)
        bsel_scr[hh, _BIAS_NULL] = jnp.full((blk, blk), NULL_LOGIT, F32)
        zraw_scr[hh] = jnp.full((blk, blk), NULL_LOGIT, F32)
        z_scr[0, hh] = jnp.full((blk, blk), NULL_LOGIT, F32)
        z_scr[1, hh] = jnp.full((blk, blk), NULL_LOGIT, F32)
        sp_scr[hh] = jnp.zeros((blk, blk), BF16)
        c_scr[hh] = jnp.zeros((blk, blk), F32)
        acc_scr[hh] = jnp.zeros((blk, LANES), F32)
        run_scr[hh] = jnp.zeros((blk, 1), F32)

    def trip(t, _):
        slot = t % 2
        keep_w = 1.0 - item(_ITEM_FIRST, t).astype(F32)
        keep_s = 1.0 - item(_ITEM_FIRST, t + 2).astype(F32)
        j_l = item(_ITEM_J, t + 3)
        bias_l = item(_ITEM_BIAS, t + 3)
        v_rows = pl.ds(pl.multiple_of(item(_ITEM_J, t) * blk, blk), blk)
        q_rows = pl.ds(pl.multiple_of(item(_ITEM_QI, t + 3) * blk, blk), blk)
        o_rows = pl.ds(pl.multiple_of(item(_ITEM_QI, t) * blk, blk), blk)
        for lt in range(nh // slots):
            hs = slice(lt * slots, (lt + 1) * slots)
            lanes = slice(lt * LANES, (lt + 1) * LANES)
            a = jnp.exp2(z_scr[slot, hs] - c_scr[hs]).astype(BF16).reshape(slots * blk, blk)
            av = jnp.dot(a, v_ref[0, v_rows, lanes], preferred_element_type=F32).reshape(slots, blk, LANES)
            acc = acc_scr[hs] * keep_w + av
            acc_scr[hs] = acc
            out = acc[0]
            for s in range(1, slots):
                out = jnp.where(lane_head == s, acc[s], out)
            o_ref[0, o_rows, lanes] = out.astype(o_ref.dtype)
            c_scr[hs] = jnp.dot(sp_scr[hs].reshape(slots * blk, blk), tri,
                                preferred_element_type=F32).reshape(slots, blk, blk)
            z2 = zraw_scr[hs]
            sp2 = _softplus2(z2)
            run2 = run_scr[hs] * keep_s
            z_scr[slot, hs] = z2 - run2
            sp_scr[hs] = sp2.astype(BF16)
            run_scr[hs] = run2 + jnp.sum(sp2, axis=-1, keepdims=True)
            q = jnp.concatenate([q_refs[s][0, q_rows, lanes] for s in range(slots)], axis=0)
            z_new = jnp.dot(q, kt_ref[0, j_l, lanes, :], preferred_element_type=F32)
            zraw_scr[hs] = z_new.reshape(slots, blk, blk) + bsel_scr[hs, bias_l]
        return 0

    lax.fori_loop(0, n_items - (_PIPE_DEPTH - 1), trip, 0)


def _sb_prompt_call(q_slots, kt_bf, v_bf, bias, tri):
    b, t, w = v_bf.shape
    blk = SB_BLOCK
    nh = SB_HEADS_PER_STEP
    gw = nh * SB_HEAD_DIM
    items, n_items = _sb_items(t // blk)
    seq_spec = pl.BlockSpec((1, t, gw), lambda bi, hg, it: (bi, 0, hg))
    grid_spec = pltpu.PrefetchScalarGridSpec(
        num_scalar_prefetch=1,
        grid=(b, w // gw),
        in_specs=([pl.BlockSpec(memory_space=pltpu.SMEM)] + [seq_spec] * HEADS_PER_LANE_TILE
                  + [pl.BlockSpec((1, t // blk, gw, blk), lambda bi, hg, it: (bi, 0, hg, 0)),
                     seq_spec,
                     pl.BlockSpec(tri.shape, lambda bi, hg, it: (0, 0))]),
        out_specs=seq_spec,
        scratch_shapes=[pltpu.VMEM((nh, 3, blk, blk), F32), pltpu.VMEM((nh, blk, blk), F32),
                        pltpu.VMEM((2, nh, blk, blk), F32), pltpu.VMEM((nh, blk, blk), BF16),
                        pltpu.VMEM((nh, blk, blk), F32), pltpu.VMEM((nh, blk, LANES), F32),
                        pltpu.VMEM((nh, blk, 1), F32)],
    )
    return pl.pallas_call(
        functools.partial(_sb_prompt_kernel, n_items=n_items),
        grid_spec=grid_spec,
        out_shape=jax.ShapeDtypeStruct((b, t, w), BF16),
        compiler_params=_params(2),
        name="sb_attention_prompt",
    )(items, bias, *q_slots, kt_bf, v_bf, tri)


def _sb_decode_kernel(pt_ref, q_ref, kn_ref, vn_ref, bias_ref, tri_ref, *rest, pages):
    del pt_ref
    k_refs = rest[:pages]
    v_refs = rest[pages:2 * pages]
    o_ref = rest[2 * pages]
    qbd_ref, run_ref, acc_ref = rest[2 * pages + 1:]
    g = pl.program_id(1)
    dec_t = q_ref.shape[1]
    rows = SB_HEADS * dec_t
    blk = tri_ref.shape[1]
    bias = bias_ref[...] * LOG2E
    tri = tri_ref[...]
    lane_head = lax.broadcasted_iota(jnp.int32, (1, SB_WIDTH), 1) // SB_HEAD_DIM

    @pl.when(g == 0)
    def _():
        q = q_ref[0] * LOG2E
        qbd = jnp.concatenate([jnp.where(lane_head == hh, q, 0.0) for hh in range(SB_HEADS)], axis=0)
        qbd_ref[...] = qbd.astype(BF16)
        pad = jnp.zeros((blk - dec_t, SB_WIDTH), F32)
        k_new = jnp.concatenate([kn_ref[0], pad], axis=0).astype(BF16)
        v_new = jnp.concatenate([vn_ref[0], pad], axis=0).astype(BF16)
        t_row = lax.broadcasted_iota(jnp.int32, (rows, blk), 0) % dec_t
        col = lax.broadcasted_iota(jnp.int32, (rows, blk), 1)
        z = lax.dot_general(qbd_ref[...], k_new, _NT, preferred_element_type=F32) + bias
        a, run = _sb_weights(z, tri, jnp.zeros((rows, 1), F32), col < t_row)
        run_ref[...] = jnp.broadcast_to(run, run_ref.shape)
        acc_ref[...] = jnp.dot(a, v_new, preferred_element_type=F32)

    qbd = qbd_ref[...]
    run = run_ref[:, 0:1]
    acc = acc_ref[...]
    pages_per_blk = blk // k_refs[0].shape[2]
    groups = [slice(p * pages_per_blk, (p + 1) * pages_per_blk)
              for p in reversed(range(pages // pages_per_blk))]
    zs = [jnp.dot(qbd, jnp.concatenate([r[0] for r in k_refs[sl]], axis=1).astype(BF16),
                  preferred_element_type=F32) + bias for sl in groups]
    sps = [_softplus2(z) for z in zs]
    halves = []
    for sp2 in sps:
        hi = sp2.astype(BF16)
        halves.append(jnp.concatenate([hi, (sp2 - hi.astype(F32)).astype(BF16)], axis=1))
    csums = [jnp.dot(h, tri, preferred_element_type=F32) for h in halves]
    for z, sp2, csum, sl in zip(zs, sps, csums, groups):
        a = jnp.exp2(z - csum - run).astype(BF16)
        v_t = jnp.concatenate([r[0] for r in v_refs[sl]], axis=1).astype(BF16)
        acc = acc + lax.dot_general(a, v_t, _NT, preferred_element_type=F32)
        run = run + jnp.sum(sp2, axis=-1, keepdims=True)
    run_ref[...] = jnp.broadcast_to(run, run_ref.shape)
    acc_ref[...] = acc

    @pl.when(g == pl.num_programs(1) - 1)
    def _():
        out = jnp.zeros((dec_t, SB_WIDTH), F32)
        for hh in range(SB_HEADS):
            out = jnp.where(lane_head == hh, acc[hh * dec_t:(hh + 1) * dec_t, :], out)
        o_ref[0] = out


def _sb_decode_call(q, k_new, v_new, cache_kt, cache_vt, page_table, bias_rows, tri):
    n_seq, dec_t, w = q.shape
    n_pages = page_table.shape[1]
    pages = DEC_PAGES_PER_STEP
    page = cache_kt.shape[2]
    rows = SB_HEADS * dec_t
    seq_spec = pl.BlockSpec((1, dec_t, w), lambda s, g, pt: (s, 0, 0))

    def page_spec(i):
        return pl.BlockSpec((1, w, page), lambda s, g, pt: (pt[s, n_pages - (g + 1) * pages + i], 0, 0))

    def const(shape):
        nd = len(shape)
        return pl.BlockSpec(shape, lambda s, g, pt: (0,) * nd)

    grid_spec = pltpu.PrefetchScalarGridSpec(
        num_scalar_prefetch=1,
        grid=(n_seq, n_pages // pages),
        in_specs=([seq_spec, seq_spec, seq_spec, const(bias_rows.shape), const(tri.shape)]
                  + [page_spec(i) for i in range(pages)] * 2),
        out_specs=seq_spec,
        scratch_shapes=[pltpu.VMEM((rows, w), BF16), pltpu.VMEM((rows, LANES), F32),
                        pltpu.VMEM((rows, w), F32)],
    )
    return pl.pallas_call(
        functools.partial(_sb_decode_kernel, pages=pages),
        grid_spec=grid_spec,
        out_shape=jax.ShapeDtypeStruct((n_seq, dec_t, w), F32),
        compiler_params=_params(2),
        name="sb_attention_decode",
    )(page_table, q, k_new, v_new, bias_rows, tri, *([cache_kt] * pages), *([cache_vt] * pages))


def _head_norm_gate(o, gn, gate):
    o = o * lax.rsqrt(jnp.mean(o * o, axis=-1, keepdims=True) + NORM_EPS) * gn
    return (_silu(gate) * o).astype(BF16)


def _ret_prompt_kernel(dc_ref, q_ref, k_ref, v_ref, g_ref, gn_ref, din_ref, dq_ref, dk_ref,
                       o_ref, s_out_ref, s_ref):
    hh = pl.program_id(1)
    c = pl.program_id(2)

    @pl.when(c == 0)
    def _():
        s_ref[...] = jnp.zeros_like(s_ref)

    dc = dc_ref[hh]
    din = din_ref[0]
    dq = dq_ref[0]
    dk = dk_ref[0]
    gn = gn_ref[...]
    chunks = [slice(i * RET_CHUNK, (i + 1) * RET_CHUNK) for i in range(q_ref.shape[1] // RET_CHUNK)]
    qs = [q_ref[0, sl, :].astype(BF16) for sl in chunks]
    ks = [k_ref[0, sl, :] for sl in chunks]
    vs = [v_ref[0, sl, :] for sl in chunks]
    inners = [lax.dot_general(q, k.astype(BF16), _NT, preferred_element_type=F32) * din
              for q, k in zip(qs, ks)]
    gains = [jnp.dot((k * dk).T.astype(BF16), v, preferred_element_type=F32) for k, v in zip(ks, vs)]
    intras = [jnp.dot(inner.astype(BF16), v, preferred_element_type=F32) for inner, v in zip(inners, vs)]
    states = [s_ref[...]]
    for gain in gains:
        states.append(states[-1] * dc + gain)
    for sl, q, intra, state in zip(chunks, qs, intras, states):
        o = intra + jnp.dot(q, state.astype(BF16), preferred_element_type=F32) * dq
        o_ref[0, sl, :] = _head_norm_gate(o, gn, g_ref[0, sl, :])
    state = states[-1]
    s_ref[...] = state

    @pl.when(c == pl.num_programs(2) - 1)
    def _():
        s_out_ref[0, 0] = state


def _ret_prompt_call(q, k, v_bf, gate, gn, tables):
    b, t, _ = q.shape
    din, dq, dk, dc = tables
    rows = RET_ROWS_PER_STEP
    qk_spec = pl.BlockSpec((1, rows, RET_QK_DIM), lambda bi, h, c: (bi, c, h))
    v_spec = pl.BlockSpec((1, rows, RET_V_DIM), lambda bi, h, c: (bi, c, h))

    def head(shape):
        return pl.BlockSpec((1,) + shape, lambda bi, h, c: (h, 0, 0))

    return pl.pallas_call(
        _ret_prompt_kernel,
        grid=(b, RET_HEADS, t // rows),
        in_specs=[pl.BlockSpec(memory_space=pltpu.SMEM), qk_spec, qk_spec, v_spec, v_spec,
                  pl.BlockSpec((1, RET_V_DIM), lambda bi, h, c: (0, h)),
                  head((RET_CHUNK, RET_CHUNK)), head((RET_CHUNK, RET_V_DIM)),
                  head((RET_CHUNK, RET_QK_DIM))],
        out_specs=[v_spec,
                   pl.BlockSpec((1, 1, RET_QK_DIM, RET_V_DIM), lambda bi, h, c: (bi, h, 0, 0))],
        out_shape=[jax.ShapeDtypeStruct((b, t, RET_V_WIDTH), BF16),
                   jax.ShapeDtypeStruct((b, RET_HEADS, RET_QK_DIM, RET_V_DIM), F32)],
        scratch_shapes=[pltpu.VMEM((RET_QK_DIM, RET_V_DIM), F32)],
        compiler_params=_params(3),
        name="retention_prompt",
    )(dc, q, k, v_bf, gate, gn, din, dq, dk)


def _ret_decode_kernel(dc_ref, q_ref, k_ref, v_ref, g_ref, gn_ref, s_in_ref, din_ref, dq_ref, dk_ref,
                       o_ref, s_out_ref, *, dec_t):
    hh = pl.program_id(0)
    n_seq = s_in_ref.shape[0]
    rows = n_seq * dec_t
    q = q_ref[...]
    k = k_ref[...]
    v = v_ref[...]
    q_bf = q.astype(BF16)
    inner = lax.dot_general(q_bf, k.astype(BF16), _NT, preferred_element_type=F32) * din_ref[0]
    o = jnp.dot(inner.astype(BF16), v, preferred_element_type=F32)

    s_old = s_in_ref[:, 0].reshape(n_seq * RET_QK_DIM, RET_V_DIM)
    row_seq = lax.broadcasted_iota(jnp.int32, (rows, RET_QK_DIM), 0) // dec_t
    q_bd = jnp.concatenate([jnp.where(row_seq == s, q, 0.0) for s in range(n_seq)], axis=1)
    o = o + jnp.dot(q_bd.astype(BF16), s_old.astype(BF16), preferred_element_type=F32) * dq_ref[0]
    o_ref[...] = _head_norm_gate(o, gn_ref[...], g_ref[...])

    kd_t = (k * dk_ref[0]).T
    col_seq = lax.broadcasted_iota(jnp.int32, (RET_QK_DIM, rows), 1) // dec_t
    k_bd_t = jnp.concatenate([jnp.where(col_seq == s, kd_t, 0.0) for s in range(n_seq)], axis=0)
    s_new = s_old * dc_ref[hh] + jnp.dot(k_bd_t.astype(BF16), v, preferred_element_type=F32)
    s_out_ref[:, 0] = s_new.reshape(n_seq, RET_QK_DIM, RET_V_DIM)


def _ret_decode_call(q, k, v_bf, gate, gn, state, tables, dec_t):
    rows = q.shape[0]
    n_seq = state.shape[0]
    din, dq, dk, dc = tables
    qk_spec = pl.BlockSpec((rows, RET_QK_DIM), lambda h: (0, h))
    v_spec = pl.BlockSpec((rows, RET_V_DIM), lambda h: (0, h))
    s_spec = pl.BlockSpec((n_seq, 1, RET_QK_DIM, RET_V_DIM), lambda h: (0, h, 0, 0))

    def head(shape):
        return pl.BlockSpec((1,) + shape, lambda h: (h, 0, 0))

    return pl.pallas_call(
        functools.partial(_ret_decode_kernel, dec_t=dec_t),
        grid=(RET_HEADS,),
        in_specs=[pl.BlockSpec(memory_space=pltpu.SMEM), qk_spec, qk_spec, v_spec, v_spec,
                  pl.BlockSpec((1, RET_V_DIM), lambda h: (0, h)), s_spec,
                  head((rows, rows)), head((rows, RET_V_DIM)), head((rows, RET_QK_DIM))],
        out_specs=[v_spec, s_spec],
        out_shape=[jax.ShapeDtypeStruct((rows, RET_V_WIDTH), BF16),
                   jax.ShapeDtypeStruct(state.shape, F32)],
        compiler_params=_params(1),
        name="retention_decode",
    )(dc, q, k, v_bf, gate, gn, state, din, dq, dk)


def _post_kernel(h_ref, osb_ref, or_ref, asb_ref, ar_ref, wsb_ref, wret_ref, wo_ref, gmix_ref,
                 gpre_ref, wgu_ref, wdown_ref, gpost_ref, o_ref):
    m = (jax.nn.sigmoid(asb_ref[...]) * jnp.dot(osb_ref[...], wsb_ref[...], preferred_element_type=F32)
         + jax.nn.sigmoid(ar_ref[...]) * jnp.dot(or_ref[...], wret_ref[...], preferred_element_type=F32))
    mix = jnp.dot(m.astype(BF16), wo_ref[...], preferred_element_type=F32)
    h = h_ref[...] + _rms(mix, gmix_ref[...])
    o_ref[...] = _ffn_residual(h, gpre_ref[...], wgu_ref, wdown_ref, gpost_ref[...])


def _post_call(h, o_sb, o_r, a_sb, a_r, w_sb, w_ret, w_o, g_mix, g_pre, wgu, wdown, g_post):
    n, d = h.shape
    consts = [w_sb, w_ret, w_o, g_mix, g_pre, wgu, wdown, g_post]
    return pl.pallas_call(
        _post_kernel,
        grid=(n // ROW_TILE,),
        in_specs=[_row_spec(d), _row_spec(SB_WIDTH), _row_spec(RET_V_WIDTH), _row_spec(d), _row_spec(d)]
                 + [_const_spec(c.shape) for c in consts],
        out_specs=_row_spec(d),
        out_shape=jax.ShapeDtypeStruct((n, d), F32),
        compiler_params=_params(1),
        name="merge_out_ffn",
    )(h, o_sb, o_r, a_sb, a_r, *consts)


def _rope_tables(pos):
    half = RET_QK_DIM // 2
    freq = ROPE_BASE ** (-np.arange(half, dtype=np.float64) / half)
    ang = np.asarray(pos, np.float64)[:, None] * freq[None, :]
    cos, sin = np.cos(ang), np.sin(ang)
    return (jnp.asarray(np.concatenate([cos, cos], axis=1), F32),
            jnp.asarray(np.concatenate([-sin, sin], axis=1), F32))


def _decay_tables(chunk, reps):
    log_gamma = np.log1p(-np.exp2(-5.0 - np.arange(RET_HEADS, dtype=np.float64)))
    idx = np.arange(chunk, dtype=np.float64)
    diff = idx[:, None] - idx[None, :]
    d_in = np.where(diff >= 0, np.exp(log_gamma[:, None, None] * np.maximum(diff, 0.0)), 0.0)
    d_q = np.exp(log_gamma[:, None] * (idx + 1.0))
    d_k = np.exp(log_gamma[:, None] * (chunk - 1.0 - idx))
    d_c = np.exp(log_gamma * chunk)
    if reps > 1:
        seq = np.arange(chunk * reps) // chunk
        d_in = np.where(seq[:, None] == seq[None, :], np.tile(d_in, (1, reps, reps)), 0.0)
        d_q = np.tile(d_q, (1, reps))
        d_k = np.tile(d_k, (1, reps))
    n = chunk * reps
    d_q = np.broadcast_to(d_q[:, :, None], (RET_HEADS, n, RET_V_DIM))
    d_k = np.broadcast_to(d_k[:, :, None], (RET_HEADS, n, RET_QK_DIM))
    return tuple(jnp.asarray(a, F32) for a in (d_in, d_q, d_k, d_c))


def _tri(n):
    i = np.arange(n)
    return jnp.asarray(i[:, None] >= i[None, :], BF16)


def kernel(x_prompt, x_sample, cache_k, cache_v, state_ret, page_table, g_ffn1_pre, w_ffn1_gu, w_ffn1_down, g_ffn1_post, g_mix_pre, w_in, sb_bias, ret_gn_g, w_sb_out, w_ret_out, w_o, g_mix_post, g_ffn2_pre, w_ffn2_gu, w_ffn2_down, g_ffn2_post):
    batch, seq, d = x_prompt.shape
    n_seq, dec_t, _ = x_sample.shape
    depth = w_in.shape[0]
    n_pool = cache_k.shape[1]
    n_pages = page_table.shape[1]
    page = cache_k.shape[2]
    past_len = n_pages * page
    assert HEADS_PER_LANE_TILE == 2
    assert SB_HEADS_PER_STEP % HEADS_PER_LANE_TILE == 0 and SB_HEADS % SB_HEADS_PER_STEP == 0
    assert seq % RET_ROWS_PER_STEP == 0 and seq % SB_BLOCK == 0 and seq % ROW_TILE == 0
    assert (n_seq * dec_t) % ROW_TILE == 0 and ROW_TILE % dec_t == 0
    assert n_pages % DEC_PAGES_PER_STEP == 0 and SB_BLOCK % page == 0
    assert (DEC_PAGES_PER_STEP * page) % SB_BLOCK == 0 and dec_t % 8 == 0 and dec_t <= SB_BLOCK
    assert dec_t % RET_CHUNK != 0

    rope_p = _rope_tables(np.arange(seq))
    rope_s = _rope_tables(past_len + np.arange(ROW_TILE) % dec_t)
    decay_p = _decay_tables(RET_CHUNK, 1)
    decay_s = _decay_tables(dec_t, n_seq)
    tri = _tri(SB_BLOCK)
    tri2 = jnp.concatenate([tri, tri], axis=0)

    hp = x_prompt.reshape(batch * seq, d)
    hs = x_sample.reshape(n_seq * dec_t, d)
    kp_l, vp_l, sp_l, ks_l, vs_l, ss_l = [], [], [], [], [], []
    for l in range(depth):
        g1pre, g1post = g_ffn1_pre[l][None], g_ffn1_post[l][None]
        g2pre, g2post = g_ffn2_pre[l][None], g_ffn2_post[l][None]
        gmpre, gmpost = g_mix_pre[l][None], g_mix_post[l][None]
        gn = ret_gn_g[l][None]
        w1gu, w1down = w_ffn1_gu[l].astype(BF16), w_ffn1_down[l].astype(BF16)
        w2gu, w2down = w_ffn2_gu[l].astype(BF16), w_ffn2_down[l].astype(BF16)
        win = w_in[l].astype(BF16)
        w_kv_t = w_in[l][:, _OFF_KSB:_OFF_QR].T.astype(BF16)
        wsb, wret, wo = w_sb_out[l].astype(BF16), w_ret_out[l].astype(BF16), w_o[l].astype(BF16)
        bias = sb_bias[l].astype(F32)
        bias_rows = jnp.broadcast_to(jnp.repeat(bias, dec_t)[:, None], (SB_HEADS * dec_t, SB_BLOCK))
        ck_t = cache_k[l].transpose(0, 2, 3, 1).reshape(n_pool, SB_WIDTH, page)
        cv_t = cache_v[l].transpose(0, 2, 3, 1).reshape(n_pool, SB_WIDTH, page)

        h1 = _ffn_call(hp, g1pre, w1gu, w1down, g1post)
        (k_t, v_t, kt_bf, v_bf, q_s0, q_s1, q_r, k_r, v_r, g_r, a_sb, a_r) = _inproj_prompt_call(
            h1, gmpre, win, w_kv_t, *rope_p, batch, seq)
        o_sb = _sb_prompt_call([q_s0.reshape(batch, seq, SB_WIDTH), q_s1.reshape(batch, seq, SB_WIDTH)],
                               kt_bf, v_bf.reshape(batch, seq, SB_WIDTH), bias, tri)
        o_r, s_p = _ret_prompt_call(q_r.reshape(batch, seq, -1), k_r.reshape(batch, seq, -1),
                                    v_r.reshape(batch, seq, -1), g_r.reshape(batch, seq, -1),
                                    gn, decay_p)
        hp = _post_call(h1, o_sb.reshape(batch * seq, SB_WIDTH), o_r.reshape(batch * seq, RET_V_WIDTH),
                        a_sb, a_r, wsb, wret, wo, gmpost, g2pre, w2gu, w2down, g2post)
        kp_l.append(k_t.reshape(batch, SB_HEADS, SB_HEAD_DIM, seq).transpose(0, 3, 1, 2))
        vp_l.append(v_t.reshape(batch, SB_HEADS, SB_HEAD_DIM, seq).transpose(0, 3, 1, 2))
        sp_l.append(s_p)

        h1 = _ffn_call(hs, g1pre, w1gu, w1down, g1post)
        (k_sb, v_sb, q_sb, q_r, k_r, v_r, g_r, a_sb, a_r) = _inproj_decode_call(
            h1, gmpre, win, *rope_s)
        o_sb = _sb_decode_call(q_sb.reshape(n_seq, dec_t, SB_WIDTH), k_sb.reshape(n_seq, dec_t, SB_WIDTH),
                               v_sb.reshape(n_seq, dec_t, SB_WIDTH), ck_t, cv_t, page_table, bias_rows, tri2)
        o_r, s_s = _ret_decode_call(q_r, k_r, v_r, g_r, gn, state_ret[l], decay_s, dec_t)
        hs = _post_call(h1, o_sb.reshape(n_seq * dec_t, SB_WIDTH).astype(BF16), o_r, a_sb, a_r,
                        wsb, wret, wo, gmpost, g2pre, w2gu, w2down, g2post)
        ks_l.append(k_sb.reshape(n_seq, dec_t, SB_HEADS, SB_HEAD_DIM))
        vs_l.append(v_sb.reshape(n_seq, dec_t, SB_HEADS, SB_HEAD_DIM))
        ss_l.append(s_s)

    return (hp.reshape(batch, seq, d), hs.reshape(n_seq, dec_t, d),
            jnp.stack(kp_l), jnp.stack(vp_l), jnp.stack(sp_l),
            jnp.stack(ks_l), jnp.stack(vs_l), jnp.stack(ss_l))
```

```python
import functools

import jax
import jax.numpy as jnp
import numpy as np
from jax import lax
from jax.experimental import pallas as pl
from jax.experimental.pallas import tpu as pltpu

F32 = jnp.float32
BF16 = jnp.bfloat16

SB_HEADS = 8
SB_HEAD_DIM = 64
SB_WIDTH = SB_HEADS * SB_HEAD_DIM
RET_HEADS = 4
RET_QK_DIM = 128
RET_V_DIM = 256
RET_QK_WIDTH = RET_HEADS * RET_QK_DIM
RET_V_WIDTH = RET_HEADS * RET_V_DIM
RET_CHUNK = 128
ROPE_BASE = 10000.0
NORM_EPS = 1e-6
LOG2E = 1.4426950408889634

LANES = 128
HEADS_PER_LANE_TILE = LANES // SB_HEAD_DIM

ROW_TILE = 256
FFN_ROW_TILE = 512
SB_BLOCK = 256
SB_HEADS_PER_STEP = 4
DEC_PAGES_PER_STEP = 32
RET_ROWS_PER_STEP = 4096
VMEM_LIMIT = 56 * 1024 * 1024

_NT = (((1,), (1,)), ((), ()))


def _const_spec(shape):
    nd = len(shape)
    return pl.BlockSpec(shape, lambda *_: (0,) * nd, pipeline_mode=pl.Buffered(1))


def _params(n_axes, vmem=VMEM_LIMIT):
    return pltpu.CompilerParams(dimension_semantics=("arbitrary",) * n_axes,
                                vmem_limit_bytes=vmem)


def _rms(x, g):
    ms = jnp.mean(x * x, axis=-1, keepdims=True)
    return x * lax.rsqrt(ms + NORM_EPS) * g


def _silu(x):
    return x * jax.nn.sigmoid(x)


def _ffn_residual(x, g_pre, wgu_ref, wdown_ref, g_post):
    d_ff = wdown_ref.shape[0]
    xn = _rms(x, g_pre).astype(BF16)
    gate = jnp.dot(xn, wgu_ref[:, :d_ff], preferred_element_type=F32)
    up = jnp.dot(xn, wgu_ref[:, d_ff:], preferred_element_type=F32)
    act = (_silu(gate) * up).astype(BF16)
    y = jnp.dot(act, wdown_ref[...], preferred_element_type=F32)
    return x + 0.5 * _rms(y, g_post)


def _ffn_kernel(x_ref, gpre_ref, wgu_ref, wdown_ref, gpost_ref, o_ref):
    o_ref[...] = _ffn_residual(x_ref[...], gpre_ref[...], wgu_ref, wdown_ref, gpost_ref[...])


def _ffn_call(x, g_pre, wgu, wdown, g_post):
    n, d = x.shape
    tile = FFN_ROW_TILE if n % FFN_ROW_TILE == 0 else ROW_TILE
    row = pl.BlockSpec((tile, d), lambda i: (i, 0))
    return pl.pallas_call(
        _ffn_kernel,
        grid=(n // tile,),
        in_specs=[row, _const_spec(g_pre.shape), _const_spec(wgu.shape),
                  _const_spec(wdown.shape), _const_spec(g_post.shape)],
        out_specs=row,
        out_shape=jax.ShapeDtypeStruct((n, d), F32),
        compiler_params=_params(1),
        name="ffn_block",
    )(x, g_pre, wgu, wdown, g_post)


_OFF_QSB = 0
_OFF_KSB = _OFF_QSB + SB_WIDTH
_OFF_VSB = _OFF_KSB + SB_WIDTH
_OFF_QR = _OFF_VSB + SB_WIDTH
_OFF_KR = _OFF_QR + RET_QK_WIDTH
_OFF_VR = _OFF_KR + RET_QK_WIDTH
_OFF_GR = _OFF_VR + RET_V_WIDTH
_OFF_ASB = _OFF_GR + RET_V_WIDTH


def _inproj_shared(u, win_ref, cos_ref, sin_ref, qr_ref, kr_ref, vr_ref, gr_ref, asb_ref, ar_ref):
    def proj(lo, width):
        return jnp.dot(u, win_ref[:, lo:lo + width], preferred_element_type=F32)

    cos = cos_ref[...]
    sin = sin_ref[...]
    q_r = proj(_OFF_QR, RET_QK_WIDTH)
    k_r = proj(_OFF_KR, RET_QK_WIDTH)
    for hh in range(RET_HEADS):
        sl = slice(hh * RET_QK_DIM, (hh + 1) * RET_QK_DIM)
        qh = q_r[:, sl]
        kh = k_r[:, sl]
        qr_ref[:, sl] = qh * cos + pltpu.roll(qh, RET_QK_DIM // 2, axis=1) * sin
        kr_ref[:, sl] = (kh * cos + pltpu.roll(kh, RET_QK_DIM // 2, axis=1) * sin) * (RET_QK_DIM ** -0.5)
    vr_ref[...] = proj(_OFF_VR, RET_V_WIDTH).astype(BF16)
    gr_ref[...] = proj(_OFF_GR, RET_V_WIDTH)
    d_model = asb_ref.shape[1]
    asb_ref[...] = proj(_OFF_ASB, d_model)
    ar_ref[...] = proj(_OFF_ASB + d_model, d_model)


def _inproj_prompt_kernel(h_ref, g_ref, win_ref, wkvt_ref, cos_ref, sin_ref,
                          kt_ref, vt_ref, ktb_ref, vb_ref, *rest):
    q_refs, shared_refs = rest[:HEADS_PER_LANE_TILE], rest[HEADS_PER_LANE_TILE:]
    u = _rms(h_ref[...], g_ref[...]).astype(BF16)
    q = jnp.dot(u, win_ref[:, _OFF_QSB:_OFF_QSB + SB_WIDTH], preferred_element_type=F32)
    q = q * (SB_HEAD_DIM ** -0.5 * LOG2E)
    head_slot = (lax.broadcasted_iota(jnp.int32, (1, SB_WIDTH), 1) // SB_HEAD_DIM) % HEADS_PER_LANE_TILE
    for hh, q_ref in enumerate(q_refs):
        q_ref[...] = jnp.where(head_slot == hh, q, 0.0).astype(BF16)
    kv_t = lax.dot_general(wkvt_ref[...], u, _NT, preferred_element_type=F32)
    kt_ref[0] = kv_t[:SB_WIDTH]
    vt_ref[0] = kv_t[SB_WIDTH:]
    ktb_ref[0, 0] = kv_t[:SB_WIDTH].astype(BF16)
    vb_ref[...] = jnp.dot(u, win_ref[:, _OFF_VSB:_OFF_VSB + SB_WIDTH],
                          preferred_element_type=F32).astype(BF16)
    _inproj_shared(u, win_ref, cos_ref, sin_ref, *shared_refs)


def _inproj_decode_kernel(h_ref, g_ref, win_ref, cos_ref, sin_ref, k_ref, v_ref, q_ref, *shared_refs):
    u = _rms(h_ref[...], g_ref[...]).astype(BF16)
    q_ref[...] = jnp.dot(u, win_ref[:, _OFF_QSB:_OFF_QSB + SB_WIDTH],
                         preferred_element_type=F32) * (SB_HEAD_DIM ** -0.5)
    k_ref[...] = jnp.dot(u, win_ref[:, _OFF_KSB:_OFF_KSB + SB_WIDTH], preferred_element_type=F32)
    v_ref[...] = jnp.dot(u, win_ref[:, _OFF_VSB:_OFF_VSB + SB_WIDTH], preferred_element_type=F32)
    _inproj_shared(u, win_ref, cos_ref, sin_ref, *shared_refs)


def _row_spec(width):
    return pl.BlockSpec((ROW_TILE, width), lambda i: (i, 0))


def _shared_outputs(n, d):
    widths_dtypes = [(RET_QK_WIDTH, F32), (RET_QK_WIDTH, F32), (RET_V_WIDTH, BF16),
                     (RET_V_WIDTH, F32), (d, F32), (d, F32)]
    return ([_row_spec(w) for w, _ in widths_dtypes],
            [jax.ShapeDtypeStruct((n, w), dt) for w, dt in widths_dtypes])


def _inproj_prompt_call(h, g_pre, w_in, w_kv_t, cos_tab, sin_tab, batch, seq):
    n, d = h.shape
    assert ROW_TILE == SB_BLOCK
    tiles = seq // ROW_TILE
    tab = pl.BlockSpec((ROW_TILE, RET_QK_DIM), lambda i: (i % tiles, 0))
    t_spec = pl.BlockSpec((1, SB_WIDTH, ROW_TILE), lambda i: (i // tiles, 0, i % tiles))
    shared_specs, shared_shapes = _shared_outputs(n, d)
    return pl.pallas_call(
        _inproj_prompt_kernel,
        grid=(n // ROW_TILE,),
        in_specs=[_row_spec(d), _const_spec(g_pre.shape), _const_spec(w_in.shape),
                  _const_spec(w_kv_t.shape), tab, tab],
        out_specs=[t_spec, t_spec,
                   pl.BlockSpec((1, 1, SB_WIDTH, SB_BLOCK), lambda i: (i // tiles, i % tiles, 0, 0)),
                   _row_spec(SB_WIDTH)] + [_row_spec(SB_WIDTH)] * HEADS_PER_LANE_TILE + shared_specs,
        out_shape=[jax.ShapeDtypeStruct((batch, SB_WIDTH, seq), F32),
                   jax.ShapeDtypeStruct((batch, SB_WIDTH, seq), F32),
                   jax.ShapeDtypeStruct((batch, tiles, SB_WIDTH, SB_BLOCK), BF16),
                   jax.ShapeDtypeStruct((n, SB_WIDTH), BF16)]
                  + [jax.ShapeDtypeStruct((n, SB_WIDTH), BF16)] * HEADS_PER_LANE_TILE + shared_shapes,
        compiler_params=_params(1),
        name="in_projection_prompt",
    )(h, g_pre, w_in, w_kv_t, cos_tab, sin_tab)


def _inproj_decode_call(h, g_pre, w_in, cos_tab, sin_tab):
    n, d = h.shape
    tab = pl.BlockSpec((ROW_TILE, RET_QK_DIM), lambda i: (0, 0))
    shared_specs, shared_shapes = _shared_outputs(n, d)
    return pl.pallas_call(
        _inproj_decode_kernel,
        grid=(n // ROW_TILE,),
        in_specs=[_row_spec(d), _const_spec(g_pre.shape), _const_spec(w_in.shape), tab, tab],
        out_specs=[_row_spec(SB_WIDTH)] * 3 + shared_specs,
        out_shape=[jax.ShapeDtypeStruct((n, SB_WIDTH), F32)] * 3 + shared_shapes,
        compiler_params=_params(1),
        name="in_projection_decode",
    )(h, g_pre, w_in, cos_tab, sin_tab)


NULL_LOGIT = -1e30


def _softplus2(z2):
    neg_abs = pltpu.bitcast(pltpu.bitcast(z2, jnp.uint32) | jnp.uint32(0x80000000), F32)
    return jnp.maximum(z2, 0.0) + jnp.log(1.0 + jnp.exp2(neg_abs)) * LOG2E


def _sb_weights(z2, tri2, run2, mask):
    if mask is not None:
        z2 = jnp.where(mask, z2, NULL_LOGIT)
    sp2 = _softplus2(z2)
    hi = sp2.astype(BF16)
    lo = (sp2 - hi.astype(F32)).astype(BF16)
    csum = jnp.dot(jnp.concatenate([hi, lo], axis=1), tri2, preferred_element_type=F32) + run2
    return jnp.exp2(z2 - csum).astype(BF16), run2 + jnp.sum(sp2, axis=-1, keepdims=True)


_ITEM_QI, _ITEM_J, _ITEM_BIAS, _ITEM_FIRST, _ITEM_FIELDS = 0, 1, 2, 3, 4
_BIAS_FULL, _BIAS_DIAG, _BIAS_NULL = 0, 1, 2
_PIPE_DEPTH = 4


def _sb_items(n_blocks):
    pad = _PIPE_DEPTH - 1
    null = (0, 0, _BIAS_NULL, 1)
    items = [null] * pad
    for qi in range(n_blocks):
        for j in range(qi, -1, -1):
            items.append((qi, j, _BIAS_DIAG if j == qi else _BIAS_FULL, int(j == qi)))
    items += [null] * pad
    return jnp.asarray(items, jnp.int32).T.reshape(-1), len(items)


def _sb_prompt_kernel(items_ref, bias_ref, *refs, n_items):
    slots = HEADS_PER_LANE_TILE
    q_refs, (kt_ref, v_ref, tri_ref, o_ref) = refs[:slots], refs[slots:slots + 4]
    bsel_scr, zraw_scr, z_scr, sp_scr, c_scr, acc_scr, run_scr = refs[slots + 4:]
    nh = acc_scr.shape[0]
    hg = pl.program_id(1)
    blk = tri_ref.shape[0]
    tri = tri_ref[...]
    lane_head = lax.broadcasted_iota(jnp.int32, (1, LANES), 1) // SB_HEAD_DIM
    row = lax.broadcasted_iota(jnp.int32, (blk, blk), 0)
    col = lax.broadcasted_iota(jnp.int32, (blk, blk), 1)
    heads = range(nh)

    def item(field, i):
        return items_ref[field * n_items + i]

    for hh in heads:
        bias2 = bias_ref[hg * nh + hh] * LOG2E
        bsel_scr[hh, _BIAS_FULL] = jnp.full((blk, blk), bias2, F32)
        bsel_scr[hh, _BIAS_DIAG] = jnp.where(col < row, bias2, NULL_LOGIT)
        bsel_scr[hh, _BIAS_NULL] = jnp.full((blk, blk), NULL_LOGIT, F32)
        zraw_scr[hh] = jnp.full((blk, blk), NULL_LOGIT, F32)
        z_scr[0, hh] = jnp.full((blk, blk), NULL_LOGIT, F32)
        z_scr[1, hh] = jnp.full((blk, blk), NULL_LOGIT, F32)
        sp_scr[hh] = jnp.zeros((blk, blk), BF16)
        c_scr[hh] = jnp.zeros((blk, blk), F32)
        acc_scr[hh] = jnp.zeros((blk, LANES), F32)
        run_scr[hh] = jnp.zeros((blk, 1), F32)

    def trip(t, _):
        slot = t % 2
        keep_w = 1.0 - item(_ITEM_FIRST, t).astype(F32)
        keep_s = 1.0 - item(_ITEM_FIRST, t + 2).astype(F32)
        j_l = item(_ITEM_J, t + 3)
        bias_l = item(_ITEM_BIAS, t + 3)
        v_rows = pl.ds(pl.multiple_of(item(_ITEM_J, t) * blk, blk), blk)
        q_rows = pl.ds(pl.multiple_of(item(_ITEM_QI, t + 3) * blk, blk), blk)
        o_rows = pl.ds(pl.multiple_of(item(_ITEM_QI, t) * blk, blk), blk)
        for lt in range(nh // slots):
            hs = slice(lt * slots, (lt + 1) * slots)
            lanes = slice(lt * LANES, (lt + 1) * LANES)
            a = jnp.exp2(z_scr[slot, hs] - c_scr[hs]).astype(BF16).reshape(slots * blk, blk)
            av = jnp.dot(a, v_ref[0, v_rows, lanes], preferred_element_type=F32).reshape(slots, blk, LANES)
            acc = acc_scr[hs] * keep_w + av
            acc_scr[hs] = acc
            out = acc[0]
            for s in range(1, slots):
                out = jnp.where(lane_head == s, acc[s], out)
            o_ref[0, o_rows, lanes] = out.astype(o_ref.dtype)
            c_scr[hs] = jnp.dot(sp_scr[hs].reshape(slots * blk, blk), tri,
                                preferred_element_type=F32).reshape(slots, blk, blk)
            z2 = zraw_scr[hs]
            sp2 = _softplus2(z2)
            run2 = run_scr[hs] * keep_s
            z_scr[slot, hs] = z2 - run2
            sp_scr[hs] = sp2.astype(BF16)
            run_scr[hs] = run2 + jnp.sum(sp2, axis=-1, keepdims=True)
            q = jnp.concatenate([q_refs[s][0, q_rows, lanes] for s in range(slots)], axis=0)
            z_new = jnp.dot(q, kt_ref[0, j_l, lanes, :], preferred_element_type=F32)
            zraw_scr[hs] = z_new.reshape(slots, blk, blk) + bsel_scr[hs, bias_l]
        return 0

    lax.fori_loop(0, n_items - (_PIPE_DEPTH - 1), trip, 0)


def _sb_prompt_call(q_slots, kt_bf, v_bf, bias, tri):
    b, t, w = v_bf.shape
    blk = SB_BLOCK
    nh = SB_HEADS_PER_STEP
    gw = nh * SB_HEAD_DIM
    items, n_items = _sb_items(t // blk)
    seq_spec = pl.BlockSpec((1, t, gw), lambda bi, hg, it: (bi, 0, hg))
    grid_spec = pltpu.PrefetchScalarGridSpec(
        num_scalar_prefetch=1,
        grid=(b, w // gw),
        in_specs=([pl.BlockSpec(memory_space=pltpu.SMEM)] + [seq_spec] * HEADS_PER_LANE_TILE
                  + [pl.BlockSpec((1, t // blk, gw, blk), lambda bi, hg, it: (bi, 0, hg, 0)),
                     seq_spec,
                     pl.BlockSpec(tri.shape, lambda bi, hg, it: (0, 0))]),
        out_specs=seq_spec,
        scratch_shapes=[pltpu.VMEM((nh, 3, blk, blk), F32), pltpu.VMEM((nh, blk, blk), F32),
                        pltpu.VMEM((2, nh, blk, blk), F32), pltpu.VMEM((nh, blk, blk), BF16),
                        pltpu.VMEM((nh, blk, blk), F32), pltpu.VMEM((nh, blk, LANES), F32),
                        pltpu.VMEM((nh, blk, 1), F32)],
    )
    return pl.pallas_call(
        functools.partial(_sb_prompt_kernel, n_items=n_items),
        grid_spec=grid_spec,
        out_shape=jax.ShapeDtypeStruct((b, t, w), BF16),
        compiler_params=_params(2),
        name="sb_attention_prompt",
    )(items, bias, *q_slots, kt_bf, v_bf, tri)


def _sb_decode_kernel(pt_ref, q_ref, kn_ref, vn_ref, bias_ref, tri_ref, *rest, pages):
    del pt_ref
    k_refs = rest[:pages]
    v_refs = rest[pages:2 * pages]
    o_ref = rest[2 * pages]
    qbd_ref, run_ref, acc_ref = rest[2 * pages + 1:]
    g = pl.program_id(1)
    dec_t = q_ref.shape[1]
    rows = SB_HEADS * dec_t
    blk = tri_ref.shape[1]
    bias = bias_ref[...] * LOG2E
    tri = tri_ref[...]
    lane_head = lax.broadcasted_iota(jnp.int32, (1, SB_WIDTH), 1) // SB_HEAD_DIM

    @pl.when(g == 0)
    def _():
        q = q_ref[0] * LOG2E
        qbd = jnp.concatenate([jnp.where(lane_head == hh, q, 0.0) for hh in range(SB_HEADS)], axis=0)
        qbd_ref[...] = qbd.astype(BF16)
        pad = jnp.zeros((blk - dec_t, SB_WIDTH), F32)
        k_new = jnp.concatenate([kn_ref[0], pad], axis=0).astype(BF16)
        v_new = jnp.concatenate([vn_ref[0], pad], axis=0).astype(BF16)
        t_row = lax.broadcasted_iota(jnp.int32, (rows, blk), 0) % dec_t
        col = lax.broadcasted_iota(jnp.int32, (rows, blk), 1)
        z = lax.dot_general(qbd_ref[...], k_new, _NT, preferred_element_type=F32) + bias
        a, run = _sb_weights(z, tri, jnp.zeros((rows, 1), F32), col < t_row)
        run_ref[...] = jnp.broadcast_to(run, run_ref.shape)
        acc_ref[...] = jnp.dot(a, v_new, preferred_element_type=F32)

    qbd = qbd_ref[...]
    run = run_ref[:, 0:1]
    acc = acc_ref[...]
    pages_per_blk = blk // k_refs[0].shape[2]
    groups = [slice(p * pages_per_blk, (p + 1) * pages_per_blk)
              for p in reversed(range(pages // pages_per_blk))]
    zs = [jnp.dot(qbd, jnp.concatenate([r[0] for r in k_refs[sl]], axis=1).astype(BF16),
                  preferred_element_type=F32) + bias for sl in groups]
    sps = [_softplus2(z) for z in zs]
    halves = []
    for sp2 in sps:
        hi = sp2.astype(BF16)
        halves.append(jnp.concatenate([hi, (sp2 - hi.astype(F32)).astype(BF16)], axis=1))
    csums = [jnp.dot(h, tri, preferred_element_type=F32) for h in halves]
    for z, sp2, csum, sl in zip(zs, sps, csums, groups):
        a = jnp.exp2(z - csum - run).astype(BF16)
        v_t = jnp.concatenate([r[0] for r in v_refs[sl]], axis=1).astype(BF16)
        acc = acc + lax.dot_general(a, v_t, _NT, preferred_element_type=F32)
        run = run + jnp.sum(sp2, axis=-1, keepdims=True)
    run_ref[...] = jnp.broadcast_to(run, run_ref.shape)
    acc_ref[...] = acc

    @pl.when(g == pl.num_programs(1) - 1)
    def _():
        out = jnp.zeros((dec_t, SB_WIDTH), F32)
        for hh in range(SB_HEADS):
            out = jnp.where(lane_head == hh, acc[hh * dec_t:(hh + 1) * dec_t, :], out)
        o_ref[0] = out


def _sb_decode_call(q, k_new, v_new, cache_kt, cache_vt, page_table, bias_rows, tri):
    n_seq, dec_t, w = q.shape
    n_pages = page_table.shape[1]
    pages = DEC_PAGES_PER_STEP
    page = cache_kt.shape[2]
    rows = SB_HEADS * dec_t
    seq_spec = pl.BlockSpec((1, dec_t, w), lambda s, g, pt: (s, 0, 0))

    def page_spec(i):
        return pl.BlockSpec((1, w, page), lambda s, g, pt: (pt[s, n_pages - (g + 1) * pages + i], 0, 0))

    def const(shape):
        nd = len(shape)
        return pl.BlockSpec(shape, lambda s, g, pt: (0,) * nd)

    grid_spec = pltpu.PrefetchScalarGridSpec(
        num_scalar_prefetch=1,
        grid=(n_seq, n_pages // pages),
        in_specs=([seq_spec, seq_spec, seq_spec, const(bias_rows.shape), const(tri.shape)]
                  + [page_spec(i) for i in range(pages)] * 2),
        out_specs=seq_spec,
        scratch_shapes=[pltpu.VMEM((rows, w), BF16), pltpu.VMEM((rows, LANES), F32),
                        pltpu.VMEM((rows, w), F32)],
    )
    return pl.pallas_call(
        functools.partial(_sb_decode_kernel, pages=pages),
        grid_spec=grid_spec,
        out_shape=jax.ShapeDtypeStruct((n_seq, dec_t, w), F32),
        compiler_params=_params(2),
        name="sb_attention_decode",
    )(page_table, q, k_new, v_new, bias_rows, tri, *([cache_kt] * pages), *([cache_vt] * pages))


def _head_norm_gate(o, gn, gate):
    o = o * lax.rsqrt(jnp.mean(o * o, axis=-1, keepdims=True) + NORM_EPS) * gn
    return (_silu(gate) * o).astype(BF16)


def _ret_prompt_kernel(dc_ref, q_ref, k_ref, v_ref, g_ref, gn_ref, din_ref, dq_ref, dk_ref,
                       o_ref, s_out_ref, s_ref):
    hh = pl.program_id(1)
    c = pl.program_id(2)

    @pl.when(c == 0)
    def _():
        s_ref[...] = jnp.zeros_like(s_ref)

    dc = dc_ref[hh]
    din = din_ref[0]
    dq = dq_ref[0]
    dk = dk_ref[0]
    gn = gn_ref[...]
    chunks = [slice(i * RET_CHUNK, (i + 1) * RET_CHUNK) for i in range(q_ref.shape[1] // RET_CHUNK)]
    qs = [q_ref[0, sl, :].astype(BF16) for sl in chunks]
    ks = [k_ref[0, sl, :] for sl in chunks]
    vs = [v_ref[0, sl, :] for sl in chunks]
    inners = [lax.dot_general(q, k.astype(BF16), _NT, preferred_element_type=F32) * din
              for q, k in zip(qs, ks)]
    gains = [jnp.dot((k * dk).T.astype(BF16), v, preferred_element_type=F32) for k, v in zip(ks, vs)]
    intras = [jnp.dot(inner.astype(BF16), v, preferred_element_type=F32) for inner, v in zip(inners, vs)]
    states = [s_ref[...]]
    for gain in gains:
        states.append(states[-1] * dc + gain)
    for sl, q, intra, state in zip(chunks, qs, intras, states):
        o = intra + jnp.dot(q, state.astype(BF16), preferred_element_type=F32) * dq
        o_ref[0, sl, :] = _head_norm_gate(o, gn, g_ref[0, sl, :])
    state = states[-1]
    s_ref[...] = state

    @pl.when(c == pl.num_programs(2) - 1)
    def _():
        s_out_ref[0, 0] = state


def _ret_prompt_call(q, k, v_bf, gate, gn, tables):
    b, t, _ = q.shape
    din, dq, dk, dc = tables
    rows = RET_ROWS_PER_STEP
    qk_spec = pl.BlockSpec((1, rows, RET_QK_DIM), lambda bi, h, c: (bi, c, h))
    v_spec = pl.BlockSpec((1, rows, RET_V_DIM), lambda bi, h, c: (bi, c, h))

    def head(shape):
        return pl.BlockSpec((1,) + shape, lambda bi, h, c: (h, 0, 0))

    return pl.pallas_call(
        _ret_prompt_kernel,
        grid=(b, RET_HEADS, t // rows),
        in_specs=[pl.BlockSpec(memory_space=pltpu.SMEM), qk_spec, qk_spec, v_spec, v_spec,
                  pl.BlockSpec((1, RET_V_DIM), lambda bi, h, c: (0, h)),
                  head((RET_CHUNK, RET_CHUNK)), head((RET_CHUNK, RET_V_DIM)),
                  head((RET_CHUNK, RET_QK_DIM))],
        out_specs=[v_spec,
                   pl.BlockSpec((1, 1, RET_QK_DIM, RET_V_DIM), lambda bi, h, c: (bi, h, 0, 0))],
        out_shape=[jax.ShapeDtypeStruct((b, t, RET_V_WIDTH), BF16),
                   jax.ShapeDtypeStruct((b, RET_HEADS, RET_QK_DIM, RET_V_DIM), F32)],
        scratch_shapes=[pltpu.VMEM((RET_QK_DIM, RET_V_DIM), F32)],
        compiler_params=_params(3),
        name="retention_prompt",
    )(dc, q, k, v_bf, gate, gn, din, dq, dk)


def _ret_decode_kernel(dc_ref, q_ref, k_ref, v_ref, g_ref, gn_ref, s_in_ref, din_ref, dq_ref, dk_ref,
                       o_ref, s_out_ref, *, dec_t):
    hh = pl.program_id(0)
    n_seq = s_in_ref.shape[0]
    rows = n_seq * dec_t
    q = q_ref[...]
    k = k_ref[...]
    v = v_ref[...]
    q_bf = q.astype(BF16)
    inner = lax.dot_general(q_bf, k.astype(BF16), _NT, preferred_element_type=F32) * din_ref[0]
    o = jnp.dot(inner.astype(BF16), v, preferred_element_type=F32)

    s_old = s_in_ref[:, 0].reshape(n_seq * RET_QK_DIM, RET_V_DIM)
    row_seq = lax.broadcasted_iota(jnp.int32, (rows, RET_QK_DIM), 0) // dec_t
    q_bd = jnp.concatenate([jnp.where(row_seq == s, q, 0.0) for s in range(n_seq)], axis=1)
    o = o + jnp.dot(q_bd.astype(BF16), s_old.astype(BF16), preferred_element_type=F32) * dq_ref[0]
    o_ref[...] = _head_norm_gate(o, gn_ref[...], g_ref[...])

    kd_t = (k * dk_ref[0]).T
    col_seq = lax.broadcasted_iota(jnp.int32, (RET_QK_DIM, rows), 1) // dec_t
    k_bd_t = jnp.concatenate([jnp.where(col_seq == s, kd_t, 0.0) for s in range(n_seq)], axis=0)
    s_new = s_old * dc_ref[hh] + jnp.dot(k_bd_t.astype(BF16), v, preferred_element_type=F32)
    s_out_ref[:, 0] = s_new.reshape(n_seq, RET_QK_DIM, RET_V_DIM)


def _ret_decode_call(q, k, v_bf, gate, gn, state, tables, dec_t):
    rows = q.shape[0]
    n_seq = state.shape[0]
    din, dq, dk, dc = tables
    qk_spec = pl.BlockSpec((rows, RET_QK_DIM), lambda h: (0, h))
    v_spec = pl.BlockSpec((rows, RET_V_DIM), lambda h: (0, h))
    s_spec = pl.BlockSpec((n_seq, 1, RET_QK_DIM, RET_V_DIM), lambda h: (0, h, 0, 0))

    def head(shape):
        return pl.BlockSpec((1,) + shape, lambda h: (h, 0, 0))

    return pl.pallas_call(
        functools.partial(_ret_decode_kernel, dec_t=dec_t),
        grid=(RET_HEADS,),
        in_specs=[pl.BlockSpec(memory_space=pltpu.SMEM), qk_spec, qk_spec, v_spec, v_spec,
                  pl.BlockSpec((1, RET_V_DIM), lambda h: (0, h)), s_spec,
                  head((rows, rows)), head((rows, RET_V_DIM)), head((rows, RET_QK_DIM))],
        out_specs=[v_spec, s_spec],
        out_shape=[jax.ShapeDtypeStruct((rows, RET_V_WIDTH), BF16),
                   jax.ShapeDtypeStruct(state.shape, F32)],
        compiler_params=_params(1),
        name="retention_decode",
    )(dc, q, k, v_bf, gate, gn, state, din, dq, dk)


def _post_kernel(h_ref, osb_ref, or_ref, asb_ref, ar_ref, wsb_ref, wret_ref, wo_ref, gmix_ref,
                 gpre_ref, wgu_ref, wdown_ref, gpost_ref, o_ref, h_scr):
    @pl.when(pl.program_id(0) == 0)
    def _():
        h_scr[...] = jnp.zeros_like(h_scr)

    h_prev = h_scr[...]
    d_ff = wdown_ref.shape[0]
    xn = _rms(h_prev, gpre_ref[...]).astype(BF16)
    sb = jnp.dot(osb_ref[...], wsb_ref[...], preferred_element_type=F32)
    ret = jnp.dot(or_ref[...], wret_ref[...], preferred_element_type=F32)
    gate = jnp.dot(xn, wgu_ref[:, :d_ff], preferred_element_type=F32)
    m = jax.nn.sigmoid(asb_ref[...]) * sb + jax.nn.sigmoid(ar_ref[...]) * ret
    mix = jnp.dot(m.astype(BF16), wo_ref[...], preferred_element_type=F32)
    up = jnp.dot(xn, wgu_ref[:, d_ff:], preferred_element_type=F32)
    act = (_silu(gate) * up).astype(BF16)
    y = jnp.dot(act, wdown_ref[...], preferred_element_type=F32)
    o_ref[...] = h_prev + 0.5 * _rms(y, gpost_ref[...])
    h_scr[...] = h_ref[...] + _rms(mix, gmix_ref[...])


def _post_call(h, o_sb, o_r, a_sb, a_r, w_sb, w_ret, w_o, g_mix, g_pre, wgu, wdown, g_post):
    n, d = h.shape
    tiles = n // ROW_TILE
    consts = [w_sb, w_ret, w_o, g_mix, g_pre, wgu, wdown, g_post]

    def merge_rows(width):
        return pl.BlockSpec((ROW_TILE, width), lambda s: (jnp.minimum(s, tiles - 1), 0))

    return pl.pallas_call(
        _post_kernel,
        grid=(tiles + 1,),
        in_specs=[merge_rows(d), merge_rows(SB_WIDTH), merge_rows(RET_V_WIDTH), merge_rows(d), merge_rows(d)]
                 + [_const_spec(c.shape) for c in consts],
        out_specs=pl.BlockSpec((ROW_TILE, d), lambda s: (jnp.maximum(s - 1, 0), 0)),
        out_shape=jax.ShapeDtypeStruct((n, d), F32),
        scratch_shapes=[pltpu.VMEM((ROW_TILE, d), F32)],
        compiler_params=_params(1),
        name="merge_out_ffn",
    )(h, o_sb, o_r, a_sb, a_r, *consts)


def _rope_tables(pos):
    half = RET_QK_DIM // 2
    freq = ROPE_BASE ** (-np.arange(half, dtype=np.float64) / half)
    ang = np.asarray(pos, np.float64)[:, None] * freq[None, :]
    cos, sin = np.cos(ang), np.sin(ang)
    return (jnp.asarray(np.concatenate([cos, cos], axis=1), F32),
            jnp.asarray(np.concatenate([-sin, sin], axis=1), F32))


def _decay_tables(chunk, reps):
    log_gamma = np.log1p(-np.exp2(-5.0 - np.arange(RET_HEADS, dtype=np.float64)))
    idx = np.arange(chunk, dtype=np.float64)
    diff = idx[:, None] - idx[None, :]
    d_in = np.where(diff >= 0, np.exp(log_gamma[:, None, None] * np.maximum(diff, 0.0)), 0.0)
    d_q = np.exp(log_gamma[:, None] * (idx + 1.0))
    d_k = np.exp(log_gamma[:, None] * (chunk - 1.0 - idx))
    d_c = np.exp(log_gamma * chunk)
    if reps > 1:
        seq = np.arange(chunk * reps) // chunk
        d_in = np.where(seq[:, None] == seq[None, :], np.tile(d_in, (1, reps, reps)), 0.0)
        d_q = np.tile(d_q, (1, reps))
        d_k = np.tile(d_k, (1, reps))
    n = chunk * reps
    d_q = np.broadcast_to(d_q[:, :, None], (RET_HEADS, n, RET_V_DIM))
    d_k = np.broadcast_to(d_k[:, :, None], (RET_HEADS, n, RET_QK_DIM))
    return tuple(jnp.asarray(a, F32) for a in (d_in, d_q, d_k, d_c))


def _tri(n):
    i = np.arange(n)
    return jnp.asarray(i[:, None] >= i[None, :], BF16)


def kernel(x_prompt, x_sample, cache_k, cache_v, state_ret, page_table, g_ffn1_pre, w_ffn1_gu, w_ffn1_down, g_ffn1_post, g_mix_pre, w_in, sb_bias, ret_gn_g, w_sb_out, w_ret_out, w_o, g_mix_post, g_ffn2_pre, w_ffn2_gu, w_ffn2_down, g_ffn2_post):
    batch, seq, d = x_prompt.shape
    n_seq, dec_t, _ = x_sample.shape
    depth = w_in.shape[0]
    n_pool = cache_k.shape[1]
    n_pages = page_table.shape[1]
    page = cache_k.shape[2]
    past_len = n_pages * page
    assert HEADS_PER_LANE_TILE == 2
    assert SB_HEADS_PER_STEP % HEADS_PER_LANE_TILE == 0 and SB_HEADS % SB_HEADS_PER_STEP == 0
    assert seq % RET_ROWS_PER_STEP == 0 and seq % SB_BLOCK == 0 and seq % ROW_TILE == 0
    assert (n_seq * dec_t) % ROW_TILE == 0 and ROW_TILE % dec_t == 0
    assert n_pages % DEC_PAGES_PER_STEP == 0 and SB_BLOCK % page == 0
    assert (DEC_PAGES_PER_STEP * page) % SB_BLOCK == 0 and dec_t % 8 == 0 and dec_t <= SB_BLOCK
    assert dec_t % RET_CHUNK != 0

    rope_p = _rope_tables(np.arange(seq))
    rope_s = _rope_tables(past_len + np.arange(ROW_TILE) % dec_t)
    decay_p = _decay_tables(RET_CHUNK, 1)
    decay_s = _decay_tables(dec_t, n_seq)
    tri = _tri(SB_BLOCK)
    tri2 = jnp.concatenate([tri, tri], axis=0)

    hp = x_prompt.reshape(batch * seq, d)
    hs = x_sample.reshape(n_seq * dec_t, d)
    kp_l, vp_l, sp_l, ks_l, vs_l, ss_l = [], [], [], [], [], []
    for l in range(depth):
        g1pre, g1post = g_ffn1_pre[l][None], g_ffn1_post[l][None]
        g2pre, g2post = g_ffn2_pre[l][None], g_ffn2_post[l][None]
        gmpre, gmpost = g_mix_pre[l][None], g_mix_post[l][None]
        gn = ret_gn_g[l][None]
        w1gu, w1down = w_ffn1_gu[l].astype(BF16), w_ffn1_down[l].astype(BF16)
        w2gu, w2down = w_ffn2_gu[l].astype(BF16), w_ffn2_down[l].astype(BF16)
        win = w_in[l].astype(BF16)
        w_kv_t = w_in[l][:, _OFF_KSB:_OFF_QR].T.astype(BF16)
        wsb, wret, wo = w_sb_out[l].astype(BF16), w_ret_out[l].astype(BF16), w_o[l].astype(BF16)
        bias = sb_bias[l].astype(F32)
        bias_rows = jnp.broadcast_to(jnp.repeat(bias, dec_t)[:, None], (SB_HEADS * dec_t, SB_BLOCK))
        ck_t = cache_k[l].transpose(0, 2, 3, 1).reshape(n_pool, SB_WIDTH, page)
        cv_t = cache_v[l].transpose(0, 2, 3, 1).reshape(n_pool, SB_WIDTH, page)

        h1 = _ffn_call(hp, g1pre, w1gu, w1down, g1post)
        (k_t, v_t, kt_bf, v_bf, q_s0, q_s1, q_r, k_r, v_r, g_r, a_sb, a_r) = _inproj_prompt_call(
            h1, gmpre, win, w_kv_t, *rope_p, batch, seq)
        o_sb = _sb_prompt_call([q_s0.reshape(batch, seq, SB_WIDTH), q_s1.reshape(batch, seq, SB_WIDTH)],
                               kt_bf, v_bf.reshape(batch, seq, SB_WIDTH), bias, tri)
        o_r, s_p = _ret_prompt_call(q_r.reshape(batch, seq, -1), k_r.reshape(batch, seq, -1),
                                    v_r.reshape(batch, seq, -1), g_r.reshape(batch, seq, -1),
                                    gn, decay_p)
        hp = _post_call(h1, o_sb.reshape(batch * seq, SB_WIDTH), o_r.reshape(batch * seq, RET_V_WIDTH),
                        a_sb, a_r, wsb, wret, wo, gmpost, g2pre, w2gu, w2down, g2post)
        kp_l.append(k_t.reshape(batch, SB_HEADS, SB_HEAD_DIM, seq).transpose(0, 3, 1, 2))
        vp_l.append(v_t.reshape(batch, SB_HEADS, SB_HEAD_DIM, seq).transpose(0, 3, 1, 2))
        sp_l.append(s_p)

        h1 = _ffn_call(hs, g1pre, w1gu, w1down, g1post)
        (k_sb, v_sb, q_sb, q_r, k_r, v_r, g_r, a_sb, a_r) = _inproj_decode_call(
            h1, gmpre, win, *rope_s)
        o_sb = _sb_decode_call(q_sb.reshape(n_seq, dec_t, SB_WIDTH), k_sb.reshape(n_seq, dec_t, SB_WIDTH),
                               v_sb.reshape(n_seq, dec_t, SB_WIDTH), ck_t, cv_t, page_table, bias_rows, tri2)
        o_r, s_s = _ret_decode_call(q_r, k_r, v_r, g_r, gn, state_ret[l], decay_s, dec_t)
        hs = _post_call(h1, o_sb.reshape(n_seq * dec_t, SB_WIDTH).astype(BF16), o_r, a_sb, a_r,
                        wsb, wret, wo, gmpost, g2pre, w2gu, w2down, g2post)
        ks_l.append(k_sb.reshape(n_seq, dec_t, SB_HEADS, SB_HEAD_DIM))
        vs_l.append(v_sb.reshape(n_seq, dec_t, SB_HEADS, SB_HEAD_DIM))
        ss_l.append(s_s)

    return (hp.reshape(batch, seq, d), hs.reshape(n_seq, dec_t, d),
            jnp.stack(kp_l), jnp.stack(vp_l), jnp.stack(sp_l),
            jnp.stack(ks_l), jnp.stack(vs_l), jnp.stack(ss_l))
```

```python
import functools

import jax
import jax.numpy as jnp
import numpy as np
from jax import lax
from jax.experimental import pallas as pl
from jax.experimental.pallas import tpu as pltpu

F32 = jnp.float32
BF16 = jnp.bfloat16

SB_HEADS = 8
SB_HEAD_DIM = 64
SB_WIDTH = SB_HEADS * SB_HEAD_DIM
RET_HEADS = 4
RET_QK_DIM = 128
RET_V_DIM = 256
RET_QK_WIDTH = RET_HEADS * RET_QK_DIM
RET_V_WIDTH = RET_HEADS * RET_V_DIM
RET_CHUNK = 128
ROPE_BASE = 10000.0
NORM_EPS = 1e-6
LOG2E = 1.4426950408889634

LANES = 128
HEADS_PER_LANE_TILE = LANES // SB_HEAD_DIM

ROW_TILE = 256
FFN_ROW_TILE = 512
SB_BLOCK = 256
SB_HEADS_PER_STEP = 4
DEC_PAGES_PER_CHUNK = 8
DEC_RING_SLOTS = 4
RET_ROWS_PER_STEP = 4096
VMEM_LIMIT = 56 * 1024 * 1024

_NT = (((1,), (1,)), ((), ()))


def _const_spec(shape):
    nd = len(shape)
    return pl.BlockSpec(shape, lambda *_: (0,) * nd, pipeline_mode=pl.Buffered(1))


def _params(n_axes, vmem=VMEM_LIMIT):
    return pltpu.CompilerParams(dimension_semantics=("arbitrary",) * n_axes,
                                vmem_limit_bytes=vmem)


def _rms(x, g):
    ms = jnp.mean(x * x, axis=-1, keepdims=True)
    return x * lax.rsqrt(ms + NORM_EPS) * g


def _silu(x):
    return x * jax.nn.sigmoid(x)


def _ffn_residual(x, g_pre, wgu_ref, wdown_ref, g_post):
    d_ff = wdown_ref.shape[0]
    xn = _rms(x, g_pre).astype(BF16)
    gate = jnp.dot(xn, wgu_ref[:, :d_ff], preferred_element_type=F32)
    up = jnp.dot(xn, wgu_ref[:, d_ff:], preferred_element_type=F32)
    act = (_silu(gate) * up).astype(BF16)
    y = jnp.dot(act, wdown_ref[...], preferred_element_type=F32)
    return x + 0.5 * _rms(y, g_post)


def _ffn_kernel(x_ref, gpre_ref, wgu_ref, wdown_ref, gpost_ref, o_ref):
    o_ref[...] = _ffn_residual(x_ref[...], gpre_ref[...], wgu_ref, wdown_ref, gpost_ref[...])


def _ffn_call(x, g_pre, wgu, wdown, g_post):
    n, d = x.shape
    tile = FFN_ROW_TILE if n % FFN_ROW_TILE == 0 else ROW_TILE
    row = pl.BlockSpec((tile, d), lambda i: (i, 0))
    return pl.pallas_call(
        _ffn_kernel,
        grid=(n // tile,),
        in_specs=[row, _const_spec(g_pre.shape), _const_spec(wgu.shape),
                  _const_spec(wdown.shape), _const_spec(g_post.shape)],
        out_specs=row,
        out_shape=jax.ShapeDtypeStruct((n, d), F32),
        compiler_params=_params(1),
        name="ffn_block",
    )(x, g_pre, wgu, wdown, g_post)


_OFF_QSB = 0
_OFF_KSB = _OFF_QSB + SB_WIDTH
_OFF_VSB = _OFF_KSB + SB_WIDTH
_OFF_QR = _OFF_VSB + SB_WIDTH
_OFF_KR = _OFF_QR + RET_QK_WIDTH
_OFF_VR = _OFF_KR + RET_QK_WIDTH
_OFF_GR = _OFF_VR + RET_V_WIDTH
_OFF_ASB = _OFF_GR + RET_V_WIDTH


def _inproj_shared(u, win_ref, cos_ref, sin_ref, qr_ref, kr_ref, vr_ref, gr_ref, asb_ref, ar_ref):
    def proj(lo, width):
        return jnp.dot(u, win_ref[:, lo:lo + width], preferred_element_type=F32)

    cos = cos_ref[...]
    sin = sin_ref[...]
    q_r = proj(_OFF_QR, RET_QK_WIDTH)
    k_r = proj(_OFF_KR, RET_QK_WIDTH)
    for hh in range(RET_HEADS):
        sl = slice(hh * RET_QK_DIM, (hh + 1) * RET_QK_DIM)
        qh = q_r[:, sl]
        kh = k_r[:, sl]
        qr_ref[:, sl] = qh * cos + pltpu.roll(qh, RET_QK_DIM // 2, axis=1) * sin
        kr_ref[:, sl] = (kh * cos + pltpu.roll(kh, RET_QK_DIM // 2, axis=1) * sin) * (RET_QK_DIM ** -0.5)
    vr_ref[...] = proj(_OFF_VR, RET_V_WIDTH).astype(BF16)
    gr_ref[...] = proj(_OFF_GR, RET_V_WIDTH)
    d_model = asb_ref.shape[1]
    asb_ref[...] = proj(_OFF_ASB, d_model)
    ar_ref[...] = proj(_OFF_ASB + d_model, d_model)


def _inproj_prompt_kernel(h_ref, g_ref, win_ref, wkvt_ref, cos_ref, sin_ref,
                          kt_ref, vt_ref, ktb_ref, vb_ref, *rest):
    q_refs, shared_refs = rest[:HEADS_PER_LANE_TILE], rest[HEADS_PER_LANE_TILE:]
    u = _rms(h_ref[...], g_ref[...]).astype(BF16)
    q = jnp.dot(u, win_ref[:, _OFF_QSB:_OFF_QSB + SB_WIDTH], preferred_element_type=F32)
    q = q * (SB_HEAD_DIM ** -0.5 * LOG2E)
    head_slot = (lax.broadcasted_iota(jnp.int32, (1, SB_WIDTH), 1) // SB_HEAD_DIM) % HEADS_PER_LANE_TILE
    for hh, q_ref in enumerate(q_refs):
        q_ref[...] = jnp.where(head_slot == hh, q, 0.0).astype(BF16)
    kv_t = lax.dot_general(wkvt_ref[...], u, _NT, preferred_element_type=F32)
    kt_ref[0] = kv_t[:SB_WIDTH]
    vt_ref[0] = kv_t[SB_WIDTH:]
    ktb_ref[0, 0] = kv_t[:SB_WIDTH].astype(BF16)
    vb_ref[...] = jnp.dot(u, win_ref[:, _OFF_VSB:_OFF_VSB + SB_WIDTH],
                          preferred_element_type=F32).astype(BF16)
    _inproj_shared(u, win_ref, cos_ref, sin_ref, *shared_refs)


def _inproj_decode_kernel(h_ref, g_ref, win_ref, cos_ref, sin_ref, k_ref, v_ref, q_ref, *shared_refs):
    u = _rms(h_ref[...], g_ref[...]).astype(BF16)
    q_ref[...] = jnp.dot(u, win_ref[:, _OFF_QSB:_OFF_QSB + SB_WIDTH],
                         preferred_element_type=F32) * (SB_HEAD_DIM ** -0.5)
    k_ref[...] = jnp.dot(u, win_ref[:, _OFF_KSB:_OFF_KSB + SB_WIDTH], preferred_element_type=F32)
    v_ref[...] = jnp.dot(u, win_ref[:, _OFF_VSB:_OFF_VSB + SB_WIDTH], preferred_element_type=F32)
    _inproj_shared(u, win_ref, cos_ref, sin_ref, *shared_refs)


def _row_spec(width):
    return pl.BlockSpec((ROW_TILE, width), lambda i: (i, 0))


def _shared_outputs(n, d):
    widths_dtypes = [(RET_QK_WIDTH, F32), (RET_QK_WIDTH, F32), (RET_V_WIDTH, BF16),
                     (RET_V_WIDTH, F32), (d, F32), (d, F32)]
    return ([_row_spec(w) for w, _ in widths_dtypes],
            [jax.ShapeDtypeStruct((n, w), dt) for w, dt in widths_dtypes])


def _inproj_prompt_call(h, g_pre, w_in, w_kv_t, cos_tab, sin_tab, batch, seq):
    n, d = h.shape
    assert ROW_TILE == SB_BLOCK
    tiles = seq // ROW_TILE
    tab = pl.BlockSpec((ROW_TILE, RET_QK_DIM), lambda i: (i % tiles, 0))
    t_spec = pl.BlockSpec((1, SB_WIDTH, ROW_TILE), lambda i: (i // tiles, 0, i % tiles))
    shared_specs, shared_shapes = _shared_outputs(n, d)
    return pl.pallas_call(
        _inproj_prompt_kernel,
        grid=(n // ROW_TILE,),
        in_specs=[_row_spec(d), _const_spec(g_pre.shape), _const_spec(w_in.shape),
                  _const_spec(w_kv_t.shape), tab, tab],
        out_specs=[t_spec, t_spec,
                   pl.BlockSpec((1, 1, SB_WIDTH, SB_BLOCK), lambda i: (i // tiles, i % tiles, 0, 0)),
                   _row_spec(SB_WIDTH)] + [_row_spec(SB_WIDTH)] * HEADS_PER_LANE_TILE + shared_specs,
        out_shape=[jax.ShapeDtypeStruct((batch, SB_WIDTH, seq), F32),
                   jax.ShapeDtypeStruct((batch, SB_WIDTH, seq), F32),
                   jax.ShapeDtypeStruct((batch, tiles, SB_WIDTH, SB_BLOCK), BF16),
                   jax.ShapeDtypeStruct((n, SB_WIDTH), BF16)]
                  + [jax.ShapeDtypeStruct((n, SB_WIDTH), BF16)] * HEADS_PER_LANE_TILE + shared_shapes,
        compiler_params=_params(1),
        name="in_projection_prompt",
    )(h, g_pre, w_in, w_kv_t, cos_tab, sin_tab)


def _inproj_decode_call(h, g_pre, w_in, cos_tab, sin_tab):
    n, d = h.shape
    tab = pl.BlockSpec((ROW_TILE, RET_QK_DIM), lambda i: (0, 0))
    shared_specs, shared_shapes = _shared_outputs(n, d)
    return pl.pallas_call(
        _inproj_decode_kernel,
        grid=(n // ROW_TILE,),
        in_specs=[_row_spec(d), _const_spec(g_pre.shape), _const_spec(w_in.shape), tab, tab],
        out_specs=[_row_spec(SB_WIDTH)] * 3 + shared_specs,
        out_shape=[jax.ShapeDtypeStruct((n, SB_WIDTH), F32)] * 3 + shared_shapes,
        compiler_params=_params(1),
        name="in_projection_decode",
    )(h, g_pre, w_in, cos_tab, sin_tab)


NULL_LOGIT = -1e30


def _softplus2(z2):
    neg_abs = pltpu.bitcast(pltpu.bitcast(z2, jnp.uint32) | jnp.uint32(0x80000000), F32)
    return jnp.maximum(z2, 0.0) + jnp.log(1.0 + jnp.exp2(neg_abs)) * LOG2E


def _sb_weights(z2, tri2, run2, mask):
    if mask is not None:
        z2 = jnp.where(mask, z2, NULL_LOGIT)
    sp2 = _softplus2(z2)
    hi = sp2.astype(BF16)
    lo = (sp2 - hi.astype(F32)).astype(BF16)
    csum = jnp.dot(jnp.concatenate([hi, lo], axis=1), tri2, preferred_element_type=F32) + run2
    return jnp.exp2(z2 - csum).astype(BF16), run2 + jnp.sum(sp2, axis=-1, keepdims=True)


_ITEM_QI, _ITEM_J, _ITEM_BIAS, _ITEM_FIRST, _ITEM_FIELDS = 0, 1, 2, 3, 4
_BIAS_FULL, _BIAS_DIAG, _BIAS_NULL = 0, 1, 2
_PIPE_DEPTH = 4


def _sb_items(n_blocks):
    pad = _PIPE_DEPTH - 1
    null = (0, 0, _BIAS_NULL, 1)
    items = [null] * pad
    for qi in range(n_blocks):
        for j in range(qi, -1, -1):
            items.append((qi, j, _BIAS_DIAG if j == qi else _BIAS_FULL, int(j == qi)))
    items += [null] * pad
    return jnp.asarray(items, jnp.int32).T.reshape(-1), len(items)


def _sb_prompt_kernel(items_ref, bias_ref, *refs, n_items):
    slots = HEADS_PER_LANE_TILE
    q_refs, (kt_ref, v_ref, tri_ref, o_ref) = refs[:slots], refs[slots:slots + 4]
    bsel_scr, zraw_scr, z_scr, sp_scr, c_scr, acc_scr, run_scr = refs[slots + 4:]
    nh = acc_scr.shape[0]
    hg = pl.program_id(1)
    blk = tri_ref.shape[0]
    tri = tri_ref[...]
    lane_head = lax.broadcasted_iota(jnp.int32, (1, LANES), 1) // SB_HEAD_DIM
    row = lax.broadcasted_iota(jnp.int32, (blk, blk), 0)
    col = lax.broadcasted_iota(jnp.int32, (blk, blk), 1)
    heads = range(nh)

    def item(field, i):
        return items_ref[field * n_items + i]

    for hh in heads:
        bias2 = bias_ref[hg * nh + hh] * LOG2E
        bsel_scr[hh, _BIAS_FULL] = jnp.full((blk, blk), bias2, F32)
        bsel_scr[hh, _BIAS_DIAG] = jnp.where(col < row, bias2, NULL_LOGIT)
        bsel_scr[hh, _BIAS_NULL] = jnp.full((blk, blk), NULL_LOGIT, F32)
        zraw_scr[hh] = jnp.full((blk, blk), NULL_LOGIT, F32)
        z_scr[0, hh] = jnp.full((blk, blk), NULL_LOGIT, F32)
        z_scr[1, hh] = jnp.full((blk, blk), NULL_LOGIT, F32)
        sp_scr[hh] = jnp.zeros((blk, blk), BF16)
        c_scr[hh] = jnp.zeros((blk, blk), F32)
        acc_scr[hh] = jnp.zeros((blk, LANES), F32)
        run_scr[hh] = jnp.zeros((blk, 1), F32)

    def trip(t, _):
        slot = t % 2
        keep_w = 1.0 - item(_ITEM_FIRST, t).astype(F32)
        keep_s = 1.0 - item(_ITEM_FIRST, t + 2).astype(F32)
        j_l = item(_ITEM_J, t + 3)
        bias_l = item(_ITEM_BIAS, t + 3)
        v_rows = pl.ds(pl.multiple_of(item(_ITEM_J, t) * blk, blk), blk)
        q_rows = pl.ds(pl.multiple_of(item(_ITEM_QI, t + 3) * blk, blk), blk)
        o_rows = pl.ds(pl.multiple_of(item(_ITEM_QI, t) * blk, blk), blk)
        for lt in range(nh // slots):
            hs = slice(lt * slots, (lt + 1) * slots)
            lanes = slice(lt * LANES, (lt + 1) * LANES)
            a = jnp.exp2(z_scr[slot, hs] - c_scr[hs]).astype(BF16).reshape(slots * blk, blk)
            av = jnp.dot(a, v_ref[0, v_rows, lanes], preferred_element_type=F32).reshape(slots, blk, LANES)
            acc = acc_scr[hs] * keep_w + av
            acc_scr[hs] = acc
            out = acc[0]
            for s in range(1, slots):
                out = jnp.where(lane_head == s, acc[s], out)
            o_ref[0, o_rows, lanes] = out.astype(o_ref.dtype)
            c_scr[hs] = jnp.dot(sp_scr[hs].reshape(slots * blk, blk), tri,
                                preferred_element_type=F32).reshape(slots, blk, blk)
            z2 = zraw_scr[hs]
            sp2 = _softplus2(z2)
            run2 = run_scr[hs] * keep_s
            z_scr[slot, hs] = (z2 - sp2) - run2
            sp_scr[hs] = sp2.astype(BF16)
            run_scr[hs] = run2 + jnp.sum(sp2, axis=-1, keepdims=True)
            q = jnp.concatenate([q_refs[s][0, q_rows, lanes] for s in range(slots)], axis=0)
            z_new = jnp.dot(q, kt_ref[0, j_l, lanes, :], preferred_element_type=F32)
            zraw_scr[hs] = z_new.reshape(slots, blk, blk) + bsel_scr[hs, bias_l]
        return 0

    lax.fori_loop(0, n_items - (_PIPE_DEPTH - 1), trip, 0)


def _sb_prompt_call(q_slots, kt_bf, v_bf, bias, tri):
    b, t, w = v_bf.shape
    blk = SB_BLOCK
    nh = SB_HEADS_PER_STEP
    gw = nh * SB_HEAD_DIM
    items, n_items = _sb_items(t // blk)
    seq_spec = pl.BlockSpec((1, t, gw), lambda bi, hg, it: (bi, 0, hg))
    grid_spec = pltpu.PrefetchScalarGridSpec(
        num_scalar_prefetch=1,
        grid=(b, w // gw),
        in_specs=([pl.BlockSpec(memory_space=pltpu.SMEM)] + [seq_spec] * HEADS_PER_LANE_TILE
                  + [pl.BlockSpec((1, t // blk, gw, blk), lambda bi, hg, it: (bi, 0, hg, 0)),
                     seq_spec,
                     pl.BlockSpec(tri.shape, lambda bi, hg, it: (0, 0))]),
        out_specs=seq_spec,
        scratch_shapes=[pltpu.VMEM((nh, 3, blk, blk), F32), pltpu.VMEM((nh, blk, blk), F32),
                        pltpu.VMEM((2, nh, blk, blk), F32), pltpu.VMEM((nh, blk, blk), BF16),
                        pltpu.VMEM((nh, blk, blk), F32), pltpu.VMEM((nh, blk, LANES), F32),
                        pltpu.VMEM((nh, blk, 1), F32)],
    )
    return pl.pallas_call(
        functools.partial(_sb_prompt_kernel, n_items=n_items),
        grid_spec=grid_spec,
        out_shape=jax.ShapeDtypeStruct((b, t, w), BF16),
        compiler_params=_params(2),
        name="sb_attention_prompt",
    )(items, bias, *q_slots, kt_bf, v_bf, tri)


def _sb_decode_kernel(pt_ref, q_ref, kn_ref, vn_ref, bias_ref, tri_ref, ck_hbm, cv_hbm, o_ref,
                      kbuf, vbuf, sem, *, chunk):
    s = pl.program_id(0)
    n_seq = pl.num_programs(0)
    n_pages = pt_ref.shape[1]
    n_chunks = n_pages // chunk
    dec_t = q_ref.shape[1]
    rows = SB_HEADS * dec_t
    blk = tri_ref.shape[1]
    pages_per_blk = blk // kbuf.shape[3]
    bias = bias_ref[...] * LOG2E
    tri = tri_ref[...]
    lane_head = lax.broadcasted_iota(jnp.int32, (1, SB_WIDTH), 1) // SB_HEAD_DIM

    n_slots = kbuf.shape[0]
    ahead = n_slots - 1
    total = n_seq * n_chunks

    def chunk_copies(g):
        slot = g % n_slots
        gc = jnp.minimum(g, total - 1)
        seq = gc // n_chunks
        base = n_pages - (gc % n_chunks + 1) * chunk
        out = []
        for i in range(chunk):
            page = pt_ref[seq, base + i]
            out.append(pltpu.make_async_copy(ck_hbm.at[page], kbuf.at[slot, i], sem.at[slot, 0]))
            out.append(pltpu.make_async_copy(cv_hbm.at[page], vbuf.at[slot, i], sem.at[slot, 1]))
        return out

    @pl.when(s == 0)
    def _():
        for g in range(ahead):
            for cp in chunk_copies(g):
                cp.start()

    q = q_ref[0] * LOG2E
    qbd = jnp.concatenate([jnp.where(lane_head == hh, q, 0.0) for hh in range(SB_HEADS)],
                          axis=0).astype(BF16)
    pad = jnp.zeros((blk - dec_t, SB_WIDTH), F32)
    k_new = jnp.concatenate([kn_ref[0], pad], axis=0).astype(BF16)
    v_new = jnp.concatenate([vn_ref[0], pad], axis=0).astype(BF16)
    t_row = lax.broadcasted_iota(jnp.int32, (rows, blk), 0) % dec_t
    col = lax.broadcasted_iota(jnp.int32, (rows, blk), 1)
    z = lax.dot_general(qbd, k_new, _NT, preferred_element_type=F32) + bias
    a, run = _sb_weights(z, tri, jnp.zeros((rows, 1), F32), col < t_row)
    acc = jnp.dot(a, v_new, preferred_element_type=F32)

    def visit(c, carry):
        run, acc = carry
        g = s * n_chunks + c
        slot = g % n_slots
        for cp in chunk_copies(g):
            cp.wait()
        for cp in chunk_copies(g + ahead):
            cp.start()
        groups = [slice(p * pages_per_blk, (p + 1) * pages_per_blk)
                  for p in reversed(range(chunk // pages_per_blk))]

        def block_t(buf, sl):
            return jnp.concatenate([buf[slot, i] for i in range(sl.start, sl.stop)], axis=1).astype(BF16)

        zs = [jnp.dot(qbd, block_t(kbuf, sl), preferred_element_type=F32) + bias for sl in groups]
        sps = [_softplus2(z) for z in zs]
        halves = []
        for sp2 in sps:
            hi = sp2.astype(BF16)
            halves.append(jnp.concatenate([hi, (sp2 - hi.astype(F32)).astype(BF16)], axis=1))
        csums = [jnp.dot(h, tri, preferred_element_type=F32) for h in halves]
        for z, sp2, csum, sl in zip(zs, sps, csums, groups):
            a = jnp.exp2(z - csum - run).astype(BF16)
            acc = acc + lax.dot_general(a, block_t(vbuf, sl), _NT, preferred_element_type=F32)
            run = run + jnp.sum(sp2, axis=-1, keepdims=True)
        return run, acc

    run, acc = lax.fori_loop(0, n_chunks, visit, (run, acc))

    @pl.when(s == n_seq - 1)
    def _():
        for g in range(ahead):
            for cp in chunk_copies(total + g):
                cp.wait()

    out = jnp.zeros((dec_t, SB_WIDTH), F32)
    for hh in range(SB_HEADS):
        out = jnp.where(lane_head == hh, acc[hh * dec_t:(hh + 1) * dec_t, :], out)
    o_ref[0] = out


def _sb_decode_call(q, k_new, v_new, cache_kt, cache_vt, page_table, bias_rows, tri):
    n_seq, dec_t, w = q.shape
    n_pages = page_table.shape[1]
    chunk = DEC_PAGES_PER_CHUNK
    page = cache_kt.shape[2]
    assert n_pages % chunk == 0
    slots = DEC_RING_SLOTS
    seq_spec = pl.BlockSpec((1, dec_t, w), lambda s, pt: (s, 0, 0))

    def const(shape):
        nd = len(shape)
        return pl.BlockSpec(shape, lambda s, pt: (0,) * nd)

    hbm = pl.BlockSpec(memory_space=pl.ANY)
    grid_spec = pltpu.PrefetchScalarGridSpec(
        num_scalar_prefetch=1,
        grid=(n_seq,),
        in_specs=[seq_spec, seq_spec, seq_spec, const(bias_rows.shape), const(tri.shape), hbm, hbm],
        out_specs=seq_spec,
        scratch_shapes=[pltpu.VMEM((slots, chunk, w, page), F32), pltpu.VMEM((slots, chunk, w, page), F32),
                        pltpu.SemaphoreType.DMA((slots, 2))],
    )
    return pl.pallas_call(
        functools.partial(_sb_decode_kernel, chunk=chunk),
        grid_spec=grid_spec,
        out_shape=jax.ShapeDtypeStruct((n_seq, dec_t, w), F32),
        compiler_params=_params(1),
        name="sb_attention_decode",
    )(page_table, q, k_new, v_new, bias_rows, tri, cache_kt, cache_vt)


def _head_norm_gate(o, gn, gate):
    o = o * lax.rsqrt(jnp.mean(o * o, axis=-1, keepdims=True) + NORM_EPS) * gn
    return (_silu(gate) * o).astype(BF16)


def _ret_prompt_kernel(dc_ref, q_ref, k_ref, v_ref, g_ref, gn_ref, din_ref, dq_ref, dk_ref,
                       o_ref, s_out_ref, s_ref):
    hh = pl.program_id(1)
    c = pl.program_id(2)

    @pl.when(c == 0)
    def _():
        s_ref[...] = jnp.zeros_like(s_ref)

    dc = dc_ref[hh]
    din = din_ref[0]
    dq = dq_ref[0]
    dk = dk_ref[0]
    gn = gn_ref[...]
    chunks = [slice(i * RET_CHUNK, (i + 1) * RET_CHUNK) for i in range(q_ref.shape[1] // RET_CHUNK)]
    qs = [q_ref[0, sl, :].astype(BF16) for sl in chunks]
    ks = [k_ref[0, sl, :] for sl in chunks]
    vs = [v_ref[0, sl, :] for sl in chunks]
    inners = [lax.dot_general(q, k.astype(BF16), _NT, preferred_element_type=F32) * din
              for q, k in zip(qs, ks)]
    gains = [jnp.dot((k * dk).T.astype(BF16), v, preferred_element_type=F32) for k, v in zip(ks, vs)]
    intras = [jnp.dot(inner.astype(BF16), v, preferred_element_type=F32) for inner, v in zip(inners, vs)]
    states = [s_ref[...]]
    for gain in gains:
        states.append(states[-1] * dc + gain)
    for sl, q, intra, state in zip(chunks, qs, intras, states):
        o = intra + jnp.dot(q, state.astype(BF16), preferred_element_type=F32) * dq
        o_ref[0, sl, :] = _head_norm_gate(o, gn, g_ref[0, sl, :])
    state = states[-1]
    s_ref[...] = state

    @pl.when(c == pl.num_programs(2) - 1)
    def _():
        s_out_ref[0, 0] = state


def _ret_prompt_call(q, k, v_bf, gate, gn, tables):
    b, t, _ = q.shape
    din, dq, dk, dc = tables
    rows = RET_ROWS_PER_STEP
    qk_spec = pl.BlockSpec((1, rows, RET_QK_DIM), lambda bi, h, c: (bi, c, h))
    v_spec = pl.BlockSpec((1, rows, RET_V_DIM), lambda bi, h, c: (bi, c, h))

    def head(shape):
        return pl.BlockSpec((1,) + shape, lambda bi, h, c: (h, 0, 0))

    return pl.pallas_call(
        _ret_prompt_kernel,
        grid=(b, RET_HEADS, t // rows),
        in_specs=[pl.BlockSpec(memory_space=pltpu.SMEM), qk_spec, qk_spec, v_spec, v_spec,
                  pl.BlockSpec((1, RET_V_DIM), lambda bi, h, c: (0, h)),
                  head((RET_CHUNK, RET_CHUNK)), head((RET_CHUNK, RET_V_DIM)),
                  head((RET_CHUNK, RET_QK_DIM))],
        out_specs=[v_spec,
                   pl.BlockSpec((1, 1, RET_QK_DIM, RET_V_DIM), lambda bi, h, c: (bi, h, 0, 0))],
        out_shape=[jax.ShapeDtypeStruct((b, t, RET_V_WIDTH), BF16),
                   jax.ShapeDtypeStruct((b, RET_HEADS, RET_QK_DIM, RET_V_DIM), F32)],
        scratch_shapes=[pltpu.VMEM((RET_QK_DIM, RET_V_DIM), F32)],
        compiler_params=_params(3),
        name="retention_prompt",
    )(dc, q, k, v_bf, gate, gn, din, dq, dk)


def _ret_decode_kernel(dc_ref, q_ref, k_ref, v_ref, g_ref, gn_ref, s_in_ref, din_ref, dq_ref, dk_ref,
                       o_ref, s_out_ref, *, dec_t):
    hh = pl.program_id(0)
    n_seq = s_in_ref.shape[0]
    rows = n_seq * dec_t
    q = q_ref[...]
    k = k_ref[...]
    v = v_ref[...]
    q_bf = q.astype(BF16)
    inner = lax.dot_general(q_bf, k.astype(BF16), _NT, preferred_element_type=F32) * din_ref[0]
    o = jnp.dot(inner.astype(BF16), v, preferred_element_type=F32)

    s_old = s_in_ref[:, 0].reshape(n_seq * RET_QK_DIM, RET_V_DIM)
    row_seq = lax.broadcasted_iota(jnp.int32, (rows, RET_QK_DIM), 0) // dec_t
    q_bd = jnp.concatenate([jnp.where(row_seq == s, q, 0.0) for s in range(n_seq)], axis=1)
    o = o + jnp.dot(q_bd.astype(BF16), s_old.astype(BF16), preferred_element_type=F32) * dq_ref[0]
    o_ref[...] = _head_norm_gate(o, gn_ref[...], g_ref[...])

    kd_t = (k * dk_ref[0]).T
    col_seq = lax.broadcasted_iota(jnp.int32, (RET_QK_DIM, rows), 1) // dec_t
    k_bd_t = jnp.concatenate([jnp.where(col_seq == s, kd_t, 0.0) for s in range(n_seq)], axis=0)
    s_new = s_old * dc_ref[hh] + jnp.dot(k_bd_t.astype(BF16), v, preferred_element_type=F32)
    s_out_ref[:, 0] = s_new.reshape(n_seq, RET_QK_DIM, RET_V_DIM)


def _ret_decode_call(q, k, v_bf, gate, gn, state, tables, dec_t):
    rows = q.shape[0]
    n_seq = state.shape[0]
    din, dq, dk, dc = tables
    qk_spec = pl.BlockSpec((rows, RET_QK_DIM), lambda h: (0, h))
    v_spec = pl.BlockSpec((rows, RET_V_DIM), lambda h: (0, h))
    s_spec = pl.BlockSpec((n_seq, 1, RET_QK_DIM, RET_V_DIM), lambda h: (0, h, 0, 0))

    def head(shape):
        return pl.BlockSpec((1,) + shape, lambda h: (h, 0, 0))

    return pl.pallas_call(
        functools.partial(_ret_decode_kernel, dec_t=dec_t),
        grid=(RET_HEADS,),
        in_specs=[pl.BlockSpec(memory_space=pltpu.SMEM), qk_spec, qk_spec, v_spec, v_spec,
                  pl.BlockSpec((1, RET_V_DIM), lambda h: (0, h)), s_spec,
                  head((rows, rows)), head((rows, RET_V_DIM)), head((rows, RET_QK_DIM))],
        out_specs=[v_spec, s_spec],
        out_shape=[jax.ShapeDtypeStruct((rows, RET_V_WIDTH), BF16),
                   jax.ShapeDtypeStruct(state.shape, F32)],
        compiler_params=_params(1),
        name="retention_decode",
    )(dc, q, k, v_bf, gate, gn, state, din, dq, dk)


def _post_kernel(h_ref, osb_ref, or_ref, asb_ref, ar_ref, wsb_ref, wret_ref, wo_ref, gmix_ref,
                 gpre_ref, wgu_ref, wdown_ref, gpost_ref, o_ref, h_scr):
    @pl.when(pl.program_id(0) == 0)
    def _():
        h_scr[...] = jnp.zeros_like(h_scr)

    h_prev = h_scr[...]
    d_ff = wdown_ref.shape[0]
    xn = _rms(h_prev, gpre_ref[...]).astype(BF16)
    sb = jnp.dot(osb_ref[...], wsb_ref[...], preferred_element_type=F32)
    ret = jnp.dot(or_ref[...], wret_ref[...], preferred_element_type=F32)
    gate = jnp.dot(xn, wgu_ref[:, :d_ff], preferred_element_type=F32)
    m = jax.nn.sigmoid(asb_ref[...]) * sb + jax.nn.sigmoid(ar_ref[...]) * ret
    mix = jnp.dot(m.astype(BF16), wo_ref[...], preferred_element_type=F32)
    up = jnp.dot(xn, wgu_ref[:, d_ff:], preferred_element_type=F32)
    act = (_silu(gate) * up).astype(BF16)
    y = jnp.dot(act, wdown_ref[...], preferred_element_type=F32)
    o_ref[...] = h_prev + 0.5 * _rms(y, gpost_ref[...])
    h_scr[...] = h_ref[...] + _rms(mix, gmix_ref[...])


def _post_call(h, o_sb, o_r, a_sb, a_r, w_sb, w_ret, w_o, g_mix, g_pre, wgu, wdown, g_post):
    n, d = h.shape
    tiles = n // ROW_TILE
    consts = [w_sb, w_ret, w_o, g_mix, g_pre, wgu, wdown, g_post]

    def merge_rows(width):
        return pl.BlockSpec((ROW_TILE, width), lambda s: (jnp.minimum(s, tiles - 1), 0))

    return pl.pallas_call(
        _post_kernel,
        grid=(tiles + 1,),
        in_specs=[merge_rows(d), merge_rows(SB_WIDTH), merge_rows(RET_V_WIDTH), merge_rows(d), merge_rows(d)]
                 + [_const_spec(c.shape) for c in consts],
        out_specs=pl.BlockSpec((ROW_TILE, d), lambda s: (jnp.maximum(s - 1, 0), 0)),
        out_shape=jax.ShapeDtypeStruct((n, d), F32),
        scratch_shapes=[pltpu.VMEM((ROW_TILE, d), F32)],
        compiler_params=_params(1),
        name="merge_out_ffn",
    )(h, o_sb, o_r, a_sb, a_r, *consts)


def _rope_tables(pos):
    half = RET_QK_DIM // 2
    freq = ROPE_BASE ** (-np.arange(half, dtype=np.float64) / half)
    ang = np.asarray(pos, np.float64)[:, None] * freq[None, :]
    cos, sin = np.cos(ang), np.sin(ang)
    return (jnp.asarray(np.concatenate([cos, cos], axis=1), F32),
            jnp.asarray(np.concatenate([-sin, sin], axis=1), F32))


def _decay_tables(chunk, reps):
    log_gamma = np.log1p(-np.exp2(-5.0 - np.arange(RET_HEADS, dtype=np.float64)))
    idx = np.arange(chunk, dtype=np.float64)
    diff = idx[:, None] - idx[None, :]
    d_in = np.where(diff >= 0, np.exp(log_gamma[:, None, None] * np.maximum(diff, 0.0)), 0.0)
    d_q = np.exp(log_gamma[:, None] * (idx + 1.0))
    d_k = np.exp(log_gamma[:, None] * (chunk - 1.0 - idx))
    d_c = np.exp(log_gamma * chunk)
    if reps > 1:
        seq = np.arange(chunk * reps) // chunk
        d_in = np.where(seq[:, None] == seq[None, :], np.tile(d_in, (1, reps, reps)), 0.0)
        d_q = np.tile(d_q, (1, reps))
        d_k = np.tile(d_k, (1, reps))
    n = chunk * reps
    d_q = np.broadcast_to(d_q[:, :, None], (RET_HEADS, n, RET_V_DIM))
    d_k = np.broadcast_to(d_k[:, :, None], (RET_HEADS, n, RET_QK_DIM))
    return tuple(jnp.asarray(a, F32) for a in (d_in, d_q, d_k, d_c))


def _tri(n, strict):
    i = np.arange(n)
    return jnp.asarray(i[:, None] > i[None, :] if strict else i[:, None] >= i[None, :], BF16)


def kernel(x_prompt, x_sample, cache_k, cache_v, state_ret, page_table, g_ffn1_pre, w_ffn1_gu, w_ffn1_down, g_ffn1_post, g_mix_pre, w_in, sb_bias, ret_gn_g, w_sb_out, w_ret_out, w_o, g_mix_post, g_ffn2_pre, w_ffn2_gu, w_ffn2_down, g_ffn2_post):
    batch, seq, d = x_prompt.shape
    n_seq, dec_t, _ = x_sample.shape
    depth = w_in.shape[0]
    n_pool = cache_k.shape[1]
    n_pages = page_table.shape[1]
    page = cache_k.shape[2]
    past_len = n_pages * page
    assert HEADS_PER_LANE_TILE == 2
    assert SB_HEADS_PER_STEP % HEADS_PER_LANE_TILE == 0 and SB_HEADS % SB_HEADS_PER_STEP == 0
    assert seq % RET_ROWS_PER_STEP == 0 and seq % SB_BLOCK == 0 and seq % ROW_TILE == 0
    assert (n_seq * dec_t) % ROW_TILE == 0 and ROW_TILE % dec_t == 0
    assert SB_BLOCK % page == 0
    assert (DEC_PAGES_PER_CHUNK * page) % SB_BLOCK == 0 and dec_t % 8 == 0 and dec_t <= SB_BLOCK
    assert dec_t % RET_CHUNK != 0

    rope_p = _rope_tables(np.arange(seq))
    rope_s = _rope_tables(past_len + np.arange(ROW_TILE) % dec_t)
    decay_p = _decay_tables(RET_CHUNK, 1)
    decay_s = _decay_tables(dec_t, n_seq)
    tri_strict = _tri(SB_BLOCK, strict=True)
    tri = _tri(SB_BLOCK, strict=False)
    tri2 = jnp.concatenate([tri, tri], axis=0)

    hp = x_prompt.reshape(batch * seq, d)
    hs = x_sample.reshape(n_seq * dec_t, d)
    kp_l, vp_l, sp_l, ks_l, vs_l, ss_l = [], [], [], [], [], []
    for l in range(depth):
        g1pre, g1post = g_ffn1_pre[l][None], g_ffn1_post[l][None]
        g2pre, g2post = g_ffn2_pre[l][None], g_ffn2_post[l][None]
        gmpre, gmpost = g_mix_pre[l][None], g_mix_post[l][None]
        gn = ret_gn_g[l][None]
        w1gu, w1down = w_ffn1_gu[l].astype(BF16), w_ffn1_down[l].astype(BF16)
        w2gu, w2down = w_ffn2_gu[l].astype(BF16), w_ffn2_down[l].astype(BF16)
        win = w_in[l].astype(BF16)
        w_kv_t = w_in[l][:, _OFF_KSB:_OFF_QR].T.astype(BF16)
        wsb, wret, wo = w_sb_out[l].astype(BF16), w_ret_out[l].astype(BF16), w_o[l].astype(BF16)
        bias = sb_bias[l].astype(F32)
        bias_rows = jnp.broadcast_to(jnp.repeat(bias, dec_t)[:, None], (SB_HEADS * dec_t, SB_BLOCK))
        ck_t = cache_k[l].transpose(0, 2, 3, 1).reshape(n_pool, SB_WIDTH, page)
        cv_t = cache_v[l].transpose(0, 2, 3, 1).reshape(n_pool, SB_WIDTH, page)

        h1 = _ffn_call(hp, g1pre, w1gu, w1down, g1post)
        (k_t, v_t, kt_bf, v_bf, q_s0, q_s1, q_r, k_r, v_r, g_r, a_sb, a_r) = _inproj_prompt_call(
            h1, gmpre, win, w_kv_t, *rope_p, batch, seq)
        o_sb = _sb_prompt_call([q_s0.reshape(batch, seq, SB_WIDTH), q_s1.reshape(batch, seq, SB_WIDTH)],
                               kt_bf, v_bf.reshape(batch, seq, SB_WIDTH), bias, tri_strict)
        o_r, s_p = _ret_prompt_call(q_r.reshape(batch, seq, -1), k_r.reshape(batch, seq, -1),
                                    v_r.reshape(batch, seq, -1), g_r.reshape(batch, seq, -1),
                                    gn, decay_p)
        hp = _post_call(h1, o_sb.reshape(batch * seq, SB_WIDTH), o_r.reshape(batch * seq, RET_V_WIDTH),
                        a_sb, a_r, wsb, wret, wo, gmpost, g2pre, w2gu, w2down, g2post)
        kp_l.append(k_t.reshape(batch, SB_HEADS, SB_HEAD_DIM, seq).transpose(0, 3, 1, 2))
        vp_l.append(v_t.reshape(batch, SB_HEADS, SB_HEAD_DIM, seq).transpose(0, 3, 1, 2))
        sp_l.append(s_p)

        h1 = _ffn_call(hs, g1pre, w1gu, w1down, g1post)
        (k_sb, v_sb, q_sb, q_r, k_r, v_r, g_r, a_sb, a_r) = _inproj_decode_call(
            h1, gmpre, win, *rope_s)
        o_sb = _sb_decode_call(q_sb.reshape(n_seq, dec_t, SB_WIDTH), k_sb.reshape(n_seq, dec_t, SB_WIDTH),
                               v_sb.reshape(n_seq, dec_t, SB_WIDTH), ck_t, cv_t, page_table, bias_rows, tri2)
        o_r, s_s = _ret_decode_call(q_r, k_r, v_r, g_r, gn, state_ret[l], decay_s, dec_t)
        hs = _post_call(h1, o_sb.reshape(n_seq * dec_t, SB_WIDTH).astype(BF16), o_r, a_sb, a_r,
                        wsb, wret, wo, gmpost, g2pre, w2gu, w2down, g2post)
        ks_l.append(k_sb.reshape(n_seq, dec_t, SB_HEADS, SB_HEAD_DIM))
        vs_l.append(v_sb.reshape(n_seq, dec_t, SB_HEADS, SB_HEAD_DIM))
        ss_l.append(s_s)

    return (hp.reshape(batch, seq, d), hs.reshape(n_seq, dec_t, d),
            jnp.stack(kp_l), jnp.stack(vp_l), jnp.stack(sp_l),
            jnp.stack(ks_l), jnp.stack(vs_l), jnp.stack(ss_l))
```

```python
import functools

import jax
import jax.numpy as jnp
import numpy as np
from jax import lax
from jax.experimental import pallas as pl
from jax.experimental.pallas import tpu as pltpu

F32 = jnp.float32
BF16 = jnp.bfloat16

SB_HEADS = 8
SB_HEAD_DIM = 64
SB_WIDTH = SB_HEADS * SB_HEAD_DIM
RET_HEADS = 4
RET_QK_DIM = 128
RET_V_DIM = 256
RET_QK_WIDTH = RET_HEADS * RET_QK_DIM
RET_V_WIDTH = RET_HEADS * RET_V_DIM
RET_CHUNK = 128
ROPE_BASE = 10000.0
NORM_EPS = 1e-6
LOG2E = 1.4426950408889634

LANES = 128
HEADS_PER_LANE_TILE = LANES // SB_HEAD_DIM

ROW_TILE = 256
FFN_ROW_TILE = 512
SB_BLOCK = 256
SB_HEADS_PER_STEP = 4
DEC_PAGES_PER_CHUNK = 8
DEC_RING_SLOTS = 4
RET_ROWS_PER_STEP = 4096
VMEM_LIMIT = 56 * 1024 * 1024

_NT = (((1,), (1,)), ((), ()))


def _const_spec(shape):
    nd = len(shape)
    return pl.BlockSpec(shape, lambda *_: (0,) * nd, pipeline_mode=pl.Buffered(1))


def _params(n_axes, vmem=VMEM_LIMIT):
    return pltpu.CompilerParams(dimension_semantics=("arbitrary",) * n_axes,
                                vmem_limit_bytes=vmem)


def _rms(x, g):
    ms = jnp.mean(x * x, axis=-1, keepdims=True)
    return x * lax.rsqrt(ms + NORM_EPS) * g


def _silu(x):
    return x * jax.nn.sigmoid(x)


def _ffn_residual(x, g_pre, wgu_ref, wdown_ref, g_post):
    d_ff = wdown_ref.shape[0]
    xn = _rms(x, g_pre).astype(BF16)
    gate = jnp.dot(xn, wgu_ref[:, :d_ff], preferred_element_type=F32)
    up = jnp.dot(xn, wgu_ref[:, d_ff:], preferred_element_type=F32)
    act = (_silu(gate) * up).astype(BF16)
    y = jnp.dot(act, wdown_ref[...], preferred_element_type=F32)
    return x + 0.5 * _rms(y, g_post)


def _ffn_kernel(x_ref, gpre_ref, wgu_ref, wdown_ref, gpost_ref, o_ref):
    o_ref[...] = _ffn_residual(x_ref[...], gpre_ref[...], wgu_ref, wdown_ref, gpost_ref[...])


def _ffn_call(x, g_pre, wgu, wdown, g_post):
    n, d = x.shape
    tile = FFN_ROW_TILE if n % FFN_ROW_TILE == 0 else ROW_TILE
    row = pl.BlockSpec((tile, d), lambda i: (i, 0))
    return pl.pallas_call(
        _ffn_kernel,
        grid=(n // tile,),
        in_specs=[row, _const_spec(g_pre.shape), _const_spec(wgu.shape),
                  _const_spec(wdown.shape), _const_spec(g_post.shape)],
        out_specs=row,
        out_shape=jax.ShapeDtypeStruct((n, d), F32),
        compiler_params=_params(1),
        name="ffn_block",
    )(x, g_pre, wgu, wdown, g_post)


_OFF_QSB = 0
_OFF_KSB = _OFF_QSB + SB_WIDTH
_OFF_VSB = _OFF_KSB + SB_WIDTH
_OFF_QR = _OFF_VSB + SB_WIDTH
_OFF_KR = _OFF_QR + RET_QK_WIDTH
_OFF_VR = _OFF_KR + RET_QK_WIDTH
_OFF_GR = _OFF_VR + RET_V_WIDTH
_OFF_ASB = _OFF_GR + RET_V_WIDTH


def _inproj_shared(u, win_ref, cos_ref, sin_ref, qr_ref, kr_ref, vr_ref, gr_ref, asb_ref, ar_ref):
    def proj(lo, width):
        return jnp.dot(u, win_ref[:, lo:lo + width], preferred_element_type=F32)

    cos = cos_ref[...]
    sin = sin_ref[...]
    q_r = proj(_OFF_QR, RET_QK_WIDTH)
    k_r = proj(_OFF_KR, RET_QK_WIDTH)
    for hh in range(RET_HEADS):
        sl = slice(hh * RET_QK_DIM, (hh + 1) * RET_QK_DIM)
        qh = q_r[:, sl]
        kh = k_r[:, sl]
        qr_ref[:, sl] = qh * cos + pltpu.roll(qh, RET_QK_DIM // 2, axis=1) * sin
        kr_ref[:, sl] = (kh * cos + pltpu.roll(kh, RET_QK_DIM // 2, axis=1) * sin) * (RET_QK_DIM ** -0.5)
    vr_ref[...] = proj(_OFF_VR, RET_V_WIDTH).astype(BF16)
    gr_ref[...] = proj(_OFF_GR, RET_V_WIDTH)
    d_model = asb_ref.shape[1]
    asb_ref[...] = proj(_OFF_ASB, d_model)
    ar_ref[...] = proj(_OFF_ASB + d_model, d_model)


def _inproj_prompt_kernel(h_ref, g_ref, win_ref, wkvt_ref, cos_ref, sin_ref,
                          kt_ref, vt_ref, ktb_ref, vb_ref, *rest):
    q_refs, shared_refs = rest[:HEADS_PER_LANE_TILE], rest[HEADS_PER_LANE_TILE:]
    u = _rms(h_ref[...], g_ref[...]).astype(BF16)
    q = jnp.dot(u, win_ref[:, _OFF_QSB:_OFF_QSB + SB_WIDTH], preferred_element_type=F32)
    q = q * (SB_HEAD_DIM ** -0.5 * LOG2E)
    head_slot = (lax.broadcasted_iota(jnp.int32, (1, SB_WIDTH), 1) // SB_HEAD_DIM) % HEADS_PER_LANE_TILE
    for hh, q_ref in enumerate(q_refs):
        q_ref[...] = jnp.where(head_slot == hh, q, 0.0).astype(BF16)
    kv_t = lax.dot_general(wkvt_ref[...], u, _NT, preferred_element_type=F32)
    kt_ref[0] = kv_t[:SB_WIDTH]
    vt_ref[0] = kv_t[SB_WIDTH:]
    ktb_ref[0, 0] = kv_t[:SB_WIDTH].astype(BF16)
    vb_ref[...] = jnp.dot(u, win_ref[:, _OFF_VSB:_OFF_VSB + SB_WIDTH],
                          preferred_element_type=F32).astype(BF16)
    _inproj_shared(u, win_ref, cos_ref, sin_ref, *shared_refs)


def _inproj_decode_kernel(h_ref, g_ref, win_ref, cos_ref, sin_ref, k_ref, v_ref, q_ref, *shared_refs):
    u = _rms(h_ref[...], g_ref[...]).astype(BF16)
    q_ref[...] = jnp.dot(u, win_ref[:, _OFF_QSB:_OFF_QSB + SB_WIDTH],
                         preferred_element_type=F32) * (SB_HEAD_DIM ** -0.5)
    k_ref[...] = jnp.dot(u, win_ref[:, _OFF_KSB:_OFF_KSB + SB_WIDTH], preferred_element_type=F32)
    v_ref[...] = jnp.dot(u, win_ref[:, _OFF_VSB:_OFF_VSB + SB_WIDTH], preferred_element_type=F32)
    _inproj_shared(u, win_ref, cos_ref, sin_ref, *shared_refs)


def _row_spec(width):
    return pl.BlockSpec((ROW_TILE, width), lambda i: (i, 0))


def _shared_outputs(n, d):
    widths_dtypes = [(RET_QK_WIDTH, F32), (RET_QK_WIDTH, F32), (RET_V_WIDTH, BF16),
                     (RET_V_WIDTH, F32), (d, F32), (d, F32)]
    return ([_row_spec(w) for w, _ in widths_dtypes],
            [jax.ShapeDtypeStruct((n, w), dt) for w, dt in widths_dtypes])


def _inproj_prompt_call(h, g_pre, w_in, w_kv_t, cos_tab, sin_tab, batch, seq):
    n, d = h.shape
    assert ROW_TILE == SB_BLOCK
    tiles = seq // ROW_TILE
    tab = pl.BlockSpec((ROW_TILE, RET_QK_DIM), lambda i: (i % tiles, 0))
    t_spec = pl.BlockSpec((1, SB_WIDTH, ROW_TILE), lambda i: (i // tiles, 0, i % tiles))
    shared_specs, shared_shapes = _shared_outputs(n, d)
    return pl.pallas_call(
        _inproj_prompt_kernel,
        grid=(n // ROW_TILE,),
        in_specs=[_row_spec(d), _const_spec(g_pre.shape), _const_spec(w_in.shape),
                  _const_spec(w_kv_t.shape), tab, tab],
        out_specs=[t_spec, t_spec,
                   pl.BlockSpec((1, 1, SB_WIDTH, SB_BLOCK), lambda i: (i // tiles, i % tiles, 0, 0)),
                   _row_spec(SB_WIDTH)] + [_row_spec(SB_WIDTH)] * HEADS_PER_LANE_TILE + shared_specs,
        out_shape=[jax.ShapeDtypeStruct((batch, SB_WIDTH, seq), F32),
                   jax.ShapeDtypeStruct((batch, SB_WIDTH, seq), F32),
                   jax.ShapeDtypeStruct((batch, tiles, SB_WIDTH, SB_BLOCK), BF16),
                   jax.ShapeDtypeStruct((n, SB_WIDTH), BF16)]
                  + [jax.ShapeDtypeStruct((n, SB_WIDTH), BF16)] * HEADS_PER_LANE_TILE + shared_shapes,
        compiler_params=_params(1),
        name="in_projection_prompt",
    )(h, g_pre, w_in, w_kv_t, cos_tab, sin_tab)


def _inproj_decode_call(h, g_pre, w_in, cos_tab, sin_tab):
    n, d = h.shape
    tab = pl.BlockSpec((ROW_TILE, RET_QK_DIM), lambda i: (0, 0))
    shared_specs, shared_shapes = _shared_outputs(n, d)
    return pl.pallas_call(
        _inproj_decode_kernel,
        grid=(n // ROW_TILE,),
        in_specs=[_row_spec(d), _const_spec(g_pre.shape), _const_spec(w_in.shape), tab, tab],
        out_specs=[_row_spec(SB_WIDTH)] * 3 + shared_specs,
        out_shape=[jax.ShapeDtypeStruct((n, SB_WIDTH), F32)] * 3 + shared_shapes,
        compiler_params=_params(1),
        name="in_projection_decode",
    )(h, g_pre, w_in, cos_tab, sin_tab)


NULL_LOGIT = -1e30


def _softplus2(z2):
    neg_abs = pltpu.bitcast(pltpu.bitcast(z2, jnp.uint32) | jnp.uint32(0x80000000), F32)
    return jnp.maximum(z2, 0.0) + jnp.log(1.0 + jnp.exp2(neg_abs)) * LOG2E


def _sb_weights(z2, tri2, run2, mask):
    if mask is not None:
        z2 = jnp.where(mask, z2, NULL_LOGIT)
    sp2 = _softplus2(z2)
    hi = sp2.astype(BF16)
    lo = (sp2 - hi.astype(F32)).astype(BF16)
    csum = jnp.dot(jnp.concatenate([hi, lo], axis=1), tri2, preferred_element_type=F32) + run2
    return jnp.exp2(z2 - csum).astype(BF16), run2 + jnp.sum(sp2, axis=-1, keepdims=True)


_ITEM_QI, _ITEM_J, _ITEM_BIAS, _ITEM_FIRST, _ITEM_FIELDS = 0, 1, 2, 3, 4
_BIAS_FULL, _BIAS_DIAG, _BIAS_NULL = 0, 1, 2
_PIPE_DEPTH = 4


def _sb_items(n_blocks):
    pad = _PIPE_DEPTH - 1
    null = (0, 0, _BIAS_NULL, 1)
    items = [null] * pad
    for qi in range(n_blocks):
        for j in range(qi, -1, -1):
            items.append((qi, j, _BIAS_DIAG if j == qi else _BIAS_FULL, int(j == qi)))
    items += [null] * pad
    return jnp.asarray(items, jnp.int32).T.reshape(-1), len(items)


def _sb_prompt_kernel(items_ref, bias_ref, *refs, n_items):
    slots = HEADS_PER_LANE_TILE
    q_refs, (kt_ref, v_ref, tri_ref, o_ref) = refs[:slots], refs[slots:slots + 4]
    bsel_scr, zraw_scr, z_scr, sp_scr, c_scr, acc_scr, run_scr, runb_scr = refs[slots + 4:]
    nh = acc_scr.shape[0]
    hg = pl.program_id(1)
    blk = tri_ref.shape[0]
    tri = tri_ref[...]
    lane_head = lax.broadcasted_iota(jnp.int32, (1, LANES), 1) // SB_HEAD_DIM
    row = lax.broadcasted_iota(jnp.int32, (blk, blk), 0)
    col = lax.broadcasted_iota(jnp.int32, (blk, blk), 1)
    heads = range(nh)

    def item(field, i):
        return items_ref[field * n_items + i]

    for hh in heads:
        bias2 = bias_ref[hg * nh + hh] * LOG2E
        bsel_scr[hh, _BIAS_FULL] = jnp.full((blk, blk), bias2, F32)
        bsel_scr[hh, _BIAS_DIAG] = jnp.where(col < row, bias2, NULL_LOGIT)
        bsel_scr[hh, _BIAS_NULL] = jnp.full((blk, blk), NULL_LOGIT, F32)
        zraw_scr[hh] = jnp.full((blk, blk), NULL_LOGIT, F32)
        z_scr[0, hh] = jnp.full((blk, blk), NULL_LOGIT, F32)
        z_scr[1, hh] = jnp.full((blk, blk), NULL_LOGIT, F32)
        sp_scr[hh] = jnp.zeros((blk, blk), BF16)
        c_scr[hh] = jnp.zeros((blk, blk), F32)
        acc_scr[hh] = jnp.zeros((blk, LANES), F32)
        run_scr[hh] = jnp.zeros((blk, 1), F32)
        runb_scr[hh] = jnp.zeros((blk, 1), F32)

    def trip(t, _):
        slot = t % 2
        keep_w = 1.0 - item(_ITEM_FIRST, t).astype(F32)
        keep_s = 1.0 - item(_ITEM_FIRST, t + 2).astype(F32)
        j_l = item(_ITEM_J, t + 3)
        bias_l = item(_ITEM_BIAS, t + 3)
        v_rows = pl.ds(pl.multiple_of(item(_ITEM_J, t) * blk, blk), blk)
        q_rows = pl.ds(pl.multiple_of(item(_ITEM_QI, t + 3) * blk, blk), blk)
        o_rows = pl.ds(pl.multiple_of(item(_ITEM_QI, t) * blk, blk), blk)
        for lt in range(nh // slots):
            hs = slice(lt * slots, (lt + 1) * slots)
            lanes = slice(lt * LANES, (lt + 1) * LANES)
            a = jnp.exp2(z_scr[slot, hs] - c_scr[hs]).astype(BF16).reshape(slots * blk, blk)
            av = jnp.dot(a, v_ref[0, v_rows, lanes], preferred_element_type=F32).reshape(slots, blk, LANES)
            acc = acc_scr[hs] * keep_w + av
            acc_scr[hs] = acc
            out = acc[0]
            for s in range(1, slots):
                out = jnp.where(lane_head == s, acc[s], out)
            o_ref[0, o_rows, lanes] = out.astype(o_ref.dtype)
            c_scr[hs] = jnp.dot(sp_scr[hs].reshape(slots * blk, blk), tri,
                                preferred_element_type=F32).reshape(slots, blk, blk) + runb_scr[hs]
            z2 = zraw_scr[hs]
            sp2 = _softplus2(z2)
            run2 = run_scr[hs] * keep_s
            z_scr[slot, hs] = z2 - sp2
            runb_scr[hs] = run2
            sp_scr[hs] = sp2.astype(BF16)
            run_scr[hs] = run2 + jnp.sum(sp2, axis=-1, keepdims=True)
            q = jnp.concatenate([q_refs[s][0, q_rows, lanes] for s in range(slots)], axis=0)
            z_new = jnp.dot(q, kt_ref[0, j_l, lanes, :], preferred_element_type=F32)
            zraw_scr[hs] = z_new.reshape(slots, blk, blk) + bsel_scr[hs, bias_l]
        return 0

    lax.fori_loop(0, n_items - (_PIPE_DEPTH - 1), trip, 0)


def _sb_prompt_call(q_slots, kt_bf, v_bf, bias, tri):
    b, t, w = v_bf.shape
    blk = SB_BLOCK
    nh = SB_HEADS_PER_STEP
    gw = nh * SB_HEAD_DIM
    items, n_items = _sb_items(t // blk)
    seq_spec = pl.BlockSpec((1, t, gw), lambda bi, hg, it: (bi, 0, hg))
    grid_spec = pltpu.PrefetchScalarGridSpec(
        num_scalar_prefetch=1,
        grid=(b, w // gw),
        in_specs=([pl.BlockSpec(memory_space=pltpu.SMEM)] + [seq_spec] * HEADS_PER_LANE_TILE
                  + [pl.BlockSpec((1, t // blk, gw, blk), lambda bi, hg, it: (bi, 0, hg, 0)),
                     seq_spec,
                     pl.BlockSpec(tri.shape, lambda bi, hg, it: (0, 0))]),
        out_specs=seq_spec,
        scratch_shapes=[pltpu.VMEM((nh, 3, blk, blk), F32), pltpu.VMEM((nh, blk, blk), F32),
                        pltpu.VMEM((2, nh, blk, blk), F32), pltpu.VMEM((nh, blk, blk), BF16),
                        pltpu.VMEM((nh, blk, blk), F32), pltpu.VMEM((nh, blk, LANES), F32),
                        pltpu.VMEM((nh, blk, 1), F32), pltpu.VMEM((nh, blk, 1), F32)],
    )
    return pl.pallas_call(
        functools.partial(_sb_prompt_kernel, n_items=n_items),
        grid_spec=grid_spec,
        out_shape=jax.ShapeDtypeStruct((b, t, w), BF16),
        compiler_params=_params(2),
        name="sb_attention_prompt",
    )(items, bias, *q_slots, kt_bf, v_bf, tri)


def _sb_decode_kernel(pt_ref, q_ref, kn_ref, vn_ref, bias_ref, tri_ref, ck_hbm, cv_hbm, o_ref,
                      kbuf, vbuf, sem, *, chunk):
    s = pl.program_id(0)
    n_seq = pl.num_programs(0)
    n_pages = pt_ref.shape[1]
    n_chunks = n_pages // chunk
    dec_t = q_ref.shape[1]
    rows = SB_HEADS * dec_t
    blk = tri_ref.shape[1]
    pages_per_blk = blk // kbuf.shape[3]
    bias = bias_ref[...] * LOG2E
    tri = tri_ref[...]
    lane_head = lax.broadcasted_iota(jnp.int32, (1, SB_WIDTH), 1) // SB_HEAD_DIM

    n_slots = kbuf.shape[0]
    ahead = n_slots - 1
    total = n_seq * n_chunks

    def chunk_copies(g):
        slot = g % n_slots
        gc = jnp.minimum(g, total - 1)
        seq = gc // n_chunks
        base = n_pages - (gc % n_chunks + 1) * chunk
        out = []
        for i in range(chunk):
            page = pt_ref[seq, base + i]
            out.append(pltpu.make_async_copy(ck_hbm.at[page], kbuf.at[slot, i], sem.at[slot, 0]))
            out.append(pltpu.make_async_copy(cv_hbm.at[page], vbuf.at[slot, i], sem.at[slot, 1]))
        return out

    @pl.when(s == 0)
    def _():
        for g in range(ahead):
            for cp in chunk_copies(g):
                cp.start()

    q = q_ref[0] * LOG2E
    qbd = jnp.concatenate([jnp.where(lane_head == hh, q, 0.0) for hh in range(SB_HEADS)],
                          axis=0).astype(BF16)
    pad = jnp.zeros((blk - dec_t, SB_WIDTH), F32)
    k_new = jnp.concatenate([kn_ref[0], pad], axis=0).astype(BF16)
    v_new = jnp.concatenate([vn_ref[0], pad], axis=0).astype(BF16)
    t_row = lax.broadcasted_iota(jnp.int32, (rows, blk), 0) % dec_t
    col = lax.broadcasted_iota(jnp.int32, (rows, blk), 1)
    z = lax.dot_general(qbd, k_new, _NT, preferred_element_type=F32) + bias
    a, run = _sb_weights(z, tri, jnp.zeros((rows, 1), F32), col < t_row)
    acc = jnp.dot(a, v_new, preferred_element_type=F32)

    def visit(c, carry):
        run, acc = carry
        g = s * n_chunks + c
        slot = g % n_slots
        for cp in chunk_copies(g):
            cp.wait()
        for cp in chunk_copies(g + ahead):
            cp.start()
        groups = [slice(p * pages_per_blk, (p + 1) * pages_per_blk)
                  for p in reversed(range(chunk // pages_per_blk))]

        def block_t(buf, sl):
            return jnp.concatenate([buf[slot, i] for i in range(sl.start, sl.stop)], axis=1).astype(BF16)

        zs = [jnp.dot(qbd, block_t(kbuf, sl), preferred_element_type=F32) + bias for sl in groups]
        sps = [_softplus2(z) for z in zs]
        halves = []
        for sp2 in sps:
            hi = sp2.astype(BF16)
            halves.append(jnp.concatenate([hi, (sp2 - hi.astype(F32)).astype(BF16)], axis=1))
        csums = [jnp.dot(h, tri, preferred_element_type=F32) for h in halves]
        for z, sp2, csum, sl in zip(zs, sps, csums, groups):
            a = jnp.exp2(z - csum - run).astype(BF16)
            acc = acc + lax.dot_general(a, block_t(vbuf, sl), _NT, preferred_element_type=F32)
            run = run + jnp.sum(sp2, axis=-1, keepdims=True)
        return run, acc

    run, acc = lax.fori_loop(0, n_chunks, visit, (run, acc))

    @pl.when(s == n_seq - 1)
    def _():
        for g in range(ahead):
            for cp in chunk_copies(total + g):
                cp.wait()

    out = jnp.zeros((dec_t, SB_WIDTH), F32)
    for hh in range(SB_HEADS):
        out = jnp.where(lane_head == hh, acc[hh * dec_t:(hh + 1) * dec_t, :], out)
    o_ref[0] = out


def _sb_decode_call(q, k_new, v_new, cache_kt, cache_vt, page_table, bias_rows, tri):
    n_seq, dec_t, w = q.shape
    n_pages = page_table.shape[1]
    chunk = DEC_PAGES_PER_CHUNK
    page = cache_kt.shape[2]
    assert n_pages % chunk == 0
    slots = DEC_RING_SLOTS
    seq_spec = pl.BlockSpec((1, dec_t, w), lambda s, pt: (s, 0, 0))

    def const(shape):
        nd = len(shape)
        return pl.BlockSpec(shape, lambda s, pt: (0,) * nd)

    hbm = pl.BlockSpec(memory_space=pl.ANY)
    grid_spec = pltpu.PrefetchScalarGridSpec(
        num_scalar_prefetch=1,
        grid=(n_seq,),
        in_specs=[seq_spec, seq_spec, seq_spec, const(bias_rows.shape), const(tri.shape), hbm, hbm],
        out_specs=seq_spec,
        scratch_shapes=[pltpu.VMEM((slots, chunk, w, page), F32), pltpu.VMEM((slots, chunk, w, page), F32),
                        pltpu.SemaphoreType.DMA((slots, 2))],
    )
    return pl.pallas_call(
        functools.partial(_sb_decode_kernel, chunk=chunk),
        grid_spec=grid_spec,
        out_shape=jax.ShapeDtypeStruct((n_seq, dec_t, w), F32),
        compiler_params=_params(1),
        name="sb_attention_decode",
    )(page_table, q, k_new, v_new, bias_rows, tri, cache_kt, cache_vt)


def _head_norm_gate(o, gn, gate):
    o = o * lax.rsqrt(jnp.mean(o * o, axis=-1, keepdims=True) + NORM_EPS) * gn
    return (_silu(gate) * o).astype(BF16)


def _ret_prompt_kernel(dc_ref, q_ref, k_ref, v_ref, g_ref, gn_ref, din_ref, dq_ref, dk_ref,
                       o_ref, s_out_ref, s_ref):
    hh = pl.program_id(1)
    c = pl.program_id(2)

    @pl.when(c == 0)
    def _():
        s_ref[...] = jnp.zeros_like(s_ref)

    dc = dc_ref[hh]
    din = din_ref[0]
    dq = dq_ref[0]
    dk = dk_ref[0]
    gn = gn_ref[...]
    chunks = [slice(i * RET_CHUNK, (i + 1) * RET_CHUNK) for i in range(q_ref.shape[1] // RET_CHUNK)]
    qs = [q_ref[0, sl, :].astype(BF16) for sl in chunks]
    ks = [k_ref[0, sl, :] for sl in chunks]
    vs = [v_ref[0, sl, :] for sl in chunks]
    inners = [lax.dot_general(q, k.astype(BF16), _NT, preferred_element_type=F32) * din
              for q, k in zip(qs, ks)]
    gains = [jnp.dot((k * dk).T.astype(BF16), v, preferred_element_type=F32) for k, v in zip(ks, vs)]
    intras = [jnp.dot(inner.astype(BF16), v, preferred_element_type=F32) for inner, v in zip(inners, vs)]
    states = [s_ref[...]]
    for gain in gains:
        states.append(states[-1] * dc + gain)
    for sl, q, intra, state in zip(chunks, qs, intras, states):
        o = intra + jnp.dot(q, state.astype(BF16), preferred_element_type=F32) * dq
        o_ref[0, sl, :] = _head_norm_gate(o, gn, g_ref[0, sl, :])
    state = states[-1]
    s_ref[...] = state

    @pl.when(c == pl.num_programs(2) - 1)
    def _():
        s_out_ref[0, 0] = state


def _ret_prompt_call(q, k, v_bf, gate, gn, tables):
    b, t, _ = q.shape
    din, dq, dk, dc = tables
    rows = RET_ROWS_PER_STEP
    qk_spec = pl.BlockSpec((1, rows, RET_QK_DIM), lambda bi, h, c: (bi, c, h))
    v_spec = pl.BlockSpec((1, rows, RET_V_DIM), lambda bi, h, c: (bi, c, h))

    def head(shape):
        return pl.BlockSpec((1,) + shape, lambda bi, h, c: (h, 0, 0))

    return pl.pallas_call(
        _ret_prompt_kernel,
        grid=(b, RET_HEADS, t // rows),
        in_specs=[pl.BlockSpec(memory_space=pltpu.SMEM), qk_spec, qk_spec, v_spec, v_spec,
                  pl.BlockSpec((1, RET_V_DIM), lambda bi, h, c: (0, h)),
                  head((RET_CHUNK, RET_CHUNK)), head((RET_CHUNK, RET_V_DIM)),
                  head((RET_CHUNK, RET_QK_DIM))],
        out_specs=[v_spec,
                   pl.BlockSpec((1, 1, RET_QK_DIM, RET_V_DIM), lambda bi, h, c: (bi, h, 0, 0))],
        out_shape=[jax.ShapeDtypeStruct((b, t, RET_V_WIDTH), BF16),
                   jax.ShapeDtypeStruct((b, RET_HEADS, RET_QK_DIM, RET_V_DIM), F32)],
        scratch_shapes=[pltpu.VMEM((RET_QK_DIM, RET_V_DIM), F32)],
        compiler_params=_params(3),
        name="retention_prompt",
    )(dc, q, k, v_bf, gate, gn, din, dq, dk)


def _ret_decode_kernel(dc_ref, q_ref, k_ref, v_ref, g_ref, gn_ref, s_in_ref, din_ref, dq_ref, dk_ref,
                       o_ref, s_out_ref, *, dec_t):
    hh = pl.program_id(0)
    n_seq = s_in_ref.shape[0]
    rows = n_seq * dec_t
    q = q_ref[...]
    k = k_ref[...]
    v = v_ref[...]
    q_bf = q.astype(BF16)
    inner = lax.dot_general(q_bf, k.astype(BF16), _NT, preferred_element_type=F32) * din_ref[0]
    o = jnp.dot(inner.astype(BF16), v, preferred_element_type=F32)

    s_old = s_in_ref[:, 0].reshape(n_seq * RET_QK_DIM, RET_V_DIM)
    row_seq = lax.broadcasted_iota(jnp.int32, (rows, RET_QK_DIM), 0) // dec_t
    q_bd = jnp.concatenate([jnp.where(row_seq == s, q, 0.0) for s in range(n_seq)], axis=1)
    o = o + jnp.dot(q_bd.astype(BF16), s_old.astype(BF16), preferred_element_type=F32) * dq_ref[0]
    o_ref[...] = _head_norm_gate(o, gn_ref[...], g_ref[...])

    kd_t = (k * dk_ref[0]).T
    col_seq = lax.broadcasted_iota(jnp.int32, (RET_QK_DIM, rows), 1) // dec_t
    k_bd_t = jnp.concatenate([jnp.where(col_seq == s, kd_t, 0.0) for s in range(n_seq)], axis=0)
    s_new = s_old * dc_ref[hh] + jnp.dot(k_bd_t.astype(BF16), v, preferred_element_type=F32)
    s_out_ref[:, 0] = s_new.reshape(n_seq, RET_QK_DIM, RET_V_DIM)


def _ret_decode_call(q, k, v_bf, gate, gn, state, tables, dec_t):
    rows = q.shape[0]
    n_seq = state.shape[0]
    din, dq, dk, dc = tables
    qk_spec = pl.BlockSpec((rows, RET_QK_DIM), lambda h: (0, h))
    v_spec = pl.BlockSpec((rows, RET_V_DIM), lambda h: (0, h))
    s_spec = pl.BlockSpec((n_seq, 1, RET_QK_DIM, RET_V_DIM), lambda h: (0, h, 0, 0))

    def head(shape):
        return pl.BlockSpec((1,) + shape, lambda h: (h, 0, 0))

    return pl.pallas_call(
        functools.partial(_ret_decode_kernel, dec_t=dec_t),
        grid=(RET_HEADS,),
        in_specs=[pl.BlockSpec(memory_space=pltpu.SMEM), qk_spec, qk_spec, v_spec, v_spec,
                  pl.BlockSpec((1, RET_V_DIM), lambda h: (0, h)), s_spec,
                  head((rows, rows)), head((rows, RET_V_DIM)), head((rows, RET_QK_DIM))],
        out_specs=[v_spec, s_spec],
        out_shape=[jax.ShapeDtypeStruct((rows, RET_V_WIDTH), BF16),
                   jax.ShapeDtypeStruct(state.shape, F32)],
        compiler_params=_params(1),
        name="retention_decode",
    )(dc, q, k, v_bf, gate, gn, state, din, dq, dk)


def _post_kernel(h_ref, osb_ref, or_ref, asb_ref, ar_ref, wsb_ref, wret_ref, wo_ref, gmix_ref,
                 gpre_ref, wgu_ref, wdown_ref, gpost_ref, o_ref, h_scr):
    @pl.when(pl.program_id(0) == 0)
    def _():
        h_scr[...] = jnp.zeros_like(h_scr)

    h_prev = h_scr[...]
    d_ff = wdown_ref.shape[0]
    xn = _rms(h_prev, gpre_ref[...]).astype(BF16)
    sb = jnp.dot(osb_ref[...], wsb_ref[...], preferred_element_type=F32)
    ret = jnp.dot(or_ref[...], wret_ref[...], preferred_element_type=F32)
    gate = jnp.dot(xn, wgu_ref[:, :d_ff], preferred_element_type=F32)
    m = jax.nn.sigmoid(asb_ref[...]) * sb + jax.nn.sigmoid(ar_ref[...]) * ret
    mix = jnp.dot(m.astype(BF16), wo_ref[...], preferred_element_type=F32)
    up = jnp.dot(xn, wgu_ref[:, d_ff:], preferred_element_type=F32)
    act = (_silu(gate) * up).astype(BF16)
    y = jnp.dot(act, wdown_ref[...], preferred_element_type=F32)
    o_ref[...] = h_prev + 0.5 * _rms(y, gpost_ref[...])
    h_scr[...] = h_ref[...] + _rms(mix, gmix_ref[...])


def _post_call(h, o_sb, o_r, a_sb, a_r, w_sb, w_ret, w_o, g_mix, g_pre, wgu, wdown, g_post):
    n, d = h.shape
    tiles = n // ROW_TILE
    consts = [w_sb, w_ret, w_o, g_mix, g_pre, wgu, wdown, g_post]

    def merge_rows(width):
        return pl.BlockSpec((ROW_TILE, width), lambda s: (jnp.minimum(s, tiles - 1), 0))

    return pl.pallas_call(
        _post_kernel,
        grid=(tiles + 1,),
        in_specs=[merge_rows(d), merge_rows(SB_WIDTH), merge_rows(RET_V_WIDTH), merge_rows(d), merge_rows(d)]
                 + [_const_spec(c.shape) for c in consts],
        out_specs=pl.BlockSpec((ROW_TILE, d), lambda s: (jnp.maximum(s - 1, 0), 0)),
        out_shape=jax.ShapeDtypeStruct((n, d), F32),
        scratch_shapes=[pltpu.VMEM((ROW_TILE, d), F32)],
        compiler_params=_params(1),
        name="merge_out_ffn",
    )(h, o_sb, o_r, a_sb, a_r, *consts)


def _rope_tables(pos):
    half = RET_QK_DIM // 2
    freq = ROPE_BASE ** (-np.arange(half, dtype=np.float64) / half)
    ang = np.asarray(pos, np.float64)[:, None] * freq[None, :]
    cos, sin = np.cos(ang), np.sin(ang)
    return (jnp.asarray(np.concatenate([cos, cos], axis=1), F32),
            jnp.asarray(np.concatenate([-sin, sin], axis=1), F32))


def _decay_tables(chunk, reps):
    log_gamma = np.log1p(-np.exp2(-5.0 - np.arange(RET_HEADS, dtype=np.float64)))
    idx = np.arange(chunk, dtype=np.float64)
    diff = idx[:, None] - idx[None, :]
    d_in = np.where(diff >= 0, np.exp(log_gamma[:, None, None] * np.maximum(diff, 0.0)), 0.0)
    d_q = np.exp(log_gamma[:, None] * (idx + 1.0))
    d_k = np.exp(log_gamma[:, None] * (chunk - 1.0 - idx))
    d_c = np.exp(log_gamma * chunk)
    if reps > 1:
        seq = np.arange(chunk * reps) // chunk
        d_in = np.where(seq[:, None] == seq[None, :], np.tile(d_in, (1, reps, reps)), 0.0)
        d_q = np.tile(d_q, (1, reps))
        d_k = np.tile(d_k, (1, reps))
    n = chunk * reps
    d_q = np.broadcast_to(d_q[:, :, None], (RET_HEADS, n, RET_V_DIM))
    d_k = np.broadcast_to(d_k[:, :, None], (RET_HEADS, n, RET_QK_DIM))
    return tuple(jnp.asarray(a, F32) for a in (d_in, d_q, d_k, d_c))


def _tri(n, strict):
    i = np.arange(n)
    return jnp.asarray(i[:, None] > i[None, :] if strict else i[:, None] >= i[None, :], BF16)


def kernel(x_prompt, x_sample, cache_k, cache_v, state_ret, page_table, g_ffn1_pre, w_ffn1_gu, w_ffn1_down, g_ffn1_post, g_mix_pre, w_in, sb_bias, ret_gn_g, w_sb_out, w_ret_out, w_o, g_mix_post, g_ffn2_pre, w_ffn2_gu, w_ffn2_down, g_ffn2_post):
    batch, seq, d = x_prompt.shape
    n_seq, dec_t, _ = x_sample.shape
    depth = w_in.shape[0]
    n_pool = cache_k.shape[1]
    n_pages = page_table.shape[1]
    page = cache_k.shape[2]
    past_len = n_pages * page
    assert HEADS_PER_LANE_TILE == 2
    assert SB_HEADS_PER_STEP % HEADS_PER_LANE_TILE == 0 and SB_HEADS % SB_HEADS_PER_STEP == 0
    assert seq % RET_ROWS_PER_STEP == 0 and seq % SB_BLOCK == 0 and seq % ROW_TILE == 0
    assert (n_seq * dec_t) % ROW_TILE == 0 and ROW_TILE % dec_t == 0
    assert SB_BLOCK % page == 0
    assert (DEC_PAGES_PER_CHUNK * page) % SB_BLOCK == 0 and dec_t % 8 == 0 and dec_t <= SB_BLOCK
    assert dec_t % RET_CHUNK != 0

    rope_p = _rope_tables(np.arange(seq))
    rope_s = _rope_tables(past_len + np.arange(ROW_TILE) % dec_t)
    decay_p = _decay_tables(RET_CHUNK, 1)
    decay_s = _decay_tables(dec_t, n_seq)
    tri_strict = _tri(SB_BLOCK, strict=True)
    tri = _tri(SB_BLOCK, strict=False)
    tri2 = jnp.concatenate([tri, tri], axis=0)

    hp = x_prompt.reshape(batch * seq, d)
    hs = x_sample.reshape(n_seq * dec_t, d)
    kp_l, vp_l, sp_l, ks_l, vs_l, ss_l = [], [], [], [], [], []
    for l in range(depth):
        g1pre, g1post = g_ffn1_pre[l][None], g_ffn1_post[l][None]
        g2pre, g2post = g_ffn2_pre[l][None], g_ffn2_post[l][None]
        gmpre, gmpost = g_mix_pre[l][None], g_mix_post[l][None]
        gn = ret_gn_g[l][None]
        w1gu, w1down = w_ffn1_gu[l].astype(BF16), w_ffn1_down[l].astype(BF16)
        w2gu, w2down = w_ffn2_gu[l].astype(BF16), w_ffn2_down[l].astype(BF16)
        win = w_in[l].astype(BF16)
        w_kv_t = w_in[l][:, _OFF_KSB:_OFF_QR].T.astype(BF16)
        wsb, wret, wo = w_sb_out[l].astype(BF16), w_ret_out[l].astype(BF16), w_o[l].astype(BF16)
        bias = sb_bias[l].astype(F32)
        bias_rows = jnp.broadcast_to(jnp.repeat(bias, dec_t)[:, None], (SB_HEADS * dec_t, SB_BLOCK))
        ck_t = cache_k[l].transpose(0, 2, 3, 1).reshape(n_pool, SB_WIDTH, page)
        cv_t = cache_v[l].transpose(0, 2, 3, 1).reshape(n_pool, SB_WIDTH, page)

        h1 = _ffn_call(hp, g1pre, w1gu, w1down, g1post)
        (k_t, v_t, kt_bf, v_bf, q_s0, q_s1, q_r, k_r, v_r, g_r, a_sb, a_r) = _inproj_prompt_call(
            h1, gmpre, win, w_kv_t, *rope_p, batch, seq)
        o_sb = _sb_prompt_call([q_s0.reshape(batch, seq, SB_WIDTH), q_s1.reshape(batch, seq, SB_WIDTH)],
                               kt_bf, v_bf.reshape(batch, seq, SB_WIDTH), bias, tri_strict)
        o_r, s_p = _ret_prompt_call(q_r.reshape(batch, seq, -1), k_r.reshape(batch, seq, -1),
                                    v_r.reshape(batch, seq, -1), g_r.reshape(batch, seq, -1),
                                    gn, decay_p)
        hp = _post_call(h1, o_sb.reshape(batch * seq, SB_WIDTH), o_r.reshape(batch * seq, RET_V_WIDTH),
                        a_sb, a_r, wsb, wret, wo, gmpost, g2pre, w2gu, w2down, g2post)
        kp_l.append(k_t.reshape(batch, SB_HEADS, SB_HEAD_DIM, seq).transpose(0, 3, 1, 2))
        vp_l.append(v_t.reshape(batch, SB_HEADS, SB_HEAD_DIM, seq).transpose(0, 3, 1, 2))
        sp_l.append(s_p)

        h1 = _ffn_call(hs, g1pre, w1gu, w1down, g1post)
        (k_sb, v_sb, q_sb, q_r, k_r, v_r, g_r, a_sb, a_r) = _inproj_decode_call(
            h1, gmpre, win, *rope_s)
        o_sb = _sb_decode_call(q_sb.reshape(n_seq, dec_t, SB_WIDTH), k_sb.reshape(n_seq, dec_t, SB_WIDTH),
                               v_sb.reshape(n_seq, dec_t, SB_WIDTH), ck_t, cv_t, page_table, bias_rows, tri2)
        o_r, s_s = _ret_decode_call(q_r, k_r, v_r, g_r, gn, state_ret[l], decay_s, dec_t)
        hs = _post_call(h1, o_sb.reshape(n_seq * dec_t, SB_WIDTH).astype(BF16), o_r, a_sb, a_r,
                        wsb, wret, wo, gmpost, g2pre, w2gu, w2down, g2post)
        ks_l.append(k_sb.reshape(n_seq, dec_t, SB_HEADS, SB_HEAD_DIM))
        vs_l.append(v_sb.reshape(n_seq, dec_t, SB_HEADS, SB_HEAD_DIM))
        ss_l.append(s_s)

    return (hp.reshape(batch, seq, d), hs.reshape(n_seq, dec_t, d),
            jnp.stack(kp_l), jnp.stack(vp_l), jnp.stack(sp_l),
            jnp.stack(ks_l), jnp.stack(vs_l), jnp.stack(ss_l))
```

```python
import functools

import jax
import jax.numpy as jnp
import numpy as np
from jax import lax
from jax.experimental import pallas as pl
from jax.experimental.pallas import tpu as pltpu

F32 = jnp.float32
BF16 = jnp.bfloat16

SB_HEADS = 8
SB_HEAD_DIM = 64
SB_WIDTH = SB_HEADS * SB_HEAD_DIM
RET_HEADS = 4
RET_QK_DIM = 128
RET_V_DIM = 256
RET_QK_WIDTH = RET_HEADS * RET_QK_DIM
RET_V_WIDTH = RET_HEADS * RET_V_DIM
RET_CHUNK = 128
ROPE_BASE = 10000.0
NORM_EPS = 1e-6
LOG2E = 1.4426950408889634

LANES = 128
HEADS_PER_LANE_TILE = LANES // SB_HEAD_DIM

ROW_TILE = 256
FFN_ROW_TILE = 512
SB_BLOCK = 256
SB_HEADS_PER_STEP = 4
DEC_PAGES_PER_CHUNK = 8
DEC_RING_SLOTS = 4
RET_ROWS_PER_STEP = 4096
VMEM_LIMIT = 56 * 1024 * 1024

_NT = (((1,), (1,)), ((), ()))


def _const_spec(shape):
    nd = len(shape)
    return pl.BlockSpec(shape, lambda *_: (0,) * nd, pipeline_mode=pl.Buffered(1))


def _params(n_axes, vmem=VMEM_LIMIT):
    return pltpu.CompilerParams(dimension_semantics=("arbitrary",) * n_axes,
                                vmem_limit_bytes=vmem)


def _rms(x, g):
    ms = jnp.mean(x * x, axis=-1, keepdims=True)
    return x * lax.rsqrt(ms + NORM_EPS) * g


def _silu(x):
    return x * jax.nn.sigmoid(x)


def _ffn_residual(x, g_pre, wgu_ref, wdown_ref, g_post):
    d_ff = wdown_ref.shape[0]
    xn = _rms(x, g_pre).astype(BF16)
    gate = jnp.dot(xn, wgu_ref[:, :d_ff], preferred_element_type=F32)
    up = jnp.dot(xn, wgu_ref[:, d_ff:], preferred_element_type=F32)
    act = (_silu(gate) * up).astype(BF16)
    y = jnp.dot(act, wdown_ref[...], preferred_element_type=F32)
    return x + 0.5 * _rms(y, g_post)


def _ffn_kernel(x_ref, gpre_ref, wgu_ref, wdown_ref, gpost_ref, o_ref):
    o_ref[...] = _ffn_residual(x_ref[...], gpre_ref[...], wgu_ref, wdown_ref, gpost_ref[...])


def _ffn_call(x, g_pre, wgu, wdown, g_post):
    n, d = x.shape
    tile = FFN_ROW_TILE if n % FFN_ROW_TILE == 0 else ROW_TILE
    row = pl.BlockSpec((tile, d), lambda i: (i, 0))
    return pl.pallas_call(
        _ffn_kernel,
        grid=(n // tile,),
        in_specs=[row, _const_spec(g_pre.shape), _const_spec(wgu.shape),
                  _const_spec(wdown.shape), _const_spec(g_post.shape)],
        out_specs=row,
        out_shape=jax.ShapeDtypeStruct((n, d), F32),
        compiler_params=_params(1),
        name="ffn_block",
    )(x, g_pre, wgu, wdown, g_post)


_OFF_QSB = 0
_OFF_KSB = _OFF_QSB + SB_WIDTH
_OFF_VSB = _OFF_KSB + SB_WIDTH
_OFF_QR = _OFF_VSB + SB_WIDTH
_OFF_KR = _OFF_QR + RET_QK_WIDTH
_OFF_VR = _OFF_KR + RET_QK_WIDTH
_OFF_GR = _OFF_VR + RET_V_WIDTH
_OFF_ASB = _OFF_GR + RET_V_WIDTH


def _inproj_shared(u, win_ref, cos_ref, sin_ref, qr_ref, kr_ref, vr_ref, gr_ref, asb_ref, ar_ref):
    def proj(lo, width):
        return jnp.dot(u, win_ref[:, lo:lo + width], preferred_element_type=F32)

    cos = cos_ref[...]
    sin = sin_ref[...]
    q_r = proj(_OFF_QR, RET_QK_WIDTH)
    k_r = proj(_OFF_KR, RET_QK_WIDTH)
    for hh in range(RET_HEADS):
        sl = slice(hh * RET_QK_DIM, (hh + 1) * RET_QK_DIM)
        qh = q_r[:, sl]
        kh = k_r[:, sl]
        qr_ref[:, sl] = qh * cos + pltpu.roll(qh, RET_QK_DIM // 2, axis=1) * sin
        kr_ref[:, sl] = (kh * cos + pltpu.roll(kh, RET_QK_DIM // 2, axis=1) * sin) * (RET_QK_DIM ** -0.5)
    vr_ref[...] = proj(_OFF_VR, RET_V_WIDTH).astype(BF16)
    gr_ref[...] = proj(_OFF_GR, RET_V_WIDTH)
    d_model = asb_ref.shape[1]
    asb_ref[...] = proj(_OFF_ASB, d_model)
    ar_ref[...] = proj(_OFF_ASB + d_model, d_model)


def _inproj_prompt_kernel(h_ref, g_ref, win_ref, wkvt_ref, cos_ref, sin_ref,
                          kt_ref, vt_ref, ktb_ref, vb_ref, *rest):
    q_refs, shared_refs = rest[:HEADS_PER_LANE_TILE], rest[HEADS_PER_LANE_TILE:]
    u = _rms(h_ref[...], g_ref[...]).astype(BF16)
    q = jnp.dot(u, win_ref[:, _OFF_QSB:_OFF_QSB + SB_WIDTH], preferred_element_type=F32)
    q = q * (SB_HEAD_DIM ** -0.5 * LOG2E)
    head_slot = (lax.broadcasted_iota(jnp.int32, (1, SB_WIDTH), 1) // SB_HEAD_DIM) % HEADS_PER_LANE_TILE
    for hh, q_ref in enumerate(q_refs):
        q_ref[...] = jnp.where(head_slot == hh, q, 0.0).astype(BF16)
    kv_t = lax.dot_general(wkvt_ref[...], u, _NT, preferred_element_type=F32)
    kt_ref[0] = kv_t[:SB_WIDTH]
    vt_ref[0] = kv_t[SB_WIDTH:]
    ktb_ref[0, 0] = kv_t[:SB_WIDTH].astype(BF16)
    vb_ref[...] = jnp.dot(u, win_ref[:, _OFF_VSB:_OFF_VSB + SB_WIDTH],
                          preferred_element_type=F32).astype(BF16)
    _inproj_shared(u, win_ref, cos_ref, sin_ref, *shared_refs)


def _inproj_decode_kernel(h_ref, g_ref, win_ref, cos_ref, sin_ref, k_ref, v_ref, q_ref, *shared_refs):
    u = _rms(h_ref[...], g_ref[...]).astype(BF16)
    q_ref[...] = jnp.dot(u, win_ref[:, _OFF_QSB:_OFF_QSB + SB_WIDTH],
                         preferred_element_type=F32) * (SB_HEAD_DIM ** -0.5)
    k_ref[...] = jnp.dot(u, win_ref[:, _OFF_KSB:_OFF_KSB + SB_WIDTH], preferred_element_type=F32)
    v_ref[...] = jnp.dot(u, win_ref[:, _OFF_VSB:_OFF_VSB + SB_WIDTH], preferred_element_type=F32)
    _inproj_shared(u, win_ref, cos_ref, sin_ref, *shared_refs)


def _row_spec(width):
    return pl.BlockSpec((ROW_TILE, width), lambda i: (i, 0))


def _shared_outputs(n, d):
    widths_dtypes = [(RET_QK_WIDTH, F32), (RET_QK_WIDTH, F32), (RET_V_WIDTH, BF16),
                     (RET_V_WIDTH, F32), (d, F32), (d, F32)]
    return ([_row_spec(w) for w, _ in widths_dtypes],
            [jax.ShapeDtypeStruct((n, w), dt) for w, dt in widths_dtypes])


def _inproj_prompt_call(h, g_pre, w_in, w_kv_t, cos_tab, sin_tab, batch, seq):
    n, d = h.shape
    assert ROW_TILE == SB_BLOCK
    tiles = seq // ROW_TILE
    tab = pl.BlockSpec((ROW_TILE, RET_QK_DIM), lambda i: (i % tiles, 0))
    t_spec = pl.BlockSpec((1, SB_WIDTH, ROW_TILE), lambda i: (i // tiles, 0, i % tiles))
    shared_specs, shared_shapes = _shared_outputs(n, d)
    return pl.pallas_call(
        _inproj_prompt_kernel,
        grid=(n // ROW_TILE,),
        in_specs=[_row_spec(d), _const_spec(g_pre.shape), _const_spec(w_in.shape),
                  _const_spec(w_kv_t.shape), tab, tab],
        out_specs=[t_spec, t_spec,
                   pl.BlockSpec((1, 1, SB_WIDTH, SB_BLOCK), lambda i: (i // tiles, i % tiles, 0, 0)),
                   _row_spec(SB_WIDTH)] + [_row_spec(SB_WIDTH)] * HEADS_PER_LANE_TILE + shared_specs,
        out_shape=[jax.ShapeDtypeStruct((batch, SB_WIDTH, seq), F32),
                   jax.ShapeDtypeStruct((batch, SB_WIDTH, seq), F32),
                   jax.ShapeDtypeStruct((batch, tiles, SB_WIDTH, SB_BLOCK), BF16),
                   jax.ShapeDtypeStruct((n, SB_WIDTH), BF16)]
                  + [jax.ShapeDtypeStruct((n, SB_WIDTH), BF16)] * HEADS_PER_LANE_TILE + shared_shapes,
        compiler_params=_params(1),
        name="in_projection_prompt",
    )(h, g_pre, w_in, w_kv_t, cos_tab, sin_tab)


def _inproj_decode_call(h, g_pre, w_in, cos_tab, sin_tab):
    n, d = h.shape
    tab = pl.BlockSpec((ROW_TILE, RET_QK_DIM), lambda i: (0, 0))
    shared_specs, shared_shapes = _shared_outputs(n, d)
    return pl.pallas_call(
        _inproj_decode_kernel,
        grid=(n // ROW_TILE,),
        in_specs=[_row_spec(d), _const_spec(g_pre.shape), _const_spec(w_in.shape), tab, tab],
        out_specs=[_row_spec(SB_WIDTH)] * 3 + shared_specs,
        out_shape=[jax.ShapeDtypeStruct((n, SB_WIDTH), F32)] * 3 + shared_shapes,
        compiler_params=_params(1),
        name="in_projection_decode",
    )(h, g_pre, w_in, cos_tab, sin_tab)


NULL_LOGIT = -1e30


def _softplus2(z2):
    neg_abs = pltpu.bitcast(pltpu.bitcast(z2, jnp.uint32) | jnp.uint32(0x80000000), F32)
    return jnp.maximum(z2, 0.0) + jnp.log(1.0 + jnp.exp2(neg_abs)) * LOG2E


def _sb_weights(z2, tri2, run2, mask):
    if mask is not None:
        z2 = jnp.where(mask, z2, NULL_LOGIT)
    sp2 = _softplus2(z2)
    hi = sp2.astype(BF16)
    lo = (sp2 - hi.astype(F32)).astype(BF16)
    csum = jnp.dot(jnp.concatenate([hi, lo], axis=1), tri2, preferred_element_type=F32) + run2
    return jnp.exp2(z2 - csum).astype(BF16), run2 + jnp.sum(sp2, axis=-1, keepdims=True)


_ITEM_QI, _ITEM_J, _ITEM_BIAS, _ITEM_FIRST, _ITEM_FIELDS = 0, 1, 2, 3, 4
_BIAS_FULL, _BIAS_DIAG, _BIAS_NULL = 0, 1, 2
_PIPE_DEPTH = 4


def _sb_items(n_blocks):
    pad = _PIPE_DEPTH - 1
    null = (0, 0, _BIAS_NULL, 1)
    items = [null] * pad
    for qi in range(n_blocks):
        for j in range(qi, -1, -1):
            items.append((qi, j, _BIAS_DIAG if j == qi else _BIAS_FULL, int(j == qi)))
    items += [null] * pad
    return jnp.asarray(items, jnp.int32).T.reshape(-1), len(items)


def _sb_prompt_kernel(items_ref, bias_ref, *refs, n_items):
    slots = HEADS_PER_LANE_TILE
    q_refs, (kt_ref, v_ref, tri_ref, o_ref) = refs[:slots], refs[slots:slots + 4]
    bsel_scr, zraw_scr, z_scr, sp_scr, c_scr, acc_scr, run_scr, runb_scr = refs[slots + 4:]
    nh = acc_scr.shape[0]
    hg = pl.program_id(1)
    blk = tri_ref.shape[0]
    tri = tri_ref[...]
    lane_head = lax.broadcasted_iota(jnp.int32, (1, LANES), 1) // SB_HEAD_DIM
    row = lax.broadcasted_iota(jnp.int32, (blk, blk), 0)
    col = lax.broadcasted_iota(jnp.int32, (blk, blk), 1)
    heads = range(nh)

    def item(field, i):
        return items_ref[field * n_items + i]

    for hh in heads:
        bias2 = bias_ref[hg * nh + hh] * LOG2E
        bsel_scr[hh, _BIAS_FULL] = jnp.full((blk, blk), bias2, F32)
        bsel_scr[hh, _BIAS_DIAG] = jnp.where(col < row, bias2, NULL_LOGIT)
        bsel_scr[hh, _BIAS_NULL] = jnp.full((blk, blk), NULL_LOGIT, F32)
        zraw_scr[hh] = jnp.full((blk, blk), NULL_LOGIT, F32)
        z_scr[0, hh] = jnp.full((blk, blk), NULL_LOGIT, F32)
        z_scr[1, hh] = jnp.full((blk, blk), NULL_LOGIT, F32)
        sp_scr[hh] = jnp.zeros((blk, blk), BF16)
        c_scr[hh] = jnp.zeros((blk, blk), F32)
        acc_scr[hh] = jnp.zeros((blk, LANES), F32)
        run_scr[hh] = jnp.zeros((blk, 1), F32)
        runb_scr[hh] = jnp.zeros((blk, 1), F32)

    def trip(t, _):
        slot = t % 2
        keep_w = 1.0 - item(_ITEM_FIRST, t).astype(F32)
        keep_s = 1.0 - item(_ITEM_FIRST, t + 2).astype(F32)
        j_l = item(_ITEM_J, t + 3)
        bias_l = item(_ITEM_BIAS, t + 3)
        v_rows = pl.ds(pl.multiple_of(item(_ITEM_J, t) * blk, blk), blk)
        q_rows = pl.ds(pl.multiple_of(item(_ITEM_QI, t + 3) * blk, blk), blk)
        o_rows = pl.ds(pl.multiple_of(item(_ITEM_QI, t) * blk, blk), blk)
        for lt in range(nh // slots):
            hs = slice(lt * slots, (lt + 1) * slots)
            lanes = slice(lt * LANES, (lt + 1) * LANES)
            a = jnp.exp2(z_scr[slot, hs] - c_scr[hs]).astype(BF16).reshape(slots * blk, blk)
            av = jnp.dot(a, v_ref[0, v_rows, lanes], preferred_element_type=F32).reshape(slots, blk, LANES)
            acc_scr[hs] = acc_scr[hs] * keep_w + av
            c_scr[hs] = jnp.dot(sp_scr[hs].reshape(slots * blk, blk), tri,
                                preferred_element_type=F32).reshape(slots, blk, blk) + runb_scr[hs]
            z2 = zraw_scr[hs]
            sp2 = _softplus2(z2)
            run2 = run_scr[hs] * keep_s
            z_scr[slot, hs] = z2 - sp2
            runb_scr[hs] = run2
            sp_scr[hs] = sp2.astype(BF16)
            run_scr[hs] = run2 + jnp.sum(sp2, axis=-1, keepdims=True)
            q = jnp.concatenate([q_refs[s][0, q_rows, lanes] for s in range(slots)], axis=0)
            z_new = jnp.dot(q, kt_ref[0, j_l, lanes, :], preferred_element_type=F32)
            zraw_scr[hs] = z_new.reshape(slots, blk, blk) + bsel_scr[hs, bias_l]

        @pl.when(item(_ITEM_FIRST, t + 1) == 1)
        def _():
            for lt in range(nh // slots):
                acc = acc_scr[lt * slots:(lt + 1) * slots]
                out = acc[0]
                for s in range(1, slots):
                    out = jnp.where(lane_head == s, acc[s], out)
                o_ref[0, o_rows, lt * LANES:(lt + 1) * LANES] = out.astype(o_ref.dtype)

        return 0

    lax.fori_loop(0, n_items - (_PIPE_DEPTH - 1), trip, 0)


def _sb_prompt_call(q_slots, kt_bf, v_bf, bias, tri):
    b, t, w = v_bf.shape
    blk = SB_BLOCK
    nh = SB_HEADS_PER_STEP
    gw = nh * SB_HEAD_DIM
    items, n_items = _sb_items(t // blk)
    seq_spec = pl.BlockSpec((1, t, gw), lambda bi, hg, it: (bi, 0, hg))
    grid_spec = pltpu.PrefetchScalarGridSpec(
        num_scalar_prefetch=1,
        grid=(b, w // gw),
        in_specs=([pl.BlockSpec(memory_space=pltpu.SMEM)] + [seq_spec] * HEADS_PER_LANE_TILE
                  + [pl.BlockSpec((1, t // blk, gw, blk), lambda bi, hg, it: (bi, 0, hg, 0)),
                     seq_spec,
                     pl.BlockSpec(tri.shape, lambda bi, hg, it: (0, 0))]),
        out_specs=seq_spec,
        scratch_shapes=[pltpu.VMEM((nh, 3, blk, blk), F32), pltpu.VMEM((nh, blk, blk), F32),
                        pltpu.VMEM((2, nh, blk, blk), F32), pltpu.VMEM((nh, blk, blk), BF16),
                        pltpu.VMEM((nh, blk, blk), F32), pltpu.VMEM((nh, blk, LANES), F32),
                        pltpu.VMEM((nh, blk, 1), F32), pltpu.VMEM((nh, blk, 1), F32)],
    )
    return pl.pallas_call(
        functools.partial(_sb_prompt_kernel, n_items=n_items),
        grid_spec=grid_spec,
        out_shape=jax.ShapeDtypeStruct((b, t, w), BF16),
        compiler_params=_params(2),
        name="sb_attention_prompt",
    )(items, bias, *q_slots, kt_bf, v_bf, tri)


def _sb_decode_kernel(pt_ref, q_ref, kn_ref, vn_ref, bias_ref, tri_ref, ck_hbm, cv_hbm, o_ref,
                      kbuf, vbuf, sem, *, chunk):
    s = pl.program_id(0)
    n_seq = pl.num_programs(0)
    n_pages = pt_ref.shape[1]
    n_chunks = n_pages // chunk
    dec_t = q_ref.shape[1]
    rows = SB_HEADS * dec_t
    blk = tri_ref.shape[1]
    pages_per_blk = blk // kbuf.shape[3]
    bias = bias_ref[...] * LOG2E
    tri = tri_ref[...]
    lane_head = lax.broadcasted_iota(jnp.int32, (1, SB_WIDTH), 1) // SB_HEAD_DIM

    n_slots = kbuf.shape[0]
    ahead = n_slots - 1
    total = n_seq * n_chunks

    def chunk_copies(g):
        slot = g % n_slots
        gc = jnp.minimum(g, total - 1)
        seq = gc // n_chunks
        base = n_pages - (gc % n_chunks + 1) * chunk
        out = []
        for i in range(chunk):
            page = pt_ref[seq, base + i]
            out.append(pltpu.make_async_copy(ck_hbm.at[page], kbuf.at[slot, i], sem.at[slot, 0]))
            out.append(pltpu.make_async_copy(cv_hbm.at[page], vbuf.at[slot, i], sem.at[slot, 1]))
        return out

    @pl.when(s == 0)
    def _():
        for g in range(ahead):
            for cp in chunk_copies(g):
                cp.start()

    q = q_ref[0] * LOG2E
    qbd = jnp.concatenate([jnp.where(lane_head == hh, q, 0.0) for hh in range(SB_HEADS)],
                          axis=0).astype(BF16)
    pad = jnp.zeros((blk - dec_t, SB_WIDTH), F32)
    k_new = jnp.concatenate([kn_ref[0], pad], axis=0).astype(BF16)
    v_new = jnp.concatenate([vn_ref[0], pad], axis=0).astype(BF16)
    t_row = lax.broadcasted_iota(jnp.int32, (rows, blk), 0) % dec_t
    col = lax.broadcasted_iota(jnp.int32, (rows, blk), 1)
    z = lax.dot_general(qbd, k_new, _NT, preferred_element_type=F32) + bias
    a, run = _sb_weights(z, tri, jnp.zeros((rows, 1), F32), col < t_row)
    acc = jnp.dot(a, v_new, preferred_element_type=F32)

    def visit(c, carry):
        run, acc = carry
        g = s * n_chunks + c
        slot = g % n_slots
        for cp in chunk_copies(g):
            cp.wait()
        for cp in chunk_copies(g + ahead):
            cp.start()
        groups = [slice(p * pages_per_blk, (p + 1) * pages_per_blk)
                  for p in reversed(range(chunk // pages_per_blk))]

        def block_t(buf, sl):
            return jnp.concatenate([buf[slot, i] for i in range(sl.start, sl.stop)], axis=1).astype(BF16)

        zs = [jnp.dot(qbd, block_t(kbuf, sl), preferred_element_type=F32) + bias for sl in groups]
        sps = [_softplus2(z) for z in zs]
        halves = []
        for sp2 in sps:
            hi = sp2.astype(BF16)
            halves.append(jnp.concatenate([hi, (sp2 - hi.astype(F32)).astype(BF16)], axis=1))
        csums = [jnp.dot(h, tri, preferred_element_type=F32) for h in halves]
        for z, sp2, csum, sl in zip(zs, sps, csums, groups):
            a = jnp.exp2(z - csum - run).astype(BF16)
            acc = acc + lax.dot_general(a, block_t(vbuf, sl), _NT, preferred_element_type=F32)
            run = run + jnp.sum(sp2, axis=-1, keepdims=True)
        return run, acc

    run, acc = lax.fori_loop(0, n_chunks, visit, (run, acc))

    @pl.when(s == n_seq - 1)
    def _():
        for g in range(ahead):
            for cp in chunk_copies(total + g):
                cp.wait()

    out = jnp.zeros((dec_t, SB_WIDTH), F32)
    for hh in range(SB_HEADS):
        out = jnp.where(lane_head == hh, acc[hh * dec_t:(hh + 1) * dec_t, :], out)
    o_ref[0] = out


def _sb_decode_call(q, k_new, v_new, cache_kt, cache_vt, page_table, bias_rows, tri):
    n_seq, dec_t, w = q.shape
    n_pages = page_table.shape[1]
    chunk = DEC_PAGES_PER_CHUNK
    page = cache_kt.shape[2]
    assert n_pages % chunk == 0
    slots = DEC_RING_SLOTS
    seq_spec = pl.BlockSpec((1, dec_t, w), lambda s, pt: (s, 0, 0))

    def const(shape):
        nd = len(shape)
        return pl.BlockSpec(shape, lambda s, pt: (0,) * nd)

    hbm = pl.BlockSpec(memory_space=pl.ANY)
    grid_spec = pltpu.PrefetchScalarGridSpec(
        num_scalar_prefetch=1,
        grid=(n_seq,),
        in_specs=[seq_spec, seq_spec, seq_spec, const(bias_rows.shape), const(tri.shape), hbm, hbm],
        out_specs=seq_spec,
        scratch_shapes=[pltpu.VMEM((slots, chunk, w, page), F32), pltpu.VMEM((slots, chunk, w, page), F32),
                        pltpu.SemaphoreType.DMA((slots, 2))],
    )
    return pl.pallas_call(
        functools.partial(_sb_decode_kernel, chunk=chunk),
        grid_spec=grid_spec,
        out_shape=jax.ShapeDtypeStruct((n_seq, dec_t, w), F32),
        compiler_params=_params(1),
        name="sb_attention_decode",
    )(page_table, q, k_new, v_new, bias_rows, tri, cache_kt, cache_vt)


def _head_norm_gate(o, gn, gate):
    o = o * lax.rsqrt(jnp.mean(o * o, axis=-1, keepdims=True) + NORM_EPS) * gn
    return (_silu(gate) * o).astype(BF16)


def _ret_prompt_kernel(dc_ref, q_ref, k_ref, v_ref, g_ref, gn_ref, din_ref, dq_ref, dk_ref,
                       o_ref, s_out_ref, s_ref):
    hh = pl.program_id(1)
    c = pl.program_id(2)

    @pl.when(c == 0)
    def _():
        s_ref[...] = jnp.zeros_like(s_ref)

    dc = dc_ref[hh]
    din = din_ref[0]
    dq = dq_ref[0]
    dk = dk_ref[0]
    gn = gn_ref[...]
    chunks = [slice(i * RET_CHUNK, (i + 1) * RET_CHUNK) for i in range(q_ref.shape[1] // RET_CHUNK)]
    qs = [q_ref[0, sl, :].astype(BF16) for sl in chunks]
    ks = [k_ref[0, sl, :] for sl in chunks]
    vs = [v_ref[0, sl, :] for sl in chunks]
    inners = [lax.dot_general(q, k.astype(BF16), _NT, preferred_element_type=F32) * din
              for q, k in zip(qs, ks)]
    gains = [jnp.dot((k * dk).T.astype(BF16), v, preferred_element_type=F32) for k, v in zip(ks, vs)]
    intras = [jnp.dot(inner.astype(BF16), v, preferred_element_type=F32) for inner, v in zip(inners, vs)]
    states = [s_ref[...]]
    for gain in gains:
        states.append(states[-1] * dc + gain)
    for sl, q, intra, state in zip(chunks, qs, intras, states):
        o = intra + jnp.dot(q, state.astype(BF16), preferred_element_type=F32) * dq
        o_ref[0, sl, :] = _head_norm_gate(o, gn, g_ref[0, sl, :])
    state = states[-1]
    s_ref[...] = state

    @pl.when(c == pl.num_programs(2) - 1)
    def _():
        s_out_ref[0, 0] = state


def _ret_prompt_call(q, k, v_bf, gate, gn, tables):
    b, t, _ = q.shape
    din, dq, dk, dc = tables
    rows = RET_ROWS_PER_STEP
    qk_spec = pl.BlockSpec((1, rows, RET_QK_DIM), lambda bi, h, c: (bi, c, h))
    v_spec = pl.BlockSpec((1, rows, RET_V_DIM), lambda bi, h, c: (bi, c, h))

    def head(shape):
        return pl.BlockSpec((1,) + shape, lambda bi, h, c: (h, 0, 0))

    return pl.pallas_call(
        _ret_prompt_kernel,
        grid=(b, RET_HEADS, t // rows),
        in_specs=[pl.BlockSpec(memory_space=pltpu.SMEM), qk_spec, qk_spec, v_spec, v_spec,
                  pl.BlockSpec((1, RET_V_DIM), lambda bi, h, c: (0, h)),
                  head((RET_CHUNK, RET_CHUNK)), head((RET_CHUNK, RET_V_DIM)),
                  head((RET_CHUNK, RET_QK_DIM))],
        out_specs=[v_spec,
                   pl.BlockSpec((1, 1, RET_QK_DIM, RET_V_DIM), lambda bi, h, c: (bi, h, 0, 0))],
        out_shape=[jax.ShapeDtypeStruct((b, t, RET_V_WIDTH), BF16),
                   jax.ShapeDtypeStruct((b, RET_HEADS, RET_QK_DIM, RET_V_DIM), F32)],
        scratch_shapes=[pltpu.VMEM((RET_QK_DIM, RET_V_DIM), F32)],
        compiler_params=_params(3),
        name="retention_prompt",
    )(dc, q, k, v_bf, gate, gn, din, dq, dk)


def _ret_decode_kernel(dc_ref, q_ref, k_ref, v_ref, g_ref, gn_ref, s_in_ref, din_ref, dq_ref, dk_ref,
                       o_ref, s_out_ref, *, dec_t):
    hh = pl.program_id(0)
    n_seq = s_in_ref.shape[0]
    rows = n_seq * dec_t
    q = q_ref[...]
    k = k_ref[...]
    v = v_ref[...]
    q_bf = q.astype(BF16)
    inner = lax.dot_general(q_bf, k.astype(BF16), _NT, preferred_element_type=F32) * din_ref[0]
    o = jnp.dot(inner.astype(BF16), v, preferred_element_type=F32)

    s_old = s_in_ref[:, 0].reshape(n_seq * RET_QK_DIM, RET_V_DIM)
    row_seq = lax.broadcasted_iota(jnp.int32, (rows, RET_QK_DIM), 0) // dec_t
    q_bd = jnp.concatenate([jnp.where(row_seq == s, q, 0.0) for s in range(n_seq)], axis=1)
    o = o + jnp.dot(q_bd.astype(BF16), s_old.astype(BF16), preferred_element_type=F32) * dq_ref[0]
    o_ref[...] = _head_norm_gate(o, gn_ref[...], g_ref[...])

    kd_t = (k * dk_ref[0]).T
    col_seq = lax.broadcasted_iota(jnp.int32, (RET_QK_DIM, rows), 1) // dec_t
    k_bd_t = jnp.concatenate([jnp.where(col_seq == s, kd_t, 0.0) for s in range(n_seq)], axis=0)
    s_new = s_old * dc_ref[hh] + jnp.dot(k_bd_t.astype(BF16), v, preferred_element_type=F32)
    s_out_ref[:, 0] = s_new.reshape(n_seq, RET_QK_DIM, RET_V_DIM)


def _ret_decode_call(q, k, v_bf, gate, gn, state, tables, dec_t):
    rows = q.shape[0]
    n_seq = state.shape[0]
    din, dq, dk, dc = tables
    qk_spec = pl.BlockSpec((rows, RET_QK_DIM), lambda h: (0, h))
    v_spec = pl.BlockSpec((rows, RET_V_DIM), lambda h: (0, h))
    s_spec = pl.BlockSpec((n_seq, 1, RET_QK_DIM, RET_V_DIM), lambda h: (0, h, 0, 0))

    def head(shape):
        return pl.BlockSpec((1,) + shape, lambda h: (h, 0, 0))

    return pl.pallas_call(
        functools.partial(_ret_decode_kernel, dec_t=dec_t),
        grid=(RET_HEADS,),
        in_specs=[pl.BlockSpec(memory_space=pltpu.SMEM), qk_spec, qk_spec, v_spec, v_spec,
                  pl.BlockSpec((1, RET_V_DIM), lambda h: (0, h)), s_spec,
                  head((rows, rows)), head((rows, RET_V_DIM)), head((rows, RET_QK_DIM))],
        out_specs=[v_spec, s_spec],
        out_shape=[jax.ShapeDtypeStruct((rows, RET_V_WIDTH), BF16),
                   jax.ShapeDtypeStruct(state.shape, F32)],
        compiler_params=_params(1),
        name="retention_decode",
    )(dc, q, k, v_bf, gate, gn, state, din, dq, dk)


def _post_kernel(h_ref, osb_ref, or_ref, asb_ref, ar_ref, wsb_ref, wret_ref, wo_ref, gmix_ref,
                 gpre_ref, wgu_ref, wdown_ref, gpost_ref, o_ref, h_scr):
    @pl.when(pl.program_id(0) == 0)
    def _():
        h_scr[...] = jnp.zeros_like(h_scr)

    h_prev = h_scr[...]
    d_ff = wdown_ref.shape[0]
    xn = _rms(h_prev, gpre_ref[...]).astype(BF16)
    sb = jnp.dot(osb_ref[...], wsb_ref[...], preferred_element_type=F32)
    ret = jnp.dot(or_ref[...], wret_ref[...], preferred_element_type=F32)
    gate = jnp.dot(xn, wgu_ref[:, :d_ff], preferred_element_type=F32)
    m = jax.nn.sigmoid(asb_ref[...]) * sb + jax.nn.sigmoid(ar_ref[...]) * ret
    mix = jnp.dot(m.astype(BF16), wo_ref[...], preferred_element_type=F32)
    up = jnp.dot(xn, wgu_ref[:, d_ff:], preferred_element_type=F32)
    act = (_silu(gate) * up).astype(BF16)
    y = jnp.dot(act, wdown_ref[...], preferred_element_type=F32)
    o_ref[...] = h_prev + 0.5 * _rms(y, gpost_ref[...])
    h_scr[...] = h_ref[...] + _rms(mix, gmix_ref[...])


def _post_call(h, o_sb, o_r, a_sb, a_r, w_sb, w_ret, w_o, g_mix, g_pre, wgu, wdown, g_post):
    n, d = h.shape
    tiles = n // ROW_TILE
    consts = [w_sb, w_ret, w_o, g_mix, g_pre, wgu, wdown, g_post]

    def merge_rows(width):
        return pl.BlockSpec((ROW_TILE, width), lambda s: (jnp.minimum(s, tiles - 1), 0))

    return pl.pallas_call(
        _post_kernel,
        grid=(tiles + 1,),
        in_specs=[merge_rows(d), merge_rows(SB_WIDTH), merge_rows(RET_V_WIDTH), merge_rows(d), merge_rows(d)]
                 + [_const_spec(c.shape) for c in consts],
        out_specs=pl.BlockSpec((ROW_TILE, d), lambda s: (jnp.maximum(s - 1, 0), 0)),
        out_shape=jax.ShapeDtypeStruct((n, d), F32),
        scratch_shapes=[pltpu.VMEM((ROW_TILE, d), F32)],
        compiler_params=_params(1),
        name="merge_out_ffn",
    )(h, o_sb, o_r, a_sb, a_r, *consts)


def _rope_tables(pos):
    half = RET_QK_DIM // 2
    freq = ROPE_BASE ** (-np.arange(half, dtype=np.float64) / half)
    ang = np.asarray(pos, np.float64)[:, None] * freq[None, :]
    cos, sin = np.cos(ang), np.sin(ang)
    return (jnp.asarray(np.concatenate([cos, cos], axis=1), F32),
            jnp.asarray(np.concatenate([-sin, sin], axis=1), F32))


def _decay_tables(chunk, reps):
    log_gamma = np.log1p(-np.exp2(-5.0 - np.arange(RET_HEADS, dtype=np.float64)))
    idx = np.arange(chunk, dtype=np.float64)
    diff = idx[:, None] - idx[None, :]
    d_in = np.where(diff >= 0, np.exp(log_gamma[:, None, None] * np.maximum(diff, 0.0)), 0.0)
    d_q = np.exp(log_gamma[:, None] * (idx + 1.0))
    d_k = np.exp(log_gamma[:, None] * (chunk - 1.0 - idx))
    d_c = np.exp(log_gamma * chunk)
    if reps > 1:
        seq = np.arange(chunk * reps) // chunk
        d_in = np.where(seq[:, None] == seq[None, :], np.tile(d_in, (1, reps, reps)), 0.0)
        d_q = np.tile(d_q, (1, reps))
        d_k = np.tile(d_k, (1, reps))
    n = chunk * reps
    d_q = np.broadcast_to(d_q[:, :, None], (RET_HEADS, n, RET_V_DIM))
    d_k = np.broadcast_to(d_k[:, :, None], (RET_HEADS, n, RET_QK_DIM))
    return tuple(jnp.asarray(a, F32) for a in (d_in, d_q, d_k, d_c))


def _tri(n, strict):
    i = np.arange(n)
    return jnp.asarray(i[:, None] > i[None, :] if strict else i[:, None] >= i[None, :], BF16)


def kernel(x_prompt, x_sample, cache_k, cache_v, state_ret, page_table, g_ffn1_pre, w_ffn1_gu, w_ffn1_down, g_ffn1_post, g_mix_pre, w_in, sb_bias, ret_gn_g, w_sb_out, w_ret_out, w_o, g_mix_post, g_ffn2_pre, w_ffn2_gu, w_ffn2_down, g_ffn2_post):
    batch, seq, d = x_prompt.shape
    n_seq, dec_t, _ = x_sample.shape
    depth = w_in.shape[0]
    n_pool = cache_k.shape[1]
    n_pages = page_table.shape[1]
    page = cache_k.shape[2]
    past_len = n_pages * page
    assert HEADS_PER_LANE_TILE == 2
    assert SB_HEADS_PER_STEP % HEADS_PER_LANE_TILE == 0 and SB_HEADS % SB_HEADS_PER_STEP == 0
    assert seq % RET_ROWS_PER_STEP == 0 and seq % SB_BLOCK == 0 and seq % ROW_TILE == 0
    assert (n_seq * dec_t) % ROW_TILE == 0 and ROW_TILE % dec_t == 0
    assert SB_BLOCK % page == 0
    assert (DEC_PAGES_PER_CHUNK * page) % SB_BLOCK == 0 and dec_t % 8 == 0 and dec_t <= SB_BLOCK
    assert dec_t % RET_CHUNK != 0

    rope_p = _rope_tables(np.arange(seq))
    rope_s = _rope_tables(past_len + np.arange(ROW_TILE) % dec_t)
    decay_p = _decay_tables(RET_CHUNK, 1)
    decay_s = _decay_tables(dec_t, n_seq)
    tri_strict = _tri(SB_BLOCK, strict=True)
    tri = _tri(SB_BLOCK, strict=False)
    tri2 = jnp.concatenate([tri, tri], axis=0)

    hp = x_prompt.reshape(batch * seq, d)
    hs = x_sample.reshape(n_seq * dec_t, d)
    kp_l, vp_l, sp_l, ks_l, vs_l, ss_l = [], [], [], [], [], []
    for l in range(depth):
        g1pre, g1post = g_ffn1_pre[l][None], g_ffn1_post[l][None]
        g2pre, g2post = g_ffn2_pre[l][None], g_ffn2_post[l][None]
        gmpre, gmpost = g_mix_pre[l][None], g_mix_post[l][None]
        gn = ret_gn_g[l][None]
        w1gu, w1down = w_ffn1_gu[l].astype(BF16), w_ffn1_down[l].astype(BF16)
        w2gu, w2down = w_ffn2_gu[l].astype(BF16), w_ffn2_down[l].astype(BF16)
        win = w_in[l].astype(BF16)
        w_kv_t = w_in[l][:, _OFF_KSB:_OFF_QR].T.astype(BF16)
        wsb, wret, wo = w_sb_out[l].astype(BF16), w_ret_out[l].astype(BF16), w_o[l].astype(BF16)
        bias = sb_bias[l].astype(F32)
        bias_rows = jnp.broadcast_to(jnp.repeat(bias, dec_t)[:, None], (SB_HEADS * dec_t, SB_BLOCK))
        ck_t = cache_k[l].transpose(0, 2, 3, 1).reshape(n_pool, SB_WIDTH, page)
        cv_t = cache_v[l].transpose(0, 2, 3, 1).reshape(n_pool, SB_WIDTH, page)

        h1 = _ffn_call(hp, g1pre, w1gu, w1down, g1post)
        (k_t, v_t, kt_bf, v_bf, q_s0, q_s1, q_r, k_r, v_r, g_r, a_sb, a_r) = _inproj_prompt_call(
            h1, gmpre, win, w_kv_t, *rope_p, batch, seq)
        o_sb = _sb_prompt_call([q_s0.reshape(batch, seq, SB_WIDTH), q_s1.reshape(batch, seq, SB_WIDTH)],
                               kt_bf, v_bf.reshape(batch, seq, SB_WIDTH), bias, tri_strict)
        o_r, s_p = _ret_prompt_call(q_r.reshape(batch, seq, -1), k_r.reshape(batch, seq, -1),
                                    v_r.reshape(batch, seq, -1), g_r.reshape(batch, seq, -1),
                                    gn, decay_p)
        hp = _post_call(h1, o_sb.reshape(batch * seq, SB_WIDTH), o_r.reshape(batch * seq, RET_V_WIDTH),
                        a_sb, a_r, wsb, wret, wo, gmpost, g2pre, w2gu, w2down, g2post)
        kp_l.append(k_t.reshape(batch, SB_HEADS, SB_HEAD_DIM, seq).transpose(0, 3, 1, 2))
        vp_l.append(v_t.reshape(batch, SB_HEADS, SB_HEAD_DIM, seq).transpose(0, 3, 1, 2))
        sp_l.append(s_p)

        h1 = _ffn_call(hs, g1pre, w1gu, w1down, g1post)
        (k_sb, v_sb, q_sb, q_r, k_r, v_r, g_r, a_sb, a_r) = _inproj_decode_call(
            h1, gmpre, win, *rope_s)
        o_sb = _sb_decode_call(q_sb.reshape(n_seq, dec_t, SB_WIDTH), k_sb.reshape(n_seq, dec_t, SB_WIDTH),
                               v_sb.reshape(n_seq, dec_t, SB_WIDTH), ck_t, cv_t, page_table, bias_rows, tri2)
        o_r, s_s = _ret_decode_call(q_r, k_r, v_r, g_r, gn, state_ret[l], decay_s, dec_t)
        hs = _post_call(h1, o_sb.reshape(n_seq * dec_t, SB_WIDTH).astype(BF16), o_r, a_sb, a_r,
                        wsb, wret, wo, gmpost, g2pre, w2gu, w2down, g2post)
        ks_l.append(k_sb.reshape(n_seq, dec_t, SB_HEADS, SB_HEAD_DIM))
        vs_l.append(v_sb.reshape(n_seq, dec_t, SB_HEADS, SB_HEAD_DIM))
        ss_l.append(s_s)

    return (hp.reshape(batch, seq, d), hs.reshape(n_seq, dec_t, d),
            jnp.stack(kp_l), jnp.stack(vp_l), jnp.stack(sp_l),
            jnp.stack(ks_l), jnp.stack(vs_l), jnp.stack(ss_l))
```
